```python
import math
import jax, jax.numpy as jnp
from jax import lax
import numpy as np

D_MODEL = 2048
BATCH = 4
SEQ = 2048
DEPTH = 4
DEC_BATCH = 32
DEC_SEQ = 4
PAST_LEN = 16384
PAGE_SIZE = 128

MIX_WIDTH = D_MODEL
SWA_WIDTH = MIX_WIDTH // 2
RET_WIDTH = MIX_WIDTH - SWA_WIDTH
SWA_HEADS = 8
SWA_KV_HEADS = 2
SWA_HEAD_DIM = SWA_WIDTH // SWA_HEADS
WINDOW = 128
BLOCK = WINDOW
RET_HEADS = 4
RET_KEY_DIM = RET_WIDTH // RET_HEADS
RET_VAL_DIM = RET_WIDTH // RET_HEADS
RET_CHUNK = 128
ROPE_BASE = 10000.0
N_BUCKETS = 32
MAX_DISTANCE = 128
EPS = 1e-6
_FF_RAW = -(-8 * D_MODEL // 3)
D_FF = -(-_FF_RAW // 256) * 256

_SEG = (SWA_HEADS * SWA_HEAD_DIM, SWA_KV_HEADS * SWA_HEAD_DIM, SWA_KV_HEADS * SWA_HEAD_DIM,
        RET_HEADS * RET_KEY_DIM, RET_HEADS * RET_KEY_DIM, RET_HEADS * RET_VAL_DIM, RET_WIDTH)
IN_COLS = sum(_SEG)
SPLITS = tuple(int(s) for s in np.cumsum(_SEG)[:-1])

kernel_name = "hymba_swa_sink_retention_decode_step"


def _rmsnorm(x, g):
    xf = x.astype(jnp.float32)
    y = xf * lax.rsqrt(jnp.mean(xf * xf, axis=-1, keepdims=True) + EPS)
    return (y * g.astype(jnp.float32)).astype(x.dtype)


def _t5_bucket(dist):
    n = jnp.maximum(dist, 0)
    max_exact = N_BUCKETS // 2
    nf = jnp.maximum(n, 1).astype(jnp.float32)
    large = max_exact + (jnp.log(nf / max_exact) / math.log(MAX_DISTANCE / max_exact)
                         * (N_BUCKETS - max_exact)).astype(jnp.int32)
    large = jnp.minimum(large, N_BUCKETS - 1)
    return jnp.where(n < max_exact, n, large)


def _rotary(x, pos):
    half = x.shape[-1] // 2
    inv = 1.0 / (ROPE_BASE ** jnp.linspace(0.0, 1.0, half, dtype=jnp.float32))
    ang = pos.astype(jnp.float32)[:, None] * inv[None, :]
    cos = jnp.cos(ang)[None, :, None, :]
    sin = jnp.sin(ang)[None, :, None, :]
    xf = x.astype(jnp.float32)
    x1, x2 = xf[..., :half], xf[..., half:]
    return jnp.concatenate([x1 * cos - x2 * sin, x1 * sin + x2 * cos], axis=-1)


def _log_gamma():
    return jnp.log(1.0 - jnp.exp2(-5.0 - jnp.arange(RET_HEADS, dtype=jnp.float32)))


def _project(xn, w_in, pos):
    B, T, _ = xn.shape
    h = jnp.einsum('btd,dc->btc', xn, w_in)
    q_s, k_s, v_s, q_r, k_r, v_r, g_r = jnp.split(h, SPLITS, axis=-1)
    q_s = q_s.reshape(B, T, SWA_HEADS, SWA_HEAD_DIM)
    k_s = k_s.reshape(B, T, SWA_KV_HEADS, SWA_HEAD_DIM)
    v_s = v_s.reshape(B, T, SWA_KV_HEADS, SWA_HEAD_DIM)
    q_r = _rotary(q_r.reshape(B, T, RET_HEADS, RET_KEY_DIM), pos)
    k_r = _rotary(k_r.reshape(B, T, RET_HEADS, RET_KEY_DIM), pos) * (RET_KEY_DIM ** -0.5)
    v_r = v_r.reshape(B, T, RET_HEADS, RET_VAL_DIM).astype(jnp.float32)
    return q_s, k_s, v_s, q_r, k_r, v_r, g_r


def _swa_attend(q, k, v, q_pos, k_pos, sinks, rel_bias):
    B, NB, Q, H, hd = q.shape
    K = k.shape[2]
    G = H // SWA_KV_HEADS
    qg = q.reshape(B, NB, Q, SWA_KV_HEADS, G, hd)
    s = jnp.einsum('bnqkgd,bnjkd->bnkgqj', qg, k).astype(jnp.float32) * (hd ** -0.5)
    delta = q_pos[:, :, None] - k_pos[:, None, :]
    valid = (delta >= 0) & (delta < WINDOW) & (k_pos[:, None, :] >= 0)
    bias = rel_bias.astype(jnp.float32)[_t5_bucket(delta)]
    bias = bias.transpose(0, 3, 1, 2).reshape(NB, SWA_KV_HEADS, G, Q, K)
    s = jnp.where(valid[None, :, None, None], s + bias[None], -1e30)
    sink = sinks.astype(jnp.float32).reshape(SWA_KV_HEADS, G)[None, None, :, :, None, None]
    m = jnp.maximum(jnp.max(s, axis=-1, keepdims=True), sink)
    p = jnp.exp(s - m)
    p = p / (jnp.sum(p, axis=-1, keepdims=True) + jnp.exp(sink - m))
    o = jnp.einsum('bnkgqj,bnjkd->bnqkgd', p.astype(v.dtype), v)
    return o.reshape(B, NB, Q, H * hd)


def _retention_chunk(state, q, k, v):
    C = q.shape[1]
    lg = _log_gamma()
    idx = jnp.arange(C, dtype=jnp.float32)
    diff = idx[:, None] - idx[None, :]
    dmask = jnp.where(diff[None] >= 0, jnp.exp(jnp.maximum(diff, 0.0)[None] * lg[:, None, None]), 0.0)
    s = jnp.einsum('bnhd,bmhd->bhnm', q, k) * dmask[None]
    o = jnp.einsum('bhnm,bmhv->bnhv', s, v)
    q_decay = jnp.exp((idx[:, None] + 1.0) * lg[None, :])
    o = o + jnp.einsum('bnhd,bhdv->bnhv', q, state) * q_decay[None, :, :, None]
    k_decay = jnp.exp((C - 1.0 - idx)[:, None] * lg[None, :])
    new_state = (jnp.exp(C * lg)[None, :, None, None] * state
                 + jnp.einsum('bmhd,bmhv->bhdv', k * k_decay[None, :, :, None], v))
    return new_state, o


def _retention_prompt(q, k, v):
    B, S, H, dk = q.shape
    dv = v.shape[-1]
    nc = S // RET_CHUNK

    def to_chunks(a):
        return a.reshape(B, nc, RET_CHUNK, H, a.shape[-1]).transpose(1, 0, 2, 3, 4)

    s0 = jnp.zeros((B, H, dk, dv), jnp.float32)
    s_fin, o = lax.scan(lambda st, xs: _retention_chunk(st, *xs), s0,
                        (to_chunks(q), to_chunks(k), to_chunks(v)))
    o = o.transpose(1, 0, 2, 3, 4).reshape(B, S, H, dv)
    return o, s_fin


def _mix_out(x, o_swa, o_ret, g_r, w_o):
    B, T = x.shape[:2]
    o_ret = o_ret * lax.rsqrt(jnp.mean(o_ret * o_ret, axis=-1, keepdims=True) + EPS)
    o_ret = o_ret.reshape(B, T, RET_WIDTH) * jax.nn.silu(g_r.astype(jnp.float32))
    cat = jnp.concatenate([o_swa.astype(x.dtype), o_ret.astype(x.dtype)], axis=-1)
    return x + jnp.einsum('btc,cd->btd', cat, w_o)


def _ffn(x, g, w_gate_up, w_down):
    xn = _rmsnorm(x, g)
    a, b = jnp.split(jnp.einsum('btd,df->btf', xn, w_gate_up), 2, axis=-1)
    return x + jnp.einsum('btf,fd->btd', jax.nn.silu(a) * b, w_down)


def setup_inputs(seed: int = 0) -> dict:
    key = jax.random.key(seed)
    ks = jax.random.split(key, 16)
    f32 = jnp.float32
    kv_shape = (DEPTH, DEC_BATCH, WINDOW, SWA_KV_HEADS, SWA_HEAD_DIM)
    return {
        "x_prompt": jax.random.normal(ks[0], (BATCH, SEQ, D_MODEL), f32),
        "x_sample": jax.random.normal(ks[1], (DEC_BATCH, DEC_SEQ, D_MODEL), f32),
        "cache_k_win": jax.random.normal(ks[2], kv_shape, f32),
        "cache_v_win": jax.random.normal(ks[3], kv_shape, f32),
        "state_ret": 0.5 * jax.random.normal(ks[4], (DEPTH, DEC_BATCH, RET_HEADS, RET_KEY_DIM, RET_VAL_DIM), f32),
        "rel_bias": 0.5 * jax.random.normal(ks[5], (N_BUCKETS, SWA_HEADS), f32),
        "w_in": jax.random.normal(ks[6], (DEPTH, D_MODEL, IN_COLS), f32) * D_MODEL ** -0.5,
        "sinks": 0.5 * jax.random.normal(ks[7], (DEPTH, SWA_HEADS), f32),
        "w_o": jax.random.normal(ks[8], (DEPTH, MIX_WIDTH, D_MODEL), f32) * MIX_WIDTH ** -0.5,
        "norm_mix": 1.0 + 0.05 * jax.random.normal(ks[9], (DEPTH, D_MODEL), f32),
        "norm_ffn": 1.0 + 0.05 * jax.random.normal(ks[10], (DEPTH, D_MODEL), f32),
        "w_gate_up": jax.random.normal(ks[11], (DEPTH, D_MODEL, 2 * D_FF), f32) * D_MODEL ** -0.5,
        "w_down": jax.random.normal(ks[12], (DEPTH, D_FF, D_MODEL), f32) * D_FF ** -0.5,
        "norm_final": 1.0 + 0.05 * jax.random.normal(ks[13], (D_MODEL,), f32),
    }


def reference(x_prompt, x_sample, cache_k_win, cache_v_win, state_ret, rel_bias, w_in, sinks,
              w_o, norm_mix, norm_ffn, w_gate_up, w_down, norm_final):
    B, S, _ = x_prompt.shape
    DB, T, _ = x_sample.shape
    nblk = S // BLOCK
    pos_p = jnp.arange(S, dtype=jnp.int32)
    qpos_p = pos_p.reshape(nblk, BLOCK)
    kpos_p = jnp.concatenate([qpos_p - BLOCK, qpos_p], axis=1)
    pos_s = PAST_LEN + jnp.arange(T, dtype=jnp.int32)
    qpos_s = pos_s[None]
    kpos_s = jnp.concatenate([PAST_LEN - WINDOW + jnp.arange(WINDOW, dtype=jnp.int32), pos_s])[None]

    hp, hs = x_prompt, x_sample
    kp_new, vp_new, rp_new, ks_new, vs_new, rs_new = [], [], [], [], [], []
    for l in range(DEPTH):
        xn = _rmsnorm(hp, norm_mix[l])
        q_s, k_s, v_s, q_r, k_r, v_r, g_r = _project(xn, w_in[l], pos_p)
        kb = k_s.reshape(B, nblk, BLOCK, SWA_KV_HEADS, SWA_HEAD_DIM)
        vb = v_s.reshape(B, nblk, BLOCK, SWA_KV_HEADS, SWA_HEAD_DIM)
        k_band = jnp.concatenate([jnp.concatenate([jnp.zeros_like(kb[:, :1]), kb[:, :-1]], axis=1), kb], axis=2)
        v_band = jnp.concatenate([jnp.concatenate([jnp.zeros_like(vb[:, :1]), vb[:, :-1]], axis=1), vb], axis=2)
        o_swa = _swa_attend(q_s.reshape(B, nblk, BLOCK, SWA_HEADS, SWA_HEAD_DIM), k_band, v_band,
                            qpos_p, kpos_p, sinks[l], rel_bias).reshape(B, S, SWA_WIDTH)
        o_ret, st_p = _retention_prompt(q_r, k_r, v_r)
        hp = _mix_out(hp, o_swa, o_ret, g_r, w_o[l])
        hp = _ffn(hp, norm_ffn[l], w_gate_up[l], w_down[l])
        kp_new.append(k_s[:, S - WINDOW:])
        vp_new.append(v_s[:, S - WINDOW:])
        rp_new.append(st_p.astype(x_prompt.dtype))

        xn = _rmsnorm(hs, norm_mix[l])
        q_s, k_s, v_s, q_r, k_r, v_r, g_r = _project(xn, w_in[l], pos_s)
        k_all = jnp.concatenate([cache_k_win[l].astype(k_s.dtype), k_s], axis=1)
        v_all = jnp.concatenate([cache_v_win[l].astype(v_s.dtype), v_s], axis=1)
        o_swa = _swa_attend(q_s[:, None], k_all[:, None], v_all[:, None], qpos_s, kpos_s,
                            sinks[l], rel_bias).reshape(DB, T, SWA_WIDTH)
        st_s, o_ret = _retention_chunk(state_ret[l].astype(jnp.float32), q_r, k_r, v_r)
        hs = _mix_out(hs, o_swa, o_ret, g_r, w_o[l])
        hs = _ffn(hs, norm_ffn[l], w_gate_up[l], w_down[l])
        ks_new.append(k_all[:, -WINDOW:].astype(cache_k_win.dtype))
        vs_new.append(v_all[:, -WINDOW:].astype(cache_v_win.dtype))
        rs_new.append(st_s.astype(state_ret.dtype))

    y_prompt = _rmsnorm(hp, norm_final)
    y_sample = _rmsnorm(hs, norm_final)
    return (y_prompt, y_sample,
            jnp.stack(kp_new), jnp.stack(vp_new), jnp.stack(rp_new),
            jnp.stack(ks_new), jnp.stack(vs_new), jnp.stack(rs_new))
```

```python
import functools
import math

import numpy as np
import jax
import jax.numpy as jnp
from jax import lax
from jax.experimental import pallas as pl
from jax.experimental.pallas import tpu as pltpu

D_MODEL = 2048
BATCH = 4
SEQ = 2048
DEPTH = 4
DEC_BATCH = 32
DEC_SEQ = 4
PAST_LEN = 16384

SWA_WIDTH = 1024
RET_WIDTH = 1024
SWA_HEADS = 8
SWA_KV_HEADS = 2
SWA_GROUP = SWA_HEADS // SWA_KV_HEADS
SWA_HEAD_DIM = 128
WINDOW = 128
BLOCK = WINDOW
RET_HEADS = 4
RET_KEY_DIM = 256
RET_VAL_DIM = 256
RET_CHUNK = 128
ROPE_BASE = 10000.0
N_BUCKETS = 32
MAX_DISTANCE = 128
EPS = 1e-6
D_FF = 5632
IN_COLS = 5632

F32 = jnp.float32
BF16 = jnp.bfloat16

VMEM_LIMIT_BYTES = 56 * 1024 * 1024

PROJ_TN = 512
N_PROJ_TILES = IN_COLS // PROJ_TN
PROJ_BLOCK_OF_TILE = (0, 1, 1, 2, 3, 3, 3, 4, 5, 5, 5)
SIDE_BLOCK_OF_TILE = (4, 4, 4, 4, 4, 0, 1, 1, 1, 2, 3)
PROJ_COLS = 3072
SIDE_COLS = 2560


def _rms_scale(x):
    return x * lax.rsqrt(jnp.mean(x * x, axis=-1, keepdims=True) + EPS)


def _silu(x):
    return x * jax.nn.sigmoid(x)


def _dot(a, b):
    return jnp.dot(a, b, preferred_element_type=F32)


def _dot_nt(a, b):
    return lax.dot_general(a, b, (((1,), (1,)), ((), ())), preferred_element_type=F32)


def _dot_tn(a, b):
    return lax.dot_general(a, b, (((0,), (0,)), ((), ())), preferred_element_type=F32)


def _bias_kernel(rbt_ref, idx_ref, out_ref):
    h = pl.program_id(0)
    idx = idx_ref[...]
    acc = jnp.zeros(idx.shape, F32)
    for b in range(N_BUCKETS):
        acc = jnp.where(idx == b, rbt_ref[h, b], acc)
    out_ref[0] = acc


def _expand_bias(rel_bias_t, bucket_idx):
    rows, cols = bucket_idx.shape
    return pl.pallas_call(
        _bias_kernel,
        grid=(SWA_HEADS,),
        in_specs=[pl.BlockSpec(memory_space=pltpu.SMEM),
                  pl.BlockSpec((rows, cols), lambda h: (0, 0))],
        out_specs=pl.BlockSpec((1, rows, cols), lambda h: (h, 0, 0)),
        out_shape=jax.ShapeDtypeStruct((SWA_HEADS, rows, cols), F32),
        name="bias_expand",
    )(rel_bias_t, bucket_idx)


def _t5_bucket_np(delta):
    n = np.maximum(delta, 0)
    max_exact = N_BUCKETS // 2
    nf = np.maximum(n, 1).astype(np.float64)
    large = max_exact + (np.log(nf / max_exact) / math.log(MAX_DISTANCE / max_exact)
                         * (N_BUCKETS - max_exact)).astype(np.int32)
    large = np.minimum(large, N_BUCKETS - 1)
    return np.where(n < max_exact, n, large).astype(np.int32)


def _proj_kernel(pblk_ref, sblk_ref, h_ref, g_ref, w_ref, cos_ref, sin_ref,
                 proj_ref, side_ref, xn_ref, acc_ref):
    del pblk_ref, sblk_ref
    j = pl.program_id(1)

    @pl.when(j == 0)
    def _():
        xn_ref[...] = (_rms_scale(h_ref[...]) * g_ref[...]).astype(BF16)

    acc_ref[...] = _dot(xn_ref[...], w_ref[...])

    def rotary(out_ref, scale):
        cos = cos_ref[...]
        sin = sin_ref[...]
        half = RET_KEY_DIM // 2
        for hh in range(PROJ_TN // RET_KEY_DIM):
            c0 = hh * RET_KEY_DIM
            x1 = acc_ref[:, c0:c0 + half]
            x2 = acc_ref[:, c0 + half:c0 + RET_KEY_DIM]
            o1 = x1 * cos - x2 * sin
            o2 = x1 * sin + x2 * cos
            if scale is not None:
                o1 = o1 * scale
                o2 = o2 * scale
            out_ref[:, c0:c0 + half] = o1.astype(out_ref.dtype)
            out_ref[:, c0 + half:c0 + RET_KEY_DIM] = o2.astype(out_ref.dtype)

    @pl.when((j < 2) | (j == 7) | (j == 8))
    def _():
        proj_ref[...] = acc_ref[...].astype(BF16)

    @pl.when((j == 2) | (j >= 9))
    def _():
        side_ref[...] = acc_ref[...]

    @pl.when((j == 3) | (j == 4))
    def _():
        rotary(proj_ref, None)

    @pl.when((j == 5) | (j == 6))
    def _():
        rotary(side_ref, RET_KEY_DIM ** -0.5)


def _proj_call(h, g, w_in_bf, layer, cos, sin, tm):
    m = h.shape[0]
    cos_blocks = cos.shape[0] // tm
    grid_spec = pltpu.PrefetchScalarGridSpec(
        num_scalar_prefetch=2,
        grid=(m // tm, N_PROJ_TILES),
        in_specs=[
            pl.BlockSpec((tm, D_MODEL), lambda i, j, pb, sb: (i, 0)),
            pl.BlockSpec((None, 1, D_MODEL), lambda i, j, pb, sb: (layer, 0, 0)),
            pl.BlockSpec((None, D_MODEL, PROJ_TN), lambda i, j, pb, sb: (layer, 0, j)),
            pl.BlockSpec((tm, RET_KEY_DIM // 2), lambda i, j, pb, sb: (i % cos_blocks, 0)),
            pl.BlockSpec((tm, RET_KEY_DIM // 2), lambda i, j, pb, sb: (i % cos_blocks, 0)),
        ],
        out_specs=[
            pl.BlockSpec((tm, PROJ_TN), lambda i, j, pb, sb: (i, pb[j])),
            pl.BlockSpec((tm, PROJ_TN), lambda i, j, pb, sb: (i, sb[j])),
        ],
        scratch_shapes=[pltpu.VMEM((tm, D_MODEL), BF16), pltpu.VMEM((tm, PROJ_TN), F32)],
    )
    return pl.pallas_call(
        _proj_kernel,
        grid_spec=grid_spec,
        out_shape=[jax.ShapeDtypeStruct((m, PROJ_COLS), BF16),
                   jax.ShapeDtypeStruct((m, SIDE_COLS), F32)],
        compiler_params=pltpu.CompilerParams(
            dimension_semantics=("arbitrary", "arbitrary"), vmem_limit_bytes=VMEM_LIMIT_BYTES),
        name="proj",
    )(jnp.asarray(PROJ_BLOCK_OF_TILE, jnp.int32), jnp.asarray(SIDE_BLOCK_OF_TILE, jnp.int32),
      h, g, w_in_bf, cos, sin)


def _softmax_sink(s, sink):
    m = jnp.maximum(jnp.max(s, axis=-1, keepdims=True), sink)
    p = jnp.exp(s - m)
    return p / (jnp.sum(p, axis=-1, keepdims=True) + jnp.exp(sink - m))


def _gate_out(o, g):
    return (_rms_scale(o) * _silu(g)).astype(BF16)


def _mix_prompt_kernel(sink_ref, cdec_ref, qs_ref, qr_ref, vr_ref, kr_ref, g_ref, kvp_ref, kvc_ref,
                       bias_ref, dmask_ref, qdec_ref, kdec_ref, cat_ref, st_ref):
    n = pl.program_id(1)

    @pl.when(n == 0)
    def _():
        st_ref[...] = jnp.zeros(st_ref.shape, F32)

    qi = lax.broadcasted_iota(jnp.int32, (BLOCK, 2 * BLOCK), 0)
    kj = lax.broadcasted_iota(jnp.int32, (BLOCK, 2 * BLOCK), 1)
    delta = qi + BLOCK - kj
    valid = (delta >= 0) & (delta < WINDOW) & ((kj >= BLOCK) | (n > 0))
    scale = SWA_HEAD_DIM ** -0.5
    v_off = SWA_KV_HEADS * SWA_HEAD_DIM
    for kh in range(SWA_KV_HEADS):
        c0 = kh * SWA_HEAD_DIM
        k_cat = jnp.concatenate([kvp_ref[:, c0:c0 + SWA_HEAD_DIM], kvc_ref[:, c0:c0 + SWA_HEAD_DIM]],
                                axis=0).astype(BF16)
        v_cat = jnp.concatenate([kvp_ref[:, v_off + c0:v_off + c0 + SWA_HEAD_DIM],
                                 kvc_ref[:, v_off + c0:v_off + c0 + SWA_HEAD_DIM]], axis=0).astype(BF16)
        for gq in range(SWA_GROUP):
            h = kh * SWA_GROUP + gq
            q = qs_ref[:, h * SWA_HEAD_DIM:(h + 1) * SWA_HEAD_DIM]
            s = _dot_nt(q, k_cat) * scale
            s = jnp.where(valid, s + bias_ref[h], -1e30)
            p = _softmax_sink(s, sink_ref[h])
            o = _dot(p.astype(BF16), v_cat)
            cat_ref[:, h * SWA_HEAD_DIM:(h + 1) * SWA_HEAD_DIM] = o.astype(BF16)

    for h in range(RET_HEADS):
        cs = slice(h * RET_KEY_DIM, (h + 1) * RET_KEY_DIM)
        q = qr_ref[:, cs]
        k32 = kr_ref[:, cs]
        v = vr_ref[:, cs]
        s = _dot_nt(q, k32.astype(BF16)) * dmask_ref[h]
        o = _dot(s.astype(BF16), v)
        st = st_ref[0, h]
        o = o + _dot(q, st.astype(BF16)) * qdec_ref[:, cs]
        kd = (k32 * kdec_ref[:, cs]).astype(BF16)
        st_ref[0, h] = cdec_ref[h] * st + _dot_tn(kd, v)
        cat_ref[:, SWA_WIDTH + h * RET_VAL_DIM:SWA_WIDTH + (h + 1) * RET_VAL_DIM] = _gate_out(o, g_ref[:, cs])


def _mix_prompt_call(proj, side, sinks_l, cdec, bias, dmask, qdec, kdec):
    nblk = SEQ // BLOCK
    row = lambda b, n: b * nblk + n
    smem = pl.BlockSpec(memory_space=pltpu.SMEM)
    full = lambda shape: pl.BlockSpec(shape, lambda b, n: (0,) * len(shape))
    return pl.pallas_call(
        _mix_prompt_kernel,
        grid=(BATCH, nblk),
        in_specs=[
            smem, smem,
            pl.BlockSpec((BLOCK, 1024), lambda b, n: (row(b, n), 0)),
            pl.BlockSpec((BLOCK, 1024), lambda b, n: (row(b, n), 1)),
            pl.BlockSpec((BLOCK, 1024), lambda b, n: (row(b, n), 2)),
            pl.BlockSpec((BLOCK, 1024), lambda b, n: (row(b, n), 0)),
            pl.BlockSpec((BLOCK, 1024), lambda b, n: (row(b, n), 1)),
            pl.BlockSpec((BLOCK, 512), lambda b, n: (row(b, jnp.maximum(n - 1, 0)), 4)),
            pl.BlockSpec((BLOCK, 512), lambda b, n: (row(b, n), 4)),
            full((SWA_HEADS, BLOCK, 2 * BLOCK)),
            full((RET_HEADS, RET_CHUNK, RET_CHUNK)),
            full((RET_CHUNK, RET_WIDTH)),
            full((RET_CHUNK, RET_WIDTH)),
        ],
        out_specs=[
            pl.BlockSpec((BLOCK, D_MODEL), lambda b, n: (row(b, n), 0)),
            pl.BlockSpec((1, RET_HEADS, RET_KEY_DIM, RET_VAL_DIM), lambda b, n: (b, 0, 0, 0)),
        ],
        out_shape=[jax.ShapeDtypeStruct((BATCH * SEQ, D_MODEL), BF16),
                   jax.ShapeDtypeStruct((BATCH, RET_HEADS, RET_KEY_DIM, RET_VAL_DIM), F32)],
        compiler_params=pltpu.CompilerParams(
            dimension_semantics=("arbitrary", "arbitrary"), vmem_limit_bytes=VMEM_LIMIT_BYTES),
        name="mix_prompt",
    )(sinks_l, cdec, proj, proj, proj, side, side, side, side, bias, dmask, qdec, kdec)


SAMPLE_BB = 4
SAMPLE_ROWS = SAMPLE_BB * DEC_SEQ
N_SAMPLE = DEC_BATCH * DEC_SEQ


def _mix_sample_kernel(sink_ref, cdec_ref, prow_ref, srow_ref, pall_ref, sall_ref, ck_ref, cv_ref, st_ref,
                       bias_ref, valid_ref, dmask_ref, qdec_ref, kdec_ref, cat_ref, stout_ref):
    r = SAMPLE_ROWS
    row_b = lax.broadcasted_iota(jnp.int32, (r, 1), 0) // DEC_SEQ
    row_b4 = lax.broadcasted_iota(jnp.int32, (SWA_GROUP * r, 1), 0) % r // DEC_SEQ
    scale = SWA_HEAD_DIM ** -0.5
    v_off = SWA_KV_HEADS * SWA_HEAD_DIM
    kv_col = 4 * PROJ_TN
    valid = valid_ref[...] > 0.5

    for kh in range(SWA_KV_HEADS):
        c0 = kh * SWA_HEAD_DIM
        q4 = jnp.concatenate(
            [prow_ref[:, (kh * SWA_GROUP + gq) * SWA_HEAD_DIM:(kh * SWA_GROUP + gq + 1) * SWA_HEAD_DIM]
             for gq in range(SWA_GROUP)], axis=0)
        k_new = sall_ref[:, kv_col + c0:kv_col + c0 + SWA_HEAD_DIM].astype(BF16)
        v_new = sall_ref[:, kv_col + v_off + c0:kv_col + v_off + c0 + SWA_HEAD_DIM].astype(BF16)
        s_cache = jnp.zeros((SWA_GROUP * r, WINDOW), F32)
        for bi in range(SAMPLE_BB):
            k_c = ck_ref[bi, :, c0:c0 + SWA_HEAD_DIM].astype(BF16)
            s_cache = jnp.where(row_b4 == bi, _dot_nt(q4, k_c), s_cache)
        s_new = _dot_nt(q4, k_new)
        s4 = jnp.concatenate([s_cache, s_new], axis=1) * scale
        p_parts = []
        for gq in range(SWA_GROUP):
            h = kh * SWA_GROUP + gq
            s = s4[gq * r:(gq + 1) * r]
            s = jnp.where(valid, s + bias_ref[h], -1e30)
            p_parts.append(_softmax_sink(s, sink_ref[h]))
        p4 = jnp.concatenate(p_parts, axis=0)
        p_cache = p4[:, :WINDOW]
        o4 = _dot(p4[:, WINDOW:].astype(BF16), v_new)
        for bi in range(SAMPLE_BB):
            v_c = cv_ref[bi, :, c0:c0 + SWA_HEAD_DIM].astype(BF16)
            o4 = o4 + _dot(jnp.where(row_b4 == bi, p_cache, 0.0).astype(BF16), v_c)
        for gq in range(SWA_GROUP):
            h = kh * SWA_GROUP + gq
            cat_ref[:, h * SWA_HEAD_DIM:(h + 1) * SWA_HEAD_DIM] = o4[gq * r:(gq + 1) * r].astype(BF16)

    for h in range(RET_HEADS):
        cs = slice(h * RET_KEY_DIM, (h + 1) * RET_KEY_DIM)
        q = prow_ref[:, 1024 + h * RET_KEY_DIM:1024 + (h + 1) * RET_KEY_DIM]
        k_all = sall_ref[:, cs].astype(BF16)
        v_all = pall_ref[:, 2048 + h * RET_VAL_DIM:2048 + (h + 1) * RET_VAL_DIM]
        s = _dot_nt(q, k_all) * dmask_ref[h]
        o = _dot(s.astype(BF16), v_all)
        k32 = srow_ref[:, cs]
        v = prow_ref[:, 2048 + h * RET_VAL_DIM:2048 + (h + 1) * RET_VAL_DIM]
        kd = k32 * kdec_ref[:, cs]
        cross = jnp.zeros((r, RET_VAL_DIM), F32)
        for bi in range(SAMPLE_BB):
            st = st_ref[bi, h]
            cross = jnp.where(row_b == bi, _dot(q, st.astype(BF16)), cross)
            kd_b = jnp.where(row_b == bi, kd, 0.0).astype(BF16)
            stout_ref[bi, h] = cdec_ref[h] * st + _dot_tn(kd_b, v)
        o = o + cross * qdec_ref[:, cs]
        g = srow_ref[:, 1024 + h * RET_VAL_DIM:1024 + (h + 1) * RET_VAL_DIM]
        cat_ref[:, SWA_WIDTH + h * RET_VAL_DIM:SWA_WIDTH + (h + 1) * RET_VAL_DIM] = _gate_out(o, g)


def _mix_sample_call(proj, side, cache_k, cache_v, state_ret, layer, sinks_l, cdec, bias, valid, dmask, qdec, kdec):
    r = SAMPLE_ROWS
    smem = pl.BlockSpec(memory_space=pltpu.SMEM)
    full = lambda shape: pl.BlockSpec(shape, lambda c: (0,) * len(shape))
    return pl.pallas_call(
        _mix_sample_kernel,
        grid=(DEC_BATCH // SAMPLE_BB,),
        in_specs=[
            smem, smem,
            pl.BlockSpec((r, PROJ_COLS), lambda c: (c, 0)),
            pl.BlockSpec((r, SIDE_COLS), lambda c: (c, 0)),
            full((N_SAMPLE, PROJ_COLS)),
            full((N_SAMPLE, SIDE_COLS)),
            pl.BlockSpec((None, SAMPLE_BB, WINDOW, 256), lambda c: (layer, c, 0, 0)),
            pl.BlockSpec((None, SAMPLE_BB, WINDOW, 256), lambda c: (layer, c, 0, 0)),
            pl.BlockSpec((None, SAMPLE_BB, RET_HEADS, RET_KEY_DIM, RET_VAL_DIM), lambda c: (layer, c, 0, 0, 0)),
            pl.BlockSpec((SWA_HEADS, r, 2 * WINDOW), lambda c: (0, c, 0)),
            pl.BlockSpec((r, 2 * WINDOW), lambda c: (c, 0)),
            pl.BlockSpec((RET_HEADS, r, N_SAMPLE), lambda c: (0, c, 0)),
            pl.BlockSpec((r, RET_WIDTH), lambda c: (c, 0)),
            pl.BlockSpec((r, RET_WIDTH), lambda c: (c, 0)),
        ],
        out_specs=[
            pl.BlockSpec((r, D_MODEL), lambda c: (c, 0)),
            pl.BlockSpec((SAMPLE_BB, RET_HEADS, RET_KEY_DIM, RET_VAL_DIM), lambda c: (c, 0, 0, 0)),
        ],
        out_shape=[jax.ShapeDtypeStruct((N_SAMPLE, D_MODEL), BF16),
                   jax.ShapeDtypeStruct((DEC_BATCH, RET_HEADS, RET_KEY_DIM, RET_VAL_DIM), F32)],
        compiler_params=pltpu.CompilerParams(
            dimension_semantics=("arbitrary",), vmem_limit_bytes=VMEM_LIMIT_BYTES),
        name="mix_sample",
    )(sinks_l, cdec, proj, side, proj, side, cache_k, cache_v, state_ret, bias, valid, dmask, qdec, kdec)


WO_TN = 512


def _wo_kernel(cat_ref, w_ref, h_ref, out_ref):
    out_ref[...] = h_ref[...] + _dot(cat_ref[...], w_ref[...])


def _wo_call(cat, w_o_bf, layer, h, tm):
    m = h.shape[0]
    return pl.pallas_call(
        _wo_kernel,
        grid=(m // tm, D_MODEL // WO_TN),
        in_specs=[
            pl.BlockSpec((tm, D_MODEL), lambda i, j: (i, 0)),
            pl.BlockSpec((None, D_MODEL, WO_TN), lambda i, j: (layer, 0, j)),
            pl.BlockSpec((tm, WO_TN), lambda i, j: (i, j)),
        ],
        out_specs=pl.BlockSpec((tm, WO_TN), lambda i, j: (i, j)),
        out_shape=jax.ShapeDtypeStruct((m, D_MODEL), F32),
        compiler_params=pltpu.CompilerParams(
            dimension_semantics=("arbitrary", "arbitrary"), vmem_limit_bytes=VMEM_LIMIT_BYTES),
        name="wo",
    )(cat, w_o_bf, h)


def _ffn_kernel(h_ref, g_ref, wg_ref, wu_ref, wd_ref, out_ref, xn_ref):
    f = pl.program_id(1)

    @pl.when(f == 0)
    def _():
        x = h_ref[...]
        xn_ref[...] = (_rms_scale(x) * g_ref[...]).astype(BF16)
        out_ref[...] = x

    xn = xn_ref[...]
    a = _dot(xn, wg_ref[...])
    b = _dot(xn, wu_ref[...])
    act = (_silu(a) * b).astype(BF16)
    out_ref[...] += _dot(act, wd_ref[...])


def _ffn_call(h, g, w_gu_bf, w_d_bf, layer, tm, tf):
    m = h.shape[0]
    nf = D_FF // tf
    return pl.pallas_call(
        _ffn_kernel,
        grid=(m // tm, nf),
        in_specs=[
            pl.BlockSpec((tm, D_MODEL), lambda i, f: (i, 0)),
            pl.BlockSpec((None, 1, D_MODEL), lambda i, f: (layer, 0, 0)),
            pl.BlockSpec((None, D_MODEL, tf), lambda i, f: (layer, 0, f)),
            pl.BlockSpec((None, D_MODEL, tf), lambda i, f: (layer, 0, f + nf)),
            pl.BlockSpec((None, tf, D_MODEL), lambda i, f: (layer, f, 0)),
        ],
        out_specs=pl.BlockSpec((tm, D_MODEL), lambda i, f: (i, 0)),
        out_shape=jax.ShapeDtypeStruct((m, D_MODEL), F32),
        scratch_shapes=[pltpu.VMEM((tm, D_MODEL), BF16)],
        compiler_params=pltpu.CompilerParams(
            dimension_semantics=("arbitrary", "arbitrary"), vmem_limit_bytes=VMEM_LIMIT_BYTES),
        name="ffn",
    )(h, g, w_gu_bf, w_gu_bf, w_d_bf)


def _norm_kernel(h_ref, g_ref, out_ref):
    out_ref[...] = _rms_scale(h_ref[...]) * g_ref[...]


def _norm_call(h, g, tm):
    m = h.shape[0]
    return pl.pallas_call(
        _norm_kernel,
        grid=(m // tm,),
        in_specs=[pl.BlockSpec((tm, D_MODEL), lambda i: (i, 0)),
                  pl.BlockSpec((1, D_MODEL), lambda i: (0, 0))],
        out_specs=pl.BlockSpec((tm, D_MODEL), lambda i: (i, 0)),
        out_shape=jax.ShapeDtypeStruct((m, D_MODEL), F32),
        name="final_norm",
    )(h, g)


def _rope_tables(pos):
    half = RET_KEY_DIM // 2
    inv = 1.0 / (ROPE_BASE ** jnp.linspace(0.0, 1.0, half, dtype=F32))
    ang = pos.astype(F32)[:, None] * inv[None, :]
    return jnp.cos(ang), jnp.sin(ang)


def _decay_tables(c):
    lg = jnp.log(1.0 - jnp.exp2(-5.0 - jnp.arange(RET_HEADS, dtype=F32)))
    idx = jnp.arange(c, dtype=F32)
    diff = idx[:, None] - idx[None, :]
    dmask = jnp.where(diff[None] >= 0, jnp.exp(jnp.maximum(diff, 0.0)[None] * lg[:, None, None]), 0.0)
    q_decay = jnp.exp((idx[:, None] + 1.0) * lg[None, :])
    k_decay = jnp.exp((c - 1.0 - idx)[:, None] * lg[None, :])
    c_decay = jnp.exp(c * lg)
    return dmask, q_decay, k_decay, c_decay


def _per_head_cols(t):
    return jnp.repeat(t, RET_KEY_DIM, axis=1)


def kernel(x_prompt, x_sample, cache_k_win, cache_v_win, state_ret, rel_bias, w_in, sinks, w_o,
           norm_mix, norm_ffn, w_gate_up, w_down, norm_final):
    w_in_bf = w_in.astype(BF16)
    w_o_bf = w_o.astype(BF16)
    w_gu_bf = w_gate_up.astype(BF16)
    w_d_bf = w_down.astype(BF16)
    norm_mix3 = norm_mix.reshape(DEPTH, 1, D_MODEL)
    norm_ffn3 = norm_ffn.reshape(DEPTH, 1, D_MODEL)
    cache_k = cache_k_win.reshape(DEPTH, DEC_BATCH, WINDOW, SWA_KV_HEADS * SWA_HEAD_DIM)
    cache_v = cache_v_win.reshape(DEPTH, DEC_BATCH, WINDOW, SWA_KV_HEADS * SWA_HEAD_DIM)

    cos_p, sin_p = _rope_tables(jnp.arange(SEQ, dtype=jnp.int32))
    cos_4, sin_4 = _rope_tables(PAST_LEN + jnp.arange(DEC_SEQ, dtype=jnp.int32))
    cos_s = jnp.tile(cos_4, (DEC_BATCH, 1))
    sin_s = jnp.tile(sin_4, (DEC_BATCH, 1))

    dmask_p, qd_p, kd_p, cdec_p = _decay_tables(RET_CHUNK)
    qdec_p = _per_head_cols(qd_p)
    kdec_p = _per_head_cols(kd_p)
    dmask_4, qd_4, kd_4, cdec_s = _decay_tables(DEC_SEQ)
    eye_b = jnp.eye(DEC_BATCH, dtype=F32)
    dmask_s = jax.vmap(lambda d: jnp.kron(eye_b, d))(dmask_4)
    qdec_s = jnp.tile(_per_head_cols(qd_4), (DEC_BATCH, 1))
    kdec_s = jnp.tile(_per_head_cols(kd_4), (DEC_BATCH, 1))

    qi = np.arange(BLOCK)[:, None]
    kj = np.arange(2 * BLOCK)[None, :]
    bucket_p = _t5_bucket_np(qi + BLOCK - kj)
    rows = np.arange(N_SAMPLE)
    rb, rt = rows // DEC_SEQ, rows % DEC_SEQ
    delta_cache = (WINDOW + rt)[:, None] - np.arange(WINDOW)[None, :]
    delta_new = rt[:, None] - rt[None, :]
    same_b = rb[:, None] == rb[None, :]
    bucket_s = _t5_bucket_np(np.concatenate([delta_cache, delta_new], axis=1))
    valid_s = np.concatenate([delta_cache < WINDOW, same_b & (delta_new >= 0)], axis=1).astype(np.float32)
    rel_bias_t = rel_bias.T
    bias_p = _expand_bias(rel_bias_t, jnp.asarray(bucket_p))
    bias_s = _expand_bias(rel_bias_t, jnp.asarray(bucket_s))
    valid_s = jnp.asarray(valid_s)

    hp = x_prompt.reshape(BATCH * SEQ, D_MODEL)
    hs = x_sample.reshape(N_SAMPLE, D_MODEL)
    tm_p, tm_s, tf = 1024, N_SAMPLE, 512
    kp_new, vp_new, rp_new, ks_new, vs_new, rs_new = [], [], [], [], [], []
    kv_col = 4 * PROJ_TN
    kv_w = SWA_KV_HEADS * SWA_HEAD_DIM
    for l in range(DEPTH):
        proj, side = _proj_call(hp, norm_mix3, w_in_bf, l, cos_p, sin_p, tm_p)
        cat, st_p = _mix_prompt_call(proj, side, sinks[l], cdec_p, bias_p, dmask_p, qdec_p, kdec_p)
        hp = _wo_call(cat, w_o_bf, l, hp, tm_p)
        hp = _ffn_call(hp, norm_ffn3, w_gu_bf, w_d_bf, l, tm_p, tf)
        kv_tail = side.reshape(BATCH, SEQ, SIDE_COLS)[:, SEQ - WINDOW:, kv_col:kv_col + 2 * kv_w]
        kp_new.append(kv_tail[..., :kv_w].reshape(BATCH, WINDOW, SWA_KV_HEADS, SWA_HEAD_DIM))
        vp_new.append(kv_tail[..., kv_w:].reshape(BATCH, WINDOW, SWA_KV_HEADS, SWA_HEAD_DIM))
        rp_new.append(st_p)

        proj, side = _proj_call(hs, norm_mix3, w_in_bf, l, cos_s, sin_s, tm_s)
        cat, st_s = _mix_sample_call(proj, side, cache_k, cache_v, state_ret, l, sinks[l], cdec_s,
                                     bias_s, valid_s, dmask_s, qdec_s, kdec_s)
        hs = _wo_call(cat, w_o_bf, l, hs, tm_s)
        hs = _ffn_call(hs, norm_ffn3, w_gu_bf, w_d_bf, l, tm_s, tf)
        kv_s = side[:, kv_col:kv_col + 2 * kv_w].reshape(DEC_BATCH, DEC_SEQ, 2 * kv_w)
        k_s = kv_s[..., :kv_w].reshape(DEC_BATCH, DEC_SEQ, SWA_KV_HEADS, SWA_HEAD_DIM)
        v_s = kv_s[..., kv_w:].reshape(DEC_BATCH, DEC_SEQ, SWA_KV_HEADS, SWA_HEAD_DIM)
        ks_new.append(jnp.concatenate([cache_k_win[l, :, DEC_SEQ:], k_s], axis=1))
        vs_new.append(jnp.concatenate([cache_v_win[l, :, DEC_SEQ:], v_s], axis=1))
        rs_new.append(st_s)

    norm_final2 = norm_final.reshape(1, D_MODEL)
    y_prompt = _norm_call(hp, norm_final2, tm_p).reshape(BATCH, SEQ, D_MODEL)
    y_sample = _norm_call(hs, norm_final2, tm_s).reshape(DEC_BATCH, DEC_SEQ, D_MODEL)
    return (y_prompt, y_sample,
            jnp.stack(kp_new), jnp.stack(vp_new), jnp.stack(rp_new),
            jnp.stack(ks_new), jnp.stack(vs_new), jnp.stack(rs_new))
```

```python
import functools
import math

import numpy as np
import jax
import jax.numpy as jnp
from jax import lax
from jax.experimental import pallas as pl
from jax.experimental.pallas import tpu as pltpu

D_MODEL = 2048
BATCH = 4
SEQ = 2048
DEPTH = 4
DEC_BATCH = 32
DEC_SEQ = 4
PAST_LEN = 16384

SWA_WIDTH = 1024
RET_WIDTH = 1024
SWA_HEADS = 8
SWA_KV_HEADS = 2
SWA_GROUP = SWA_HEADS // SWA_KV_HEADS
SWA_HEAD_DIM = 128
WINDOW = 128
BLOCK = WINDOW
RET_HEADS = 4
RET_KEY_DIM = 256
RET_VAL_DIM = 256
RET_CHUNK = 128
ROPE_BASE = 10000.0
N_BUCKETS = 32
MAX_DISTANCE = 128
EPS = 1e-6
D_FF = 5632
IN_COLS = 5632

F32 = jnp.float32
BF16 = jnp.bfloat16

VMEM_LIMIT_BYTES = 60 * 1024 * 1024

PROJ_TN = 512
PROJ_COLS = 3072
SIDE_COLS = 2560
KV_COL = 2048
PROJ_TILES = (
    ("proj", 0, "plain"), ("proj", 512, "plain"),
    ("side", KV_COL, "plain"),
    ("proj", 1024, "rotary"), ("proj", 1536, "rotary"),
    ("side", 0, "rotary_k"), ("side", 512, "rotary_k"),
    ("proj", 2048, "plain"), ("proj", 2560, "plain"),
    ("side", 1024, "plain"), ("side", 1536, "plain"),
)


def _rms_scale(x):
    return x * lax.rsqrt(jnp.mean(x * x, axis=-1, keepdims=True) + EPS)


def _silu(x):
    return x * jax.nn.sigmoid(x)


def _dot(a, b):
    return jnp.dot(a, b, preferred_element_type=F32)


def _dot_nt(a, b):
    return lax.dot_general(a, b, (((1,), (1,)), ((), ())), preferred_element_type=F32)


def _dot_tn(a, b):
    return lax.dot_general(a, b, (((0,), (0,)), ((), ())), preferred_element_type=F32)


def _resident(block_shape, index_map):
    return pl.BlockSpec(block_shape, index_map, pipeline_mode=pl.Buffered(1))


MASKED = -1e30


def _bias_kernel(rbt_ref, idx_ref, valid_ref, out_ref):
    h = pl.program_id(1)
    idx = idx_ref[...]
    acc = jnp.zeros(idx.shape, F32)
    for b in range(N_BUCKETS):
        acc = jnp.where(idx == b, rbt_ref[h, b], acc)
    out_ref[...] = jnp.where(valid_ref[...] > 0.5, acc, MASKED)


def _expand_bias(rel_bias_t, bucket_idx, valid):
    nv, rows, cols = valid.shape
    return pl.pallas_call(
        _bias_kernel,
        grid=(nv, SWA_HEADS),
        in_specs=[pl.BlockSpec(memory_space=pltpu.SMEM),
                  pl.BlockSpec((rows, cols), lambda v, h: (0, 0)),
                  pl.BlockSpec((None, rows, cols), lambda v, h: (v, 0, 0))],
        out_specs=pl.BlockSpec((None, None, rows, cols), lambda v, h: (v, h, 0, 0)),
        out_shape=jax.ShapeDtypeStruct((nv, SWA_HEADS, rows, cols), F32),
        name="bias_expand",
    )(rel_bias_t, bucket_idx, valid)


def _t5_bucket_np(delta):
    n = np.maximum(delta, 0)
    max_exact = N_BUCKETS // 2
    nf = np.maximum(n, 1).astype(np.float64)
    large = max_exact + (np.log(nf / max_exact) / math.log(MAX_DISTANCE / max_exact)
                         * (N_BUCKETS - max_exact)).astype(np.int32)
    large = np.minimum(large, N_BUCKETS - 1)
    return np.where(n < max_exact, n, large).astype(np.int32)


def _proj_epilogue(acc, mode, cos_ref, sin_ref, out_ref, col):
    if mode == "plain":
        out_ref[:, col:col + PROJ_TN] = acc.astype(out_ref.dtype)
        return
    half = RET_KEY_DIM // 2
    cos = cos_ref[...]
    sin = sin_ref[...]
    for c0 in range(0, PROJ_TN, RET_KEY_DIM):
        x1 = acc[:, c0:c0 + half]
        x2 = acc[:, c0 + half:c0 + RET_KEY_DIM]
        o1 = x1 * cos - x2 * sin
        o2 = x1 * sin + x2 * cos
        if mode == "rotary_k":
            o1 = o1 * (RET_KEY_DIM ** -0.5)
            o2 = o2 * (RET_KEY_DIM ** -0.5)
        out_ref[:, col + c0:col + c0 + half] = o1.astype(out_ref.dtype)
        out_ref[:, col + c0 + half:col + c0 + RET_KEY_DIM] = o2.astype(out_ref.dtype)


def _proj_kernel(h_ref, g_ref, w_ref, cos_ref, sin_ref, proj_ref, side_ref, xn_ref):
    xn_ref[...] = (_rms_scale(h_ref[...]) * g_ref[...]).astype(BF16)
    outs = {"proj": proj_ref, "side": side_ref}
    for t, (dst, col, mode) in enumerate(PROJ_TILES):
        acc = _dot(xn_ref[...], w_ref[:, t * PROJ_TN:(t + 1) * PROJ_TN])
        _proj_epilogue(acc, mode, cos_ref, sin_ref, outs[dst], col)


def _proj_call(h, g, layer, w_in_bf, cos, sin, tm):
    m = h.shape[0]
    cos_blocks = cos.shape[0] // tm
    return pl.pallas_call(
        _proj_kernel,
        grid=(m // tm,),
        in_specs=[
            pl.BlockSpec((tm, D_MODEL), lambda i: (i, 0)),
            _resident((None, 1, D_MODEL), lambda i: (layer, 0, 0)),
            _resident((D_MODEL, IN_COLS), lambda i: (0, 0)),
            pl.BlockSpec((tm, RET_KEY_DIM // 2), lambda i: (i % cos_blocks, 0)),
            pl.BlockSpec((tm, RET_KEY_DIM // 2), lambda i: (i % cos_blocks, 0)),
        ],
        out_specs=[
            pl.BlockSpec((tm, PROJ_COLS), lambda i: (i, 0)),
            pl.BlockSpec((tm, SIDE_COLS), lambda i: (i, 0)),
        ],
        out_shape=[jax.ShapeDtypeStruct((m, PROJ_COLS), BF16),
                   jax.ShapeDtypeStruct((m, SIDE_COLS), F32)],
        scratch_shapes=[pltpu.VMEM((tm, D_MODEL), BF16)],
        compiler_params=pltpu.CompilerParams(
            dimension_semantics=("arbitrary",), vmem_limit_bytes=VMEM_LIMIT_BYTES),
        name="proj",
    )(h, g, w_in_bf, cos, sin)


def _proj_cast_kernel(h_ref, g_ref, w_ref, cos_ref, sin_ref, proj_ref, side_ref, wbf_ref, xn_ref):
    j = pl.program_id(0)

    @pl.when(j == 0)
    def _():
        xn_ref[...] = (_rms_scale(h_ref[...]) * g_ref[...]).astype(BF16)

    w_bf = w_ref[...].astype(BF16)
    wbf_ref[...] = w_bf
    acc = _dot(xn_ref[...], w_bf)
    outs = {"proj": proj_ref, "side": side_ref}
    for t, (dst, col, mode) in enumerate(PROJ_TILES):
        @pl.when(j == t)
        def _():
            _proj_epilogue(acc, mode, cos_ref, sin_ref, outs[dst], col)


def _proj_cast_call(h, g, layer, w_in, cos, sin):
    m = h.shape[0]
    full = lambda shape: _resident(shape, lambda j: (0,) * len(shape))
    return pl.pallas_call(
        _proj_cast_kernel,
        grid=(len(PROJ_TILES),),
        in_specs=[
            full((m, D_MODEL)),
            _resident((None, 1, D_MODEL), lambda j: (layer, 0, 0)),
            pl.BlockSpec((None, D_MODEL, PROJ_TN), lambda j: (layer, 0, j)),
            full((m, RET_KEY_DIM // 2)),
            full((m, RET_KEY_DIM // 2)),
        ],
        out_specs=[
            pl.BlockSpec((m, PROJ_COLS), lambda j: (0, 0)),
            pl.BlockSpec((m, SIDE_COLS), lambda j: (0, 0)),
            pl.BlockSpec((D_MODEL, PROJ_TN), lambda j: (0, j)),
        ],
        out_shape=[jax.ShapeDtypeStruct((m, PROJ_COLS), BF16),
                   jax.ShapeDtypeStruct((m, SIDE_COLS), F32),
                   jax.ShapeDtypeStruct((D_MODEL, IN_COLS), BF16)],
        scratch_shapes=[pltpu.VMEM((m, D_MODEL), BF16)],
        compiler_params=pltpu.CompilerParams(
            dimension_semantics=("arbitrary",), vmem_limit_bytes=VMEM_LIMIT_BYTES),
        name="proj_cast",
    )(h, g, w_in, cos, sin)


def _softmax_sink(s, sink):
    m = jnp.maximum(jnp.max(s, axis=-1, keepdims=True), sink)
    p = jnp.exp(s - m)
    return p, 1.0 / (jnp.sum(p, axis=-1, keepdims=True) + jnp.exp(sink - m))


def _gate_out(o, g):
    return (_rms_scale(o) * _silu(g)).astype(BF16)


WO_BATCHES = 2


def _mix_prompt_kernel(sink_ref, cdec_ref, qs_ref, qr_ref, vr_ref, kr_ref, g_ref, kvc_ref,
                       bias_ref, dmask_ref, qdec_ref, kdec_ref, wo_ref, h_ref, out_ref, st_ref, cat_ref, kvp_ref):
    n = pl.program_id(0)

    @pl.when(n == 0)
    def _():
        st_ref[...] = jnp.zeros(st_ref.shape, F32)
        kvp_ref[...] = jnp.zeros(kvp_ref.shape, BF16)

    scale = SWA_HEAD_DIM ** -0.5
    v_off = SWA_KV_HEADS * SWA_HEAD_DIM

    def wo_rows(b0):
        bs = slice(b0, b0 + WO_BATCHES)
        cat = cat_ref[bs].reshape(WO_BATCHES * BLOCK, D_MODEL)
        out_ref[bs] = h_ref[bs] + _dot(cat, wo_ref[...]).reshape(WO_BATCHES, BLOCK, D_MODEL)

    def attention(b, kh):
        c0 = kh * SWA_HEAD_DIM
        k_cur = kvc_ref[b, :, c0:c0 + SWA_HEAD_DIM].astype(BF16)
        v_cur = kvc_ref[b, :, v_off + c0:v_off + c0 + SWA_HEAD_DIM].astype(BF16)
        k_cat = jnp.concatenate([kvp_ref[b, :, c0:c0 + SWA_HEAD_DIM], k_cur], axis=0)
        v_cat = jnp.concatenate([kvp_ref[b, :, v_off + c0:v_off + c0 + SWA_HEAD_DIM], v_cur], axis=0)
        kvp_ref[b, :, c0:c0 + SWA_HEAD_DIM] = k_cur
        kvp_ref[b, :, v_off + c0:v_off + c0 + SWA_HEAD_DIM] = v_cur
        for gq in range(SWA_GROUP):
            h = kh * SWA_GROUP + gq
            q = qs_ref[b, :, h * SWA_HEAD_DIM:(h + 1) * SWA_HEAD_DIM]
            s = _dot_nt(q, k_cat) * scale + bias_ref[h]
            p, inv = _softmax_sink(s, sink_ref[h])
            o = _dot(p.astype(BF16), v_cat) * inv
            cat_ref[b, :, h * SWA_HEAD_DIM:(h + 1) * SWA_HEAD_DIM] = o.astype(BF16)

    def retention(b, h):
        cs = slice(h * RET_KEY_DIM, (h + 1) * RET_KEY_DIM)
        q = qr_ref[b, :, cs]
        k32 = kr_ref[b, :, cs]
        v = vr_ref[b, :, cs]
        s = _dot_nt(q, k32.astype(BF16)) * dmask_ref[h]
        o = _dot(s.astype(BF16), v)
        st = st_ref[b, h]
        o = o + _dot(q, st.astype(BF16)) * qdec_ref[:, cs]
        kd = (k32 * kdec_ref[:, cs]).astype(BF16)
        st_ref[b, h] = cdec_ref[h] * st + _dot_tn(kd, v)
        cat_ref[b, :, SWA_WIDTH + h * RET_VAL_DIM:SWA_WIDTH + (h + 1) * RET_VAL_DIM] = (
            _gate_out(o, g_ref[b, :, cs]))

    for b in range(BATCH):
        for kh in range(SWA_KV_HEADS):
            attention(b, kh)
        for h in range(RET_HEADS):
            retention(b, h)
        if (b + 1) % WO_BATCHES == 0:
            wo_rows(b + 1 - WO_BATCHES)


def _mix_prompt_call(proj, side, h, w_o_bf, sinks_l, cdec, bias, dmask, qdec, kdec):
    nblk = SEQ // BLOCK
    proj3 = proj.reshape(BATCH, SEQ, PROJ_COLS)
    side3 = side.reshape(BATCH, SEQ, SIDE_COLS)
    h3 = h.reshape(BATCH, SEQ, D_MODEL)
    smem = pl.BlockSpec(memory_space=pltpu.SMEM)
    full = lambda shape: _resident(shape, lambda n: (0,) * len(shape))
    cur = lambda n: n
    prv = lambda n: n
    out, st = pl.pallas_call(
        _mix_prompt_kernel,
        grid=(nblk,),
        in_specs=[
            smem, smem,
            pl.BlockSpec((BATCH, BLOCK, 1024), lambda n: (0, cur(n), 0)),
            pl.BlockSpec((BATCH, BLOCK, 1024), lambda n: (0, cur(n), 1)),
            pl.BlockSpec((BATCH, BLOCK, 1024), lambda n: (0, cur(n), 2)),
            pl.BlockSpec((BATCH, BLOCK, 1024), lambda n: (0, cur(n), 0)),
            pl.BlockSpec((BATCH, BLOCK, 1024), lambda n: (0, cur(n), 1)),
            pl.BlockSpec((BATCH, BLOCK, 512), lambda n: (0, cur(n), 4)),
            pl.BlockSpec((None, SWA_HEADS, BLOCK, 2 * BLOCK), lambda n: (jnp.minimum(n, 1), 0, 0, 0)),
            full((RET_HEADS, RET_CHUNK, RET_CHUNK)),
            full((RET_CHUNK, RET_WIDTH)),
            full((RET_CHUNK, RET_WIDTH)),
            full((D_MODEL, D_MODEL)),
            pl.BlockSpec((BATCH, BLOCK, D_MODEL), lambda n: (0, prv(n), 0)),
        ],
        out_specs=[
            pl.BlockSpec((BATCH, BLOCK, D_MODEL), lambda n: (0, prv(n), 0)),
            _resident((BATCH, RET_HEADS, RET_KEY_DIM, RET_VAL_DIM), lambda n: (0, 0, 0, 0)),
        ],
        out_shape=[jax.ShapeDtypeStruct((BATCH, SEQ, D_MODEL), F32),
                   jax.ShapeDtypeStruct((BATCH, RET_HEADS, RET_KEY_DIM, RET_VAL_DIM), F32)],
        scratch_shapes=[pltpu.VMEM((BATCH, BLOCK, D_MODEL), BF16),
                        pltpu.VMEM((BATCH, BLOCK, 2 * SWA_KV_HEADS * SWA_HEAD_DIM), BF16)],
        compiler_params=pltpu.CompilerParams(
            dimension_semantics=("arbitrary",), vmem_limit_bytes=VMEM_LIMIT_BYTES),
        name="mix_prompt",
    )(sinks_l, cdec, proj3, proj3, proj3, side3, side3, side3, bias, dmask, qdec, kdec, w_o_bf, h3)
    return out.reshape(BATCH * SEQ, D_MODEL), st


SAMPLE_BB = 4
SAMPLE_ROWS = SAMPLE_BB * DEC_SEQ
N_SAMPLE = DEC_BATCH * DEC_SEQ


def _mix_sample_kernel(sink_ref, cdec_ref, prow_ref, srow_ref, pall_ref, sall_ref, ck_ref, cv_ref, st_ref,
                       bias_ref, dmask_ref, qdec_ref, kdec_ref, cat_ref, stout_ref):
    r = SAMPLE_ROWS
    row_b = lax.broadcasted_iota(jnp.int32, (r, 1), 0) // DEC_SEQ
    row_b4 = lax.broadcasted_iota(jnp.int32, (SWA_GROUP * r, 1), 0) % r // DEC_SEQ
    scale = SWA_HEAD_DIM ** -0.5
    v_off = SWA_KV_HEADS * SWA_HEAD_DIM

    for kh in range(SWA_KV_HEADS):
        c0 = kh * SWA_HEAD_DIM
        q4 = jnp.concatenate(
            [prow_ref[:, (kh * SWA_GROUP + gq) * SWA_HEAD_DIM:(kh * SWA_GROUP + gq + 1) * SWA_HEAD_DIM]
             for gq in range(SWA_GROUP)], axis=0)
        k_new = sall_ref[:, KV_COL + c0:KV_COL + c0 + SWA_HEAD_DIM].astype(BF16)
        v_new = sall_ref[:, KV_COL + v_off + c0:KV_COL + v_off + c0 + SWA_HEAD_DIM].astype(BF16)
        s_cache = jnp.zeros((SWA_GROUP * r, WINDOW), F32)
        for bi in range(SAMPLE_BB):
            k_c = ck_ref[bi, :, c0:c0 + SWA_HEAD_DIM].astype(BF16)
            s_cache = jnp.where(row_b4 == bi, _dot_nt(q4, k_c), s_cache)
        s_new = _dot_nt(q4, k_new)
        s4 = jnp.concatenate([s_cache, s_new], axis=1) * scale
        p_parts, inv_parts = [], []
        for gq in range(SWA_GROUP):
            h = kh * SWA_GROUP + gq
            s = s4[gq * r:(gq + 1) * r] + bias_ref[h]
            p, inv = _softmax_sink(s, sink_ref[h])
            p_parts.append(p)
            inv_parts.append(inv)
        p4 = jnp.concatenate(p_parts, axis=0)
        inv4 = jnp.concatenate(inv_parts, axis=0)
        p_cache = p4[:, :WINDOW]
        o4 = _dot(p4[:, WINDOW:].astype(BF16), v_new)
        for bi in range(SAMPLE_BB):
            v_c = cv_ref[bi, :, c0:c0 + SWA_HEAD_DIM].astype(BF16)
            o4 = o4 + _dot(jnp.where(row_b4 == bi, p_cache, 0.0).astype(BF16), v_c)
        o4 = o4 * inv4
        for gq in range(SWA_GROUP):
            h = kh * SWA_GROUP + gq
            cat_ref[:, h * SWA_HEAD_DIM:(h + 1) * SWA_HEAD_DIM] = o4[gq * r:(gq + 1) * r].astype(BF16)

    for h in range(RET_HEADS):
        cs = slice(h * RET_KEY_DIM, (h + 1) * RET_KEY_DIM)
        q = prow_ref[:, 1024 + h * RET_KEY_DIM:1024 + (h + 1) * RET_KEY_DIM]
        k_all = sall_ref[:, cs].astype(BF16)
        v_all = pall_ref[:, 2048 + h * RET_VAL_DIM:2048 + (h + 1) * RET_VAL_DIM]
        s = _dot_nt(q, k_all) * dmask_ref[h]
        o = _dot(s.astype(BF16), v_all)
        k32 = srow_ref[:, cs]
        v = prow_ref[:, 2048 + h * RET_VAL_DIM:2048 + (h + 1) * RET_VAL_DIM]
        kd = k32 * kdec_ref[:, cs]
        cross = jnp.zeros((r, RET_VAL_DIM), F32)
        for bi in range(SAMPLE_BB):
            st = st_ref[bi, h]
            cross = jnp.where(row_b == bi, _dot(q, st.astype(BF16)), cross)
            kd_b = jnp.where(row_b == bi, kd, 0.0).astype(BF16)
            stout_ref[bi, h] = cdec_ref[h] * st + _dot_tn(kd_b, v)
        o = o + cross * qdec_ref[:, cs]
        g = srow_ref[:, 1024 + h * RET_VAL_DIM:1024 + (h + 1) * RET_VAL_DIM]
        cat_ref[:, SWA_WIDTH + h * RET_VAL_DIM:SWA_WIDTH + (h + 1) * RET_VAL_DIM] = _gate_out(o, g)


def _mix_sample_call(proj, side, cache_k, cache_v, state_ret, layer, sinks_l, cdec, bias, dmask, qdec, kdec):
    r = SAMPLE_ROWS
    smem = pl.BlockSpec(memory_space=pltpu.SMEM)
    full = lambda shape: pl.BlockSpec(shape, lambda c: (0,) * len(shape))
    in_specs = [
        smem, smem,
        pl.BlockSpec((r, PROJ_COLS), lambda c: (c, 0)),
        pl.BlockSpec((r, SIDE_COLS), lambda c: (c, 0)),
        full((N_SAMPLE, PROJ_COLS)),
        full((N_SAMPLE, SIDE_COLS)),
        pl.BlockSpec((None, SAMPLE_BB, WINDOW, 256), lambda c: (layer, c, 0, 0)),
        pl.BlockSpec((None, SAMPLE_BB, WINDOW, 256), lambda c: (layer, c, 0, 0)),
        pl.BlockSpec((None, SAMPLE_BB, RET_HEADS, RET_KEY_DIM, RET_VAL_DIM), lambda c: (layer, c, 0, 0, 0)),
        pl.BlockSpec((SWA_HEADS, r, 2 * WINDOW), lambda c: (0, c, 0)),
        pl.BlockSpec((RET_HEADS, r, N_SAMPLE), lambda c: (0, c, 0)),
        pl.BlockSpec((r, RET_WIDTH), lambda c: (c, 0)),
        pl.BlockSpec((r, RET_WIDTH), lambda c: (c, 0)),
    ]
    return pl.pallas_call(
        _mix_sample_kernel,
        grid=(DEC_BATCH // SAMPLE_BB,),
        in_specs=in_specs,
        out_specs=[
            pl.BlockSpec((r, D_MODEL), lambda c: (c, 0)),
            pl.BlockSpec((SAMPLE_BB, RET_HEADS, RET_KEY_DIM, RET_VAL_DIM), lambda c: (c, 0, 0, 0)),
        ],
        out_shape=[jax.ShapeDtypeStruct((N_SAMPLE, D_MODEL), BF16),
                   jax.ShapeDtypeStruct((DEC_BATCH, RET_HEADS, RET_KEY_DIM, RET_VAL_DIM), F32)],
        compiler_params=pltpu.CompilerParams(
            dimension_semantics=("arbitrary",), vmem_limit_bytes=VMEM_LIMIT_BYTES),
        name="mix_sample",
    )(sinks_l, cdec, proj, side, proj, side, cache_k, cache_v, state_ret, bias, dmask, qdec, kdec)


WO_TN = 512


def _wo_cast_kernel(cat_ref, w_ref, h_ref, out_ref, wbf_ref):
    w_bf = w_ref[...].astype(BF16)
    wbf_ref[...] = w_bf
    out_ref[...] = h_ref[...] + _dot(cat_ref[...], w_bf)


def _wo_cast_call(cat, w_o, layer, h):
    m = h.shape[0]
    return pl.pallas_call(
        _wo_cast_kernel,
        grid=(D_MODEL // WO_TN,),
        in_specs=[
            _resident((m, D_MODEL), lambda j: (0, 0)),
            pl.BlockSpec((None, D_MODEL, WO_TN), lambda j: (layer, 0, j)),
            pl.BlockSpec((m, WO_TN), lambda j: (0, j)),
        ],
        out_specs=[pl.BlockSpec((m, WO_TN), lambda j: (0, j)),
                   pl.BlockSpec((D_MODEL, WO_TN), lambda j: (0, j))],
        out_shape=[jax.ShapeDtypeStruct((m, D_MODEL), F32),
                   jax.ShapeDtypeStruct((D_MODEL, D_MODEL), BF16)],
        compiler_params=pltpu.CompilerParams(
            dimension_semantics=("arbitrary",), vmem_limit_bytes=VMEM_LIMIT_BYTES),
        name="wo_cast",
    )(cat, w_o, h)


def _ffn_step(f, n_f, h_ref, g_ref, weights, gfin_ref, out_ref, xn_ref, final_norm):
    @pl.when(f == 0)
    def _():
        x = h_ref[...]
        xn_ref[...] = (_rms_scale(x) * g_ref[...]).astype(BF16)
        out_ref[...] = x

    w_gate, w_up, w_down = weights()
    xn = xn_ref[...]
    a = _dot(xn, w_gate)
    b = _dot(xn, w_up)
    act = (_silu(a) * b).astype(BF16)
    out_ref[...] += _dot(act, w_down)

    if final_norm:
        @pl.when(f == n_f - 1)
        def _():
            out_ref[...] = _rms_scale(out_ref[...]) * gfin_ref[...]


def _ffn_kernel(h_ref, g_ref, wg_ref, wu_ref, wd_ref, gfin_ref, out_ref, xn_ref, *, final_norm):
    _ffn_step(pl.program_id(1), pl.num_programs(1), h_ref, g_ref,
              lambda: (wg_ref[...], wu_ref[...], wd_ref[...]), gfin_ref, out_ref, xn_ref, final_norm)


def _ffn_call(h, g, w_g_bf, w_u_bf, w_d_bf, g_final, layer, tm, tf):
    m = h.shape[0]
    return pl.pallas_call(
        functools.partial(_ffn_kernel, final_norm=(layer == DEPTH - 1)),
        grid=(m // tm, D_FF // tf),
        in_specs=[
            pl.BlockSpec((tm, D_MODEL), lambda i, f: (i, 0)),
            pl.BlockSpec((None, 1, D_MODEL), lambda i, f: (layer, 0, 0)),
            pl.BlockSpec((D_MODEL, tf), lambda i, f: (0, f)),
            pl.BlockSpec((D_MODEL, tf), lambda i, f: (0, f)),
            pl.BlockSpec((tf, D_MODEL), lambda i, f: (f, 0)),
            pl.BlockSpec((1, D_MODEL), lambda i, f: (0, 0)),
        ],
        out_specs=pl.BlockSpec((tm, D_MODEL), lambda i, f: (i, 0)),
        out_shape=jax.ShapeDtypeStruct((m, D_MODEL), F32),
        scratch_shapes=[pltpu.VMEM((tm, D_MODEL), BF16)],
        compiler_params=pltpu.CompilerParams(
            dimension_semantics=("arbitrary", "arbitrary"), vmem_limit_bytes=VMEM_LIMIT_BYTES),
        name="ffn",
    )(h, g, w_g_bf, w_u_bf, w_d_bf, g_final)


def _ffn_cast_kernel(h_ref, g_ref, wg_ref, wu_ref, wd_ref, gfin_ref, out_ref, wgbf_ref, wubf_ref, wdbf_ref,
                     xn_ref, *, final_norm):
    def weights():
        wgbf_ref[...] = wg_ref[...].astype(BF16)
        wubf_ref[...] = wu_ref[...].astype(BF16)
        wdbf_ref[...] = wd_ref[...].astype(BF16)
        return wgbf_ref[...], wubf_ref[...], wdbf_ref[...]

    _ffn_step(pl.program_id(0), pl.num_programs(0), h_ref, g_ref, weights, gfin_ref, out_ref, xn_ref, final_norm)


def _ffn_cast_call(h, g, w_gate_up, w_down, g_final, layer, tf):
    m = h.shape[0]
    nf = D_FF // tf
    return pl.pallas_call(
        functools.partial(_ffn_cast_kernel, final_norm=(layer == DEPTH - 1)),
        grid=(nf,),
        in_specs=[
            _resident((m, D_MODEL), lambda f: (0, 0)),
            _resident((None, 1, D_MODEL), lambda f: (layer, 0, 0)),
            pl.BlockSpec((None, D_MODEL, tf), lambda f: (layer, 0, f)),
            pl.BlockSpec((None, D_MODEL, tf), lambda f: (layer, 0, f + nf)),
            pl.BlockSpec((None, tf, D_MODEL), lambda f: (layer, f, 0)),
            _resident((1, D_MODEL), lambda f: (0, 0)),
        ],
        out_specs=[
            pl.BlockSpec((m, D_MODEL), lambda f: (0, 0)),
            pl.BlockSpec((D_MODEL, tf), lambda f: (0, f)),
            pl.BlockSpec((D_MODEL, tf), lambda f: (0, f)),
            pl.BlockSpec((tf, D_MODEL), lambda f: (f, 0)),
        ],
        out_shape=[jax.ShapeDtypeStruct((m, D_MODEL), F32),
                   jax.ShapeDtypeStruct((D_MODEL, D_FF), BF16),
                   jax.ShapeDtypeStruct((D_MODEL, D_FF), BF16),
                   jax.ShapeDtypeStruct((D_FF, D_MODEL), BF16)],
        scratch_shapes=[pltpu.VMEM((m, D_MODEL), BF16)],
        compiler_params=pltpu.CompilerParams(
            dimension_semantics=("arbitrary",), vmem_limit_bytes=VMEM_LIMIT_BYTES),
        name="ffn_cast",
    )(h, g, w_gate_up, w_gate_up, w_down, g_final)


def _rope_tables(pos):
    half = RET_KEY_DIM // 2
    inv = 1.0 / (ROPE_BASE ** jnp.linspace(0.0, 1.0, half, dtype=F32))
    ang = pos.astype(F32)[:, None] * inv[None, :]
    return jnp.cos(ang), jnp.sin(ang)


def _decay_tables(c):
    lg = jnp.log(1.0 - jnp.exp2(-5.0 - jnp.arange(RET_HEADS, dtype=F32)))
    idx = jnp.arange(c, dtype=F32)
    diff = idx[:, None] - idx[None, :]
    dmask = jnp.where(diff[None] >= 0, jnp.exp(jnp.maximum(diff, 0.0)[None] * lg[:, None, None]), 0.0)
    q_decay = jnp.exp((idx[:, None] + 1.0) * lg[None, :])
    k_decay = jnp.exp((c - 1.0 - idx)[:, None] * lg[None, :])
    c_decay = jnp.exp(c * lg)
    return dmask, q_decay, k_decay, c_decay


def _per_head_cols(t):
    return jnp.repeat(t, RET_KEY_DIM, axis=1)


def kernel(x_prompt, x_sample, cache_k_win, cache_v_win, state_ret, rel_bias, w_in, sinks, w_o,
           norm_mix, norm_ffn, w_gate_up, w_down, norm_final):
    norm_mix3 = norm_mix.reshape(DEPTH, 1, D_MODEL)
    norm_ffn3 = norm_ffn.reshape(DEPTH, 1, D_MODEL)
    norm_final2 = norm_final.reshape(1, D_MODEL)
    cache_k = cache_k_win.reshape(DEPTH, DEC_BATCH, WINDOW, SWA_KV_HEADS * SWA_HEAD_DIM)
    cache_v = cache_v_win.reshape(DEPTH, DEC_BATCH, WINDOW, SWA_KV_HEADS * SWA_HEAD_DIM)

    cos_p, sin_p = _rope_tables(jnp.arange(SEQ, dtype=jnp.int32))
    cos_4, sin_4 = _rope_tables(PAST_LEN + jnp.arange(DEC_SEQ, dtype=jnp.int32))
    cos_s = jnp.tile(cos_4, (DEC_BATCH, 1))
    sin_s = jnp.tile(sin_4, (DEC_BATCH, 1))

    dmask_p, qd_p, kd_p, cdec_p = _decay_tables(RET_CHUNK)
    qdec_p = _per_head_cols(qd_p)
    kdec_p = _per_head_cols(kd_p)
    dmask_4, qd_4, kd_4, cdec_s = _decay_tables(DEC_SEQ)
    eye_b = jnp.eye(DEC_BATCH, dtype=F32)
    dmask_s = jax.vmap(lambda d: jnp.kron(eye_b, d))(dmask_4)
    qdec_s = jnp.tile(_per_head_cols(qd_4), (DEC_BATCH, 1))
    kdec_s = jnp.tile(_per_head_cols(kd_4), (DEC_BATCH, 1))

    qi = np.arange(BLOCK)[:, None]
    kj = np.arange(2 * BLOCK)[None, :]
    delta_p = qi + BLOCK - kj
    in_window = (delta_p >= 0) & (delta_p < WINDOW)
    valid_p = np.stack([in_window & (kj >= BLOCK), in_window]).astype(np.float32)
    rows = np.arange(N_SAMPLE)
    rb, rt = rows // DEC_SEQ, rows % DEC_SEQ
    delta_cache = (WINDOW + rt)[:, None] - np.arange(WINDOW)[None, :]
    delta_new = rt[:, None] - rt[None, :]
    same_b = rb[:, None] == rb[None, :]
    delta_s = np.concatenate([delta_cache, delta_new], axis=1)
    valid_s = np.concatenate([delta_cache < WINDOW, same_b & (delta_new >= 0)], axis=1).astype(np.float32)[None]
    rel_bias_t = rel_bias.T
    bias_p = _expand_bias(rel_bias_t, jnp.asarray(_t5_bucket_np(delta_p)), jnp.asarray(valid_p))
    bias_s = _expand_bias(rel_bias_t, jnp.asarray(_t5_bucket_np(delta_s)), jnp.asarray(valid_s))[0]

    hp = x_prompt.reshape(BATCH * SEQ, D_MODEL)
    hs = x_sample.reshape(N_SAMPLE, D_MODEL)
    tm_proj, tm_ffn, tf = 512, 1024, 512
    kp_new, vp_new, rp_new, ks_new, vs_new, rs_new = [], [], [], [], [], []
    kv_w = SWA_KV_HEADS * SWA_HEAD_DIM
    for l in range(DEPTH):
        proj, side, w_in_bf = _proj_cast_call(hs, norm_mix3, l, w_in, cos_s, sin_s)
        cat, st_s = _mix_sample_call(proj, side, cache_k, cache_v, state_ret, l, sinks[l], cdec_s,
                                     bias_s, dmask_s, qdec_s, kdec_s)
        rs_new.append(st_s)
        hs, w_o_bf = _wo_cast_call(cat, w_o, l, hs)
        hs, w_g_bf, w_u_bf, w_d_bf = _ffn_cast_call(hs, norm_ffn3, w_gate_up, w_down, norm_final2, l, tf)
        kv_s = side[:, KV_COL:KV_COL + 2 * kv_w].reshape(DEC_BATCH, DEC_SEQ, 2 * kv_w)
        k_s = kv_s[..., :kv_w].reshape(DEC_BATCH, DEC_SEQ, SWA_KV_HEADS, SWA_HEAD_DIM)
        v_s = kv_s[..., kv_w:].reshape(DEC_BATCH, DEC_SEQ, SWA_KV_HEADS, SWA_HEAD_DIM)
        ks_new.append(jnp.concatenate([cache_k_win[l, :, DEC_SEQ:], k_s], axis=1))
        vs_new.append(jnp.concatenate([cache_v_win[l, :, DEC_SEQ:], v_s], axis=1))

        proj, side = _proj_call(hp, norm_mix3, l, w_in_bf, cos_p, sin_p, tm_proj)
        hp, st_p = _mix_prompt_call(proj, side, hp, w_o_bf, sinks[l], cdec_p, bias_p, dmask_p, qdec_p, kdec_p)
        hp = _ffn_call(hp, norm_ffn3, w_g_bf, w_u_bf, w_d_bf, norm_final2, l, tm_ffn, tf)
        kv_tail = side.reshape(BATCH, SEQ, SIDE_COLS)[:, SEQ - WINDOW:, KV_COL:KV_COL + 2 * kv_w]
        kp_new.append(kv_tail[..., :kv_w].reshape(BATCH, WINDOW, SWA_KV_HEADS, SWA_HEAD_DIM))
        vp_new.append(kv_tail[..., kv_w:].reshape(BATCH, WINDOW, SWA_KV_HEADS, SWA_HEAD_DIM))
        rp_new.append(st_p)

    y_prompt = hp.reshape(BATCH, SEQ, D_MODEL)
    y_sample = hs.reshape(DEC_BATCH, DEC_SEQ, D_MODEL)
    return (y_prompt, y_sample,
            jnp.stack(kp_new), jnp.stack(vp_new), jnp.stack(rp_new),
            jnp.stack(ks_new), jnp.stack(vs_new), jnp.stack(rs_new))
```

```python
import functools
import math

import numpy as np
import jax
import jax.numpy as jnp
from jax import lax
from jax.experimental import pallas as pl
from jax.experimental.pallas import tpu as pltpu

D_MODEL = 2048
BATCH = 4
SEQ = 2048
DEPTH = 4
DEC_BATCH = 32
DEC_SEQ = 4
PAST_LEN = 16384

SWA_WIDTH = 1024
RET_WIDTH = 1024
SWA_HEADS = 8
SWA_KV_HEADS = 2
SWA_GROUP = SWA_HEADS // SWA_KV_HEADS
SWA_HEAD_DIM = 128
WINDOW = 128
BLOCK = WINDOW
RET_HEADS = 4
RET_KEY_DIM = 256
RET_VAL_DIM = 256
RET_CHUNK = 128
ROPE_BASE = 10000.0
N_BUCKETS = 32
MAX_DISTANCE = 128
EPS = 1e-6
D_FF = 5632
IN_COLS = 5632

F32 = jnp.float32
BF16 = jnp.bfloat16

VMEM_LIMIT_BYTES = 60 * 1024 * 1024

PROJ_TN = 512
PROJ_COLS = 3072
SIDE_COLS = 2560
KV_COL = 2048
PROJ_TILES = (
    ("proj", 0, "plain"), ("proj", 512, "plain"),
    ("side", KV_COL, "plain"),
    ("proj", 1024, "rotary"), ("proj", 1536, "rotary"),
    ("side", 0, "rotary_k"), ("side", 512, "rotary_k"),
    ("proj", 2048, "plain"), ("proj", 2560, "plain"),
    ("side", 1024, "plain"), ("side", 1536, "plain"),
)


def _rms_scale(x):
    return x * lax.rsqrt(jnp.mean(x * x, axis=-1, keepdims=True) + EPS)


def _silu(x):
    return x * jax.nn.sigmoid(x)


def _dot(a, b):
    return jnp.dot(a, b, preferred_element_type=F32)


def _dot_nt(a, b):
    return lax.dot_general(a, b, (((1,), (1,)), ((), ())), preferred_element_type=F32)


def _dot_tn(a, b):
    return lax.dot_general(a, b, (((0,), (0,)), ((), ())), preferred_element_type=F32)


def _resident(block_shape, index_map):
    return pl.BlockSpec(block_shape, index_map, pipeline_mode=pl.Buffered(1))


MASKED = -1e30


def _bias_kernel(rbt_ref, idx_ref, valid_ref, out_ref):
    h = pl.program_id(1)
    idx = idx_ref[...]
    acc = jnp.zeros(idx.shape, F32)
    for b in range(N_BUCKETS):
        acc = jnp.where(idx == b, rbt_ref[h, b], acc)
    out_ref[...] = jnp.where(valid_ref[...] > 0.5, acc, MASKED)


def _expand_bias(rel_bias_t, bucket_idx, valid):
    nv, rows, cols = valid.shape
    return pl.pallas_call(
        _bias_kernel,
        grid=(nv, SWA_HEADS),
        in_specs=[pl.BlockSpec(memory_space=pltpu.SMEM),
                  pl.BlockSpec((rows, cols), lambda v, h: (0, 0)),
                  pl.BlockSpec((None, rows, cols), lambda v, h: (v, 0, 0))],
        out_specs=pl.BlockSpec((None, None, rows, cols), lambda v, h: (v, h, 0, 0)),
        out_shape=jax.ShapeDtypeStruct((nv, SWA_HEADS, rows, cols), F32),
        name="bias_expand",
    )(rel_bias_t, bucket_idx, valid)


def _t5_bucket_np(delta):
    n = np.maximum(delta, 0)
    max_exact = N_BUCKETS // 2
    nf = np.maximum(n, 1).astype(np.float64)
    large = max_exact + (np.log(nf / max_exact) / math.log(MAX_DISTANCE / max_exact)
                         * (N_BUCKETS - max_exact)).astype(np.int32)
    large = np.minimum(large, N_BUCKETS - 1)
    return np.where(n < max_exact, n, large).astype(np.int32)


def _proj_epilogue(acc, mode, cos_ref, sin_ref, out_ref, col):
    if mode == "plain":
        out_ref[:, col:col + PROJ_TN] = acc.astype(out_ref.dtype)
        return
    half = RET_KEY_DIM // 2
    cos = cos_ref[...]
    sin = sin_ref[...]
    for c0 in range(0, PROJ_TN, RET_KEY_DIM):
        x1 = acc[:, c0:c0 + half]
        x2 = acc[:, c0 + half:c0 + RET_KEY_DIM]
        o1 = x1 * cos - x2 * sin
        o2 = x1 * sin + x2 * cos
        if mode == "rotary_k":
            o1 = o1 * (RET_KEY_DIM ** -0.5)
            o2 = o2 * (RET_KEY_DIM ** -0.5)
        out_ref[:, col + c0:col + c0 + half] = o1.astype(out_ref.dtype)
        out_ref[:, col + c0 + half:col + c0 + RET_KEY_DIM] = o2.astype(out_ref.dtype)


def _proj_kernel(h_ref, g_ref, w_ref, cos_ref, sin_ref, proj_ref, side_ref, xn_ref):
    xn_ref[...] = (_rms_scale(h_ref[...]) * g_ref[...]).astype(BF16)
    outs = {"proj": proj_ref, "side": side_ref}
    for t, (dst, col, mode) in enumerate(PROJ_TILES):
        acc = _dot(xn_ref[...], w_ref[:, t * PROJ_TN:(t + 1) * PROJ_TN])
        _proj_epilogue(acc, mode, cos_ref, sin_ref, outs[dst], col)


def _proj_call(h, g, layer, w_in_bf, cos, sin, tm):
    m = h.shape[0]
    cos_blocks = cos.shape[0] // tm
    return pl.pallas_call(
        _proj_kernel,
        grid=(m // tm,),
        in_specs=[
            pl.BlockSpec((tm, D_MODEL), lambda i: (i, 0)),
            _resident((None, 1, D_MODEL), lambda i: (layer, 0, 0)),
            _resident((D_MODEL, IN_COLS), lambda i: (0, 0)),
            pl.BlockSpec((tm, RET_KEY_DIM // 2), lambda i: (i % cos_blocks, 0)),
            pl.BlockSpec((tm, RET_KEY_DIM // 2), lambda i: (i % cos_blocks, 0)),
        ],
        out_specs=[
            pl.BlockSpec((tm, PROJ_COLS), lambda i: (i, 0)),
            pl.BlockSpec((tm, SIDE_COLS), lambda i: (i, 0)),
        ],
        out_shape=[jax.ShapeDtypeStruct((m, PROJ_COLS), BF16),
                   jax.ShapeDtypeStruct((m, SIDE_COLS), F32)],
        scratch_shapes=[pltpu.VMEM((tm, D_MODEL), BF16)],
        compiler_params=pltpu.CompilerParams(
            dimension_semantics=("arbitrary",), vmem_limit_bytes=VMEM_LIMIT_BYTES),
        name="proj",
    )(h, g, w_in_bf, cos, sin)


def _proj_cast_kernel(h_ref, g_ref, w_ref, cos_ref, sin_ref, proj_ref, side_ref, wbf_ref, xn_ref):
    j = pl.program_id(0)

    @pl.when(j == 0)
    def _():
        xn_ref[...] = (_rms_scale(h_ref[...]) * g_ref[...]).astype(BF16)

    w_bf = w_ref[...].astype(BF16)
    wbf_ref[...] = w_bf
    acc = _dot(xn_ref[...], w_bf)
    outs = {"proj": proj_ref, "side": side_ref}
    for t, (dst, col, mode) in enumerate(PROJ_TILES):
        @pl.when(j == t)
        def _():
            _proj_epilogue(acc, mode, cos_ref, sin_ref, outs[dst], col)


def _proj_cast_call(h, g, layer, w_in, cos, sin):
    m = h.shape[0]
    full = lambda shape: _resident(shape, lambda j: (0,) * len(shape))
    return pl.pallas_call(
        _proj_cast_kernel,
        grid=(len(PROJ_TILES),),
        in_specs=[
            full((m, D_MODEL)),
            _resident((None, 1, D_MODEL), lambda j: (layer, 0, 0)),
            pl.BlockSpec((None, D_MODEL, PROJ_TN), lambda j: (layer, 0, j)),
            full((m, RET_KEY_DIM // 2)),
            full((m, RET_KEY_DIM // 2)),
        ],
        out_specs=[
            pl.BlockSpec((m, PROJ_COLS), lambda j: (0, 0)),
            pl.BlockSpec((m, SIDE_COLS), lambda j: (0, 0)),
            pl.BlockSpec((D_MODEL, PROJ_TN), lambda j: (0, j)),
        ],
        out_shape=[jax.ShapeDtypeStruct((m, PROJ_COLS), BF16),
                   jax.ShapeDtypeStruct((m, SIDE_COLS), F32),
                   jax.ShapeDtypeStruct((D_MODEL, IN_COLS), BF16)],
        scratch_shapes=[pltpu.VMEM((m, D_MODEL), BF16)],
        compiler_params=pltpu.CompilerParams(
            dimension_semantics=("arbitrary",), vmem_limit_bytes=VMEM_LIMIT_BYTES),
        name="proj_cast",
    )(h, g, w_in, cos, sin)


def _softmax_sink(s, sink):
    m = jnp.maximum(jnp.max(s, axis=-1, keepdims=True), sink)
    p = jnp.exp(s - m)
    return p, 1.0 / (jnp.sum(p, axis=-1, keepdims=True) + jnp.exp(sink - m))


def _gate_out(o, g):
    return (_rms_scale(o) * _silu(g)).astype(BF16)


WO_BATCHES = 2


def _mix_prompt_kernel(sink_ref, cdec_ref, qs_ref, qr_ref, vr_ref, kr_ref, g_ref, kvc_ref,
                       bias_ref, dmask_ref, qdec_ref, kdec_ref, wo_ref, h_ref, out_ref, st_ref, cat_ref, kvp_ref):
    n = pl.program_id(0)

    @pl.when(n == 0)
    def _():
        st_ref[...] = jnp.zeros(st_ref.shape, F32)
        kvp_ref[...] = jnp.zeros(kvp_ref.shape, BF16)

    scale = SWA_HEAD_DIM ** -0.5
    v_off = SWA_KV_HEADS * SWA_HEAD_DIM

    def wo_rows(b0):
        bs = slice(b0, b0 + WO_BATCHES)
        cat = cat_ref[bs].reshape(WO_BATCHES * BLOCK, D_MODEL)
        out_ref[bs] = h_ref[bs] + _dot(cat, wo_ref[...]).reshape(WO_BATCHES, BLOCK, D_MODEL)

    def attention(b, kh):
        c0 = kh * SWA_HEAD_DIM
        k_cur = kvc_ref[b, :, c0:c0 + SWA_HEAD_DIM].astype(BF16)
        v_cur = kvc_ref[b, :, v_off + c0:v_off + c0 + SWA_HEAD_DIM].astype(BF16)
        k_cat = jnp.concatenate([kvp_ref[b, :, c0:c0 + SWA_HEAD_DIM], k_cur], axis=0)
        v_cat = jnp.concatenate([kvp_ref[b, :, v_off + c0:v_off + c0 + SWA_HEAD_DIM], v_cur], axis=0)
        kvp_ref[b, :, c0:c0 + SWA_HEAD_DIM] = k_cur
        kvp_ref[b, :, v_off + c0:v_off + c0 + SWA_HEAD_DIM] = v_cur
        for gq in range(SWA_GROUP):
            h = kh * SWA_GROUP + gq
            q = qs_ref[b, :, h * SWA_HEAD_DIM:(h + 1) * SWA_HEAD_DIM]
            s = _dot_nt(q, k_cat) * scale + bias_ref[h]
            p, inv = _softmax_sink(s, sink_ref[h])
            o = _dot(p.astype(BF16), v_cat) * inv
            cat_ref[b, :, h * SWA_HEAD_DIM:(h + 1) * SWA_HEAD_DIM] = o.astype(BF16)

    def retention(b, h):
        cs = slice(h * RET_KEY_DIM, (h + 1) * RET_KEY_DIM)
        q = qr_ref[b, :, cs]
        k32 = kr_ref[b, :, cs]
        v = vr_ref[b, :, cs]
        s = _dot_nt(q, k32.astype(BF16)) * dmask_ref[h]
        o = _dot(s.astype(BF16), v)
        st = st_ref[b, h]
        o = o + _dot(q, st.astype(BF16)) * qdec_ref[:, cs]
        kd = (k32 * kdec_ref[:, cs]).astype(BF16)
        st_ref[b, h] = cdec_ref[h] * st + _dot_tn(kd, v)
        cat_ref[b, :, SWA_WIDTH + h * RET_VAL_DIM:SWA_WIDTH + (h + 1) * RET_VAL_DIM] = (
            _gate_out(o, g_ref[b, :, cs]))

    for b in range(BATCH):
        for kh in range(SWA_KV_HEADS):
            attention(b, kh)
        for h in range(RET_HEADS):
            retention(b, h)
        if (b + 1) % WO_BATCHES == 0:
            wo_rows(b + 1 - WO_BATCHES)


def _mix_prompt_call(proj, side, h, w_o_bf, sinks_l, cdec, bias, dmask, qdec, kdec):
    nblk = SEQ // BLOCK
    proj3 = proj.reshape(BATCH, SEQ, PROJ_COLS)
    side3 = side.reshape(BATCH, SEQ, SIDE_COLS)
    h3 = h.reshape(BATCH, SEQ, D_MODEL)
    smem = pl.BlockSpec(memory_space=pltpu.SMEM)
    full = lambda shape: _resident(shape, lambda n: (0,) * len(shape))
    cur = lambda n: n
    prv = lambda n: n
    out, st = pl.pallas_call(
        _mix_prompt_kernel,
        grid=(nblk,),
        in_specs=[
            smem, smem,
            pl.BlockSpec((BATCH, BLOCK, 1024), lambda n: (0, cur(n), 0)),
            pl.BlockSpec((BATCH, BLOCK, 1024), lambda n: (0, cur(n), 1)),
            pl.BlockSpec((BATCH, BLOCK, 1024), lambda n: (0, cur(n), 2)),
            pl.BlockSpec((BATCH, BLOCK, 1024), lambda n: (0, cur(n), 0)),
            pl.BlockSpec((BATCH, BLOCK, 1024), lambda n: (0, cur(n), 1)),
            pl.BlockSpec((BATCH, BLOCK, 512), lambda n: (0, cur(n), 4)),
            pl.BlockSpec((None, SWA_HEADS, BLOCK, 2 * BLOCK), lambda n: (jnp.minimum(n, 1), 0, 0, 0)),
            full((RET_HEADS, RET_CHUNK, RET_CHUNK)),
            full((RET_CHUNK, RET_WIDTH)),
            full((RET_CHUNK, RET_WIDTH)),
            full((D_MODEL, D_MODEL)),
            pl.BlockSpec((BATCH, BLOCK, D_MODEL), lambda n: (0, prv(n), 0)),
        ],
        out_specs=[
            pl.BlockSpec((BATCH, BLOCK, D_MODEL), lambda n: (0, prv(n), 0)),
            _resident((BATCH, RET_HEADS, RET_KEY_DIM, RET_VAL_DIM), lambda n: (0, 0, 0, 0)),
        ],
        out_shape=[jax.ShapeDtypeStruct((BATCH, SEQ, D_MODEL), F32),
                   jax.ShapeDtypeStruct((BATCH, RET_HEADS, RET_KEY_DIM, RET_VAL_DIM), F32)],
        scratch_shapes=[pltpu.VMEM((BATCH, BLOCK, D_MODEL), BF16),
                        pltpu.VMEM((BATCH, BLOCK, 2 * SWA_KV_HEADS * SWA_HEAD_DIM), BF16)],
        compiler_params=pltpu.CompilerParams(
            dimension_semantics=("arbitrary",), vmem_limit_bytes=VMEM_LIMIT_BYTES),
        name="mix_prompt",
    )(sinks_l, cdec, proj3, proj3, proj3, side3, side3, side3, bias, dmask, qdec, kdec, w_o_bf, h3)
    return out.reshape(BATCH * SEQ, D_MODEL), st


SAMPLE_BB = 4
SAMPLE_ROWS = SAMPLE_BB * DEC_SEQ
N_SAMPLE = DEC_BATCH * DEC_SEQ
CACHE_ROWS = WINDOW * SWA_KV_HEADS


def _mix_sample_kernel(sink_ref, cdec_ref, prow_ref, srow_ref, pall_ref, sall_ref, ck_ref, cv_ref,
                       knew_ref, vnew_ref, st_ref, bias_ref, dmask_ref, qdec_ref, kdec_ref,
                       cat_ref, stout_ref, ckout_ref, cvout_ref):
    r = SAMPLE_ROWS
    row_b = lax.broadcasted_iota(jnp.int32, (r, 1), 0) // DEC_SEQ
    row_b4 = lax.broadcasted_iota(jnp.int32, (SWA_GROUP * r, 1), 0) % r // DEC_SEQ
    scale = SWA_HEAD_DIM ** -0.5
    v_off = SWA_KV_HEADS * SWA_HEAD_DIM
    new_rows = DEC_SEQ * SWA_KV_HEADS

    for bi in range(SAMPLE_BB):
        ckout_ref[bi, :CACHE_ROWS - new_rows] = ck_ref[bi, new_rows:]
        ckout_ref[bi, CACHE_ROWS - new_rows:] = knew_ref[bi]
        cvout_ref[bi, :CACHE_ROWS - new_rows] = cv_ref[bi, new_rows:]
        cvout_ref[bi, CACHE_ROWS - new_rows:] = vnew_ref[bi]

    for kh in range(SWA_KV_HEADS):
        c0 = kh * SWA_HEAD_DIM
        q4 = jnp.concatenate(
            [prow_ref[:, (kh * SWA_GROUP + gq) * SWA_HEAD_DIM:(kh * SWA_GROUP + gq + 1) * SWA_HEAD_DIM]
             for gq in range(SWA_GROUP)], axis=0)
        k_new = sall_ref[:, KV_COL + c0:KV_COL + c0 + SWA_HEAD_DIM].astype(BF16)
        v_new = sall_ref[:, KV_COL + v_off + c0:KV_COL + v_off + c0 + SWA_HEAD_DIM].astype(BF16)
        s_cache = jnp.zeros((SWA_GROUP * r, CACHE_ROWS), F32)
        for bi in range(SAMPLE_BB):
            s_cache = jnp.where(row_b4 == bi, _dot_nt(q4, ck_ref[bi].astype(BF16)), s_cache)
        s_new = _dot_nt(q4, k_new)
        s4 = jnp.concatenate([s_cache, s_new], axis=1) * scale
        p_parts, inv_parts = [], []
        for gq in range(SWA_GROUP):
            h = kh * SWA_GROUP + gq
            s = s4[gq * r:(gq + 1) * r] + bias_ref[kh, h]
            p, inv = _softmax_sink(s, sink_ref[h])
            p_parts.append(p)
            inv_parts.append(inv)
        p4 = jnp.concatenate(p_parts, axis=0)
        inv4 = jnp.concatenate(inv_parts, axis=0)
        p_cache = p4[:, :CACHE_ROWS]
        o4 = _dot(p4[:, CACHE_ROWS:].astype(BF16), v_new)
        for bi in range(SAMPLE_BB):
            o4 = o4 + _dot(jnp.where(row_b4 == bi, p_cache, 0.0).astype(BF16), cv_ref[bi].astype(BF16))
        o4 = o4 * inv4
        for gq in range(SWA_GROUP):
            h = kh * SWA_GROUP + gq
            cat_ref[:, h * SWA_HEAD_DIM:(h + 1) * SWA_HEAD_DIM] = o4[gq * r:(gq + 1) * r].astype(BF16)

    for h in range(RET_HEADS):
        cs = slice(h * RET_KEY_DIM, (h + 1) * RET_KEY_DIM)
        q = prow_ref[:, 1024 + h * RET_KEY_DIM:1024 + (h + 1) * RET_KEY_DIM]
        k_all = sall_ref[:, cs].astype(BF16)
        v_all = pall_ref[:, 2048 + h * RET_VAL_DIM:2048 + (h + 1) * RET_VAL_DIM]
        s = _dot_nt(q, k_all) * dmask_ref[h]
        o = _dot(s.astype(BF16), v_all)
        k32 = srow_ref[:, cs]
        v = prow_ref[:, 2048 + h * RET_VAL_DIM:2048 + (h + 1) * RET_VAL_DIM]
        kd = k32 * kdec_ref[:, cs]
        cross = jnp.zeros((r, RET_VAL_DIM), F32)
        for bi in range(SAMPLE_BB):
            st = st_ref[bi, h]
            cross = jnp.where(row_b == bi, _dot(q, st.astype(BF16)), cross)
            kd_b = jnp.where(row_b == bi, kd, 0.0).astype(BF16)
            stout_ref[bi, h] = cdec_ref[h] * st + _dot_tn(kd_b, v)
        o = o + cross * qdec_ref[:, cs]
        g = srow_ref[:, 1024 + h * RET_VAL_DIM:1024 + (h + 1) * RET_VAL_DIM]
        cat_ref[:, SWA_WIDTH + h * RET_VAL_DIM:SWA_WIDTH + (h + 1) * RET_VAL_DIM] = _gate_out(o, g)


def _mix_sample_call(proj, side, cache_k, cache_v, state_ret, layer, sinks_l, cdec, bias, dmask, qdec, kdec):
    r = SAMPLE_ROWS
    new_rows = DEC_SEQ * SWA_KV_HEADS
    kv_w = SWA_KV_HEADS * SWA_HEAD_DIM
    k_new = side[:, KV_COL:KV_COL + kv_w].reshape(DEC_BATCH, new_rows, SWA_HEAD_DIM)
    v_new = side[:, KV_COL + kv_w:KV_COL + 2 * kv_w].reshape(DEC_BATCH, new_rows, SWA_HEAD_DIM)
    smem = pl.BlockSpec(memory_space=pltpu.SMEM)
    full = lambda shape: pl.BlockSpec(shape, lambda c: (0,) * len(shape))
    cache_in = pl.BlockSpec((None, SAMPLE_BB, CACHE_ROWS, SWA_HEAD_DIM), lambda c: (layer, c, 0, 0))
    cache_out = pl.BlockSpec((SAMPLE_BB, CACHE_ROWS, SWA_HEAD_DIM), lambda c: (c, 0, 0))
    new_in = pl.BlockSpec((SAMPLE_BB, new_rows, SWA_HEAD_DIM), lambda c: (c, 0, 0))
    in_specs = [
        smem, smem,
        pl.BlockSpec((r, PROJ_COLS), lambda c: (c, 0)),
        pl.BlockSpec((r, SIDE_COLS), lambda c: (c, 0)),
        full((N_SAMPLE, PROJ_COLS)),
        full((N_SAMPLE, SIDE_COLS)),
        cache_in, cache_in, new_in, new_in,
        pl.BlockSpec((None, SAMPLE_BB, RET_HEADS, RET_KEY_DIM, RET_VAL_DIM), lambda c: (layer, c, 0, 0, 0)),
        pl.BlockSpec((SWA_KV_HEADS, SWA_HEADS, r, CACHE_ROWS + N_SAMPLE), lambda c: (0, 0, c, 0)),
        pl.BlockSpec((RET_HEADS, r, N_SAMPLE), lambda c: (0, c, 0)),
        pl.BlockSpec((r, RET_WIDTH), lambda c: (c, 0)),
        pl.BlockSpec((r, RET_WIDTH), lambda c: (c, 0)),
    ]
    cache_shape = jax.ShapeDtypeStruct((DEC_BATCH, CACHE_ROWS, SWA_HEAD_DIM), F32)
    return pl.pallas_call(
        _mix_sample_kernel,
        grid=(DEC_BATCH // SAMPLE_BB,),
        in_specs=in_specs,
        out_specs=[
            pl.BlockSpec((r, D_MODEL), lambda c: (c, 0)),
            pl.BlockSpec((SAMPLE_BB, RET_HEADS, RET_KEY_DIM, RET_VAL_DIM), lambda c: (c, 0, 0, 0)),
            cache_out, cache_out,
        ],
        out_shape=[jax.ShapeDtypeStruct((N_SAMPLE, D_MODEL), BF16),
                   jax.ShapeDtypeStruct((DEC_BATCH, RET_HEADS, RET_KEY_DIM, RET_VAL_DIM), F32),
                   cache_shape, cache_shape],
        compiler_params=pltpu.CompilerParams(
            dimension_semantics=("arbitrary",), vmem_limit_bytes=VMEM_LIMIT_BYTES),
        name="mix_sample",
    )(sinks_l, cdec, proj, side, proj, side, cache_k, cache_v, k_new, v_new, state_ret, bias, dmask, qdec, kdec)


WO_TN = 512


def _wo_cast_kernel(cat_ref, w_ref, h_ref, out_ref, wbf_ref):
    w_bf = w_ref[...].astype(BF16)
    wbf_ref[...] = w_bf
    out_ref[...] = h_ref[...] + _dot(cat_ref[...], w_bf)


def _wo_cast_call(cat, w_o, layer, h):
    m = h.shape[0]
    return pl.pallas_call(
        _wo_cast_kernel,
        grid=(D_MODEL // WO_TN,),
        in_specs=[
            _resident((m, D_MODEL), lambda j: (0, 0)),
            pl.BlockSpec((None, D_MODEL, WO_TN), lambda j: (layer, 0, j)),
            pl.BlockSpec((m, WO_TN), lambda j: (0, j)),
        ],
        out_specs=[pl.BlockSpec((m, WO_TN), lambda j: (0, j)),
                   pl.BlockSpec((D_MODEL, WO_TN), lambda j: (0, j))],
        out_shape=[jax.ShapeDtypeStruct((m, D_MODEL), F32),
                   jax.ShapeDtypeStruct((D_MODEL, D_MODEL), BF16)],
        compiler_params=pltpu.CompilerParams(
            dimension_semantics=("arbitrary",), vmem_limit_bytes=VMEM_LIMIT_BYTES),
        name="wo_cast",
    )(cat, w_o, h)


def _ffn_step(f, n_f, h_ref, g_ref, weights, gfin_ref, out_ref, xn_ref, final_norm):
    @pl.when(f == 0)
    def _():
        x = h_ref[...]
        xn_ref[...] = (_rms_scale(x) * g_ref[...]).astype(BF16)
        out_ref[...] = x

    w_gate, w_up, w_down = weights()
    xn = xn_ref[...]
    a = _dot(xn, w_gate)
    b = _dot(xn, w_up)
    act = (_silu(a) * b).astype(BF16)
    out_ref[...] += _dot(act, w_down)

    if final_norm:
        @pl.when(f == n_f - 1)
        def _():
            out_ref[...] = _rms_scale(out_ref[...]) * gfin_ref[...]


def _ffn_kernel(h_ref, g_ref, wg_ref, wu_ref, wd_ref, gfin_ref, out_ref, xn_ref, *, final_norm):
    _ffn_step(pl.program_id(1), pl.num_programs(1), h_ref, g_ref,
              lambda: (wg_ref[...], wu_ref[...], wd_ref[...]), gfin_ref, out_ref, xn_ref, final_norm)


def _ffn_call(h, g, w_g_bf, w_u_bf, w_d_bf, g_final, layer, tm, tf):
    m = h.shape[0]
    return pl.pallas_call(
        functools.partial(_ffn_kernel, final_norm=(layer == DEPTH - 1)),
        grid=(m // tm, D_FF // tf),
        in_specs=[
            pl.BlockSpec((tm, D_MODEL), lambda i, f: (i, 0)),
            pl.BlockSpec((None, 1, D_MODEL), lambda i, f: (layer, 0, 0)),
            pl.BlockSpec((D_MODEL, tf), lambda i, f: (0, f)),
            pl.BlockSpec((D_MODEL, tf), lambda i, f: (0, f)),
            pl.BlockSpec((tf, D_MODEL), lambda i, f: (f, 0)),
            pl.BlockSpec((1, D_MODEL), lambda i, f: (0, 0)),
        ],
        out_specs=pl.BlockSpec((tm, D_MODEL), lambda i, f: (i, 0)),
        out_shape=jax.ShapeDtypeStruct((m, D_MODEL), F32),
        scratch_shapes=[pltpu.VMEM((tm, D_MODEL), BF16)],
        compiler_params=pltpu.CompilerParams(
            dimension_semantics=("arbitrary", "arbitrary"), vmem_limit_bytes=VMEM_LIMIT_BYTES),
        name="ffn",
    )(h, g, w_g_bf, w_u_bf, w_d_bf, g_final)


def _ffn_cast_kernel(h_ref, g_ref, wg_ref, wu_ref, wd_ref, gfin_ref, out_ref, wgbf_ref, wubf_ref, wdbf_ref,
                     xn_ref, *, final_norm):
    def weights():
        wgbf_ref[...] = wg_ref[...].astype(BF16)
        wubf_ref[...] = wu_ref[...].astype(BF16)
        wdbf_ref[...] = wd_ref[...].astype(BF16)
        return wgbf_ref[...], wubf_ref[...], wdbf_ref[...]

    _ffn_step(pl.program_id(0), pl.num_programs(0), h_ref, g_ref, weights, gfin_ref, out_ref, xn_ref, final_norm)


def _ffn_cast_call(h, g, w_gate_up, w_down, g_final, layer, tf):
    m = h.shape[0]
    nf = D_FF // tf
    return pl.pallas_call(
        functools.partial(_ffn_cast_kernel, final_norm=(layer == DEPTH - 1)),
        grid=(nf,),
        in_specs=[
            _resident((m, D_MODEL), lambda f: (0, 0)),
            _resident((None, 1, D_MODEL), lambda f: (layer, 0, 0)),
            pl.BlockSpec((None, D_MODEL, tf), lambda f: (layer, 0, f)),
            pl.BlockSpec((None, D_MODEL, tf), lambda f: (layer, 0, f + nf)),
            pl.BlockSpec((None, tf, D_MODEL), lambda f: (layer, f, 0)),
            _resident((1, D_MODEL), lambda f: (0, 0)),
        ],
        out_specs=[
            pl.BlockSpec((m, D_MODEL), lambda f: (0, 0)),
            pl.BlockSpec((D_MODEL, tf), lambda f: (0, f)),
            pl.BlockSpec((D_MODEL, tf), lambda f: (0, f)),
            pl.BlockSpec((tf, D_MODEL), lambda f: (f, 0)),
        ],
        out_shape=[jax.ShapeDtypeStruct((m, D_MODEL), F32),
                   jax.ShapeDtypeStruct((D_MODEL, D_FF), BF16),
                   jax.ShapeDtypeStruct((D_MODEL, D_FF), BF16),
                   jax.ShapeDtypeStruct((D_FF, D_MODEL), BF16)],
        scratch_shapes=[pltpu.VMEM((m, D_MODEL), BF16)],
        compiler_params=pltpu.CompilerParams(
            dimension_semantics=("arbitrary",), vmem_limit_bytes=VMEM_LIMIT_BYTES),
        name="ffn_cast",
    )(h, g, w_gate_up, w_gate_up, w_down, g_final)


def _rope_tables(pos):
    half = RET_KEY_DIM // 2
    inv = 1.0 / (ROPE_BASE ** jnp.linspace(0.0, 1.0, half, dtype=F32))
    ang = pos.astype(F32)[:, None] * inv[None, :]
    return jnp.cos(ang), jnp.sin(ang)


def _decay_tables(c):
    lg = jnp.log(1.0 - jnp.exp2(-5.0 - jnp.arange(RET_HEADS, dtype=F32)))
    idx = jnp.arange(c, dtype=F32)
    diff = idx[:, None] - idx[None, :]
    dmask = jnp.where(diff[None] >= 0, jnp.exp(jnp.maximum(diff, 0.0)[None] * lg[:, None, None]), 0.0)
    q_decay = jnp.exp((idx[:, None] + 1.0) * lg[None, :])
    k_decay = jnp.exp((c - 1.0 - idx)[:, None] * lg[None, :])
    c_decay = jnp.exp(c * lg)
    return dmask, q_decay, k_decay, c_decay


def _per_head_cols(t):
    return jnp.repeat(t, RET_KEY_DIM, axis=1)


def kernel(x_prompt, x_sample, cache_k_win, cache_v_win, state_ret, rel_bias, w_in, sinks, w_o,
           norm_mix, norm_ffn, w_gate_up, w_down, norm_final):
    norm_mix3 = norm_mix.reshape(DEPTH, 1, D_MODEL)
    norm_ffn3 = norm_ffn.reshape(DEPTH, 1, D_MODEL)
    norm_final2 = norm_final.reshape(1, D_MODEL)
    cache_k = cache_k_win.reshape(DEPTH, DEC_BATCH, CACHE_ROWS, SWA_HEAD_DIM)
    cache_v = cache_v_win.reshape(DEPTH, DEC_BATCH, CACHE_ROWS, SWA_HEAD_DIM)

    cos_p, sin_p = _rope_tables(jnp.arange(SEQ, dtype=jnp.int32))
    cos_4, sin_4 = _rope_tables(PAST_LEN + jnp.arange(DEC_SEQ, dtype=jnp.int32))
    cos_s = jnp.tile(cos_4, (DEC_BATCH, 1))
    sin_s = jnp.tile(sin_4, (DEC_BATCH, 1))

    dmask_p, qd_p, kd_p, cdec_p = _decay_tables(RET_CHUNK)
    qdec_p = _per_head_cols(qd_p)
    kdec_p = _per_head_cols(kd_p)
    dmask_4, qd_4, kd_4, cdec_s = _decay_tables(DEC_SEQ)
    eye_b = jnp.eye(DEC_BATCH, dtype=F32)
    dmask_s = jax.vmap(lambda d: jnp.kron(eye_b, d))(dmask_4)
    qdec_s = jnp.tile(_per_head_cols(qd_4), (DEC_BATCH, 1))
    kdec_s = jnp.tile(_per_head_cols(kd_4), (DEC_BATCH, 1))

    qi = np.arange(BLOCK)[:, None]
    kj = np.arange(2 * BLOCK)[None, :]
    delta_p = qi + BLOCK - kj
    in_window = (delta_p >= 0) & (delta_p < WINDOW)
    valid_p = np.stack([in_window & (kj >= BLOCK), in_window]).astype(np.float32)
    rows = np.arange(N_SAMPLE)
    rb, rt = rows // DEC_SEQ, rows % DEC_SEQ
    cache_pos = np.arange(CACHE_ROWS) // SWA_KV_HEADS
    cache_head = np.arange(CACHE_ROWS) % SWA_KV_HEADS
    delta_cache = (WINDOW + rt)[:, None] - cache_pos[None, :]
    delta_new = rt[:, None] - rt[None, :]
    same_b = rb[:, None] == rb[None, :]
    delta_s = np.concatenate([delta_cache, delta_new], axis=1)
    valid_s = np.stack([
        np.concatenate([(delta_cache < WINDOW) & (cache_head == kh)[None, :], same_b & (delta_new >= 0)], axis=1)
        for kh in range(SWA_KV_HEADS)]).astype(np.float32)
    rel_bias_t = rel_bias.T
    bias_p = _expand_bias(rel_bias_t, jnp.asarray(_t5_bucket_np(delta_p)), jnp.asarray(valid_p))
    bias_s = _expand_bias(rel_bias_t, jnp.asarray(_t5_bucket_np(delta_s)), jnp.asarray(valid_s))

    hp = x_prompt.reshape(BATCH * SEQ, D_MODEL)
    hs = x_sample.reshape(N_SAMPLE, D_MODEL)
    tm_proj, tm_ffn, tf = 512, 1024, 512
    kp_new, vp_new, rp_new, ks_new, vs_new, rs_new = [], [], [], [], [], []
    kv_w = SWA_KV_HEADS * SWA_HEAD_DIM
    for l in range(DEPTH):
        proj, side, w_in_bf = _proj_cast_call(hs, norm_mix3, l, w_in, cos_s, sin_s)
        cat, st_s, k_win, v_win = _mix_sample_call(proj, side, cache_k, cache_v, state_ret, l, sinks[l], cdec_s,
                                                   bias_s, dmask_s, qdec_s, kdec_s)
        rs_new.append(st_s)
        ks_new.append(k_win)
        vs_new.append(v_win)
        hs, w_o_bf = _wo_cast_call(cat, w_o, l, hs)
        hs, w_g_bf, w_u_bf, w_d_bf = _ffn_cast_call(hs, norm_ffn3, w_gate_up, w_down, norm_final2, l, tf)

        proj, side = _proj_call(hp, norm_mix3, l, w_in_bf, cos_p, sin_p, tm_proj)
        hp, st_p = _mix_prompt_call(proj, side, hp, w_o_bf, sinks[l], cdec_p, bias_p, dmask_p, qdec_p, kdec_p)
        hp = _ffn_call(hp, norm_ffn3, w_g_bf, w_u_bf, w_d_bf, norm_final2, l, tm_ffn, tf)
        kv_tail = side.reshape(BATCH, SEQ, SIDE_COLS)[:, SEQ - WINDOW:, KV_COL:KV_COL + 2 * kv_w]
        kp_new.append(kv_tail[..., :kv_w].reshape(BATCH, WINDOW, SWA_KV_HEADS, SWA_HEAD_DIM))
        vp_new.append(kv_tail[..., kv_w:].reshape(BATCH, WINDOW, SWA_KV_HEADS, SWA_HEAD_DIM))
        rp_new.append(st_p)

    y_prompt = hp.reshape(BATCH, SEQ, D_MODEL)
    y_sample = hs.reshape(DEC_BATCH, DEC_SEQ, D_MODEL)
    return (y_prompt, y_sample,
            jnp.stack(kp_new), jnp.stack(vp_new), jnp.stack(rp_new),
            jnp.stack(ks_new).reshape(cache_k_win.shape), jnp.stack(vs_new).reshape(cache_v_win.shape),
            jnp.stack(rs_new))
```

```python
import functools
import math

import numpy as np
import jax
import jax.numpy as jnp
from jax import lax
from jax.experimental import pallas as pl
from jax.experimental.pallas import tpu as pltpu

D_MODEL = 2048
BATCH = 4
SEQ = 2048
DEPTH = 4
DEC_BATCH = 32
DEC_SEQ = 4
PAST_LEN = 16384

SWA_WIDTH = 1024
RET_WIDTH = 1024
SWA_HEADS = 8
SWA_KV_HEADS = 2
SWA_GROUP = SWA_HEADS // SWA_KV_HEADS
SWA_HEAD_DIM = 128
WINDOW = 128
BLOCK = WINDOW
RET_HEADS = 4
RET_KEY_DIM = 256
RET_VAL_DIM = 256
RET_CHUNK = 128
ROPE_BASE = 10000.0
N_BUCKETS = 32
MAX_DISTANCE = 128
EPS = 1e-6
D_FF = 5632
IN_COLS = 5632

F32 = jnp.float32
BF16 = jnp.bfloat16

VMEM_LIMIT_BYTES = 60 * 1024 * 1024

PROJ_TN = 512
PROJ_COLS = 3072
SIDE_COLS = 2560
KV_COL = 2048
PROJ_TILES = (
    ("proj", 0, "plain"), ("proj", 512, "plain"),
    ("side", KV_COL, "plain"),
    ("proj", 1024, "rotary"), ("proj", 1536, "rotary"),
    ("side", 0, "rotary_k"), ("side", 512, "rotary_k"),
    ("proj", 2048, "plain"), ("proj", 2560, "plain"),
    ("side", 1024, "plain"), ("side", 1536, "plain"),
)


def _rms_scale(x):
    return x * lax.rsqrt(jnp.mean(x * x, axis=-1, keepdims=True) + EPS)


def _silu(x):
    return x * jax.nn.sigmoid(x)


def _dot(a, b):
    return jnp.dot(a, b, preferred_element_type=F32)


def _dot_nt(a, b):
    return lax.dot_general(a, b, (((1,), (1,)), ((), ())), preferred_element_type=F32)


def _dot_tn(a, b):
    return lax.dot_general(a, b, (((0,), (0,)), ((), ())), preferred_element_type=F32)


def _resident(block_shape, index_map):
    return pl.BlockSpec(block_shape, index_map, pipeline_mode=pl.Buffered(1))


MASKED = -1e30


def _bias_kernel(rbt_ref, idx_ref, valid_ref, out_ref):
    h = pl.program_id(1)
    idx = idx_ref[...]
    acc = jnp.zeros(idx.shape, F32)
    for b in range(N_BUCKETS):
        acc = jnp.where(idx == b, rbt_ref[h, b], acc)
    out_ref[...] = jnp.where(valid_ref[...] > 0.5, acc, MASKED)


def _expand_bias(rel_bias_t, bucket_idx, valid):
    nv, rows, cols = valid.shape
    return pl.pallas_call(
        _bias_kernel,
        grid=(nv, SWA_HEADS),
        in_specs=[pl.BlockSpec(memory_space=pltpu.SMEM),
                  pl.BlockSpec((rows, cols), lambda v, h: (0, 0)),
                  pl.BlockSpec((None, rows, cols), lambda v, h: (v, 0, 0))],
        out_specs=pl.BlockSpec((None, None, rows, cols), lambda v, h: (v, h, 0, 0)),
        out_shape=jax.ShapeDtypeStruct((nv, SWA_HEADS, rows, cols), F32),
        name="bias_expand",
    )(rel_bias_t, bucket_idx, valid)


def _t5_bucket_np(delta):
    n = np.maximum(delta, 0)
    max_exact = N_BUCKETS // 2
    nf = np.maximum(n, 1).astype(np.float64)
    large = max_exact + (np.log(nf / max_exact) / math.log(MAX_DISTANCE / max_exact)
                         * (N_BUCKETS - max_exact)).astype(np.int32)
    large = np.minimum(large, N_BUCKETS - 1)
    return np.where(n < max_exact, n, large).astype(np.int32)


def _proj_epilogue(acc, mode, cos_ref, sin_ref, out_ref, col):
    if mode == "plain":
        out_ref[:, col:col + PROJ_TN] = acc.astype(out_ref.dtype)
        return
    half = RET_KEY_DIM // 2
    cos = cos_ref[...]
    sin = sin_ref[...]
    for c0 in range(0, PROJ_TN, RET_KEY_DIM):
        x1 = acc[:, c0:c0 + half]
        x2 = acc[:, c0 + half:c0 + RET_KEY_DIM]
        o1 = x1 * cos - x2 * sin
        o2 = x1 * sin + x2 * cos
        if mode == "rotary_k":
            o1 = o1 * (RET_KEY_DIM ** -0.5)
            o2 = o2 * (RET_KEY_DIM ** -0.5)
        out_ref[:, col + c0:col + c0 + half] = o1.astype(out_ref.dtype)
        out_ref[:, col + c0 + half:col + c0 + RET_KEY_DIM] = o2.astype(out_ref.dtype)


def _proj_kernel(h_ref, g_ref, w_ref, cos_ref, sin_ref, proj_ref, side_ref, xn_ref):
    xn_ref[...] = (_rms_scale(h_ref[...]) * g_ref[...]).astype(BF16)
    outs = {"proj": proj_ref, "side": side_ref}
    for t, (dst, col, mode) in enumerate(PROJ_TILES):
        acc = _dot(xn_ref[...], w_ref[:, t * PROJ_TN:(t + 1) * PROJ_TN])
        _proj_epilogue(acc, mode, cos_ref, sin_ref, outs[dst], col)


def _proj_call(h, g, layer, w_in_bf, cos, sin, tm):
    m = h.shape[0]
    cos_blocks = cos.shape[0] // tm
    return pl.pallas_call(
        _proj_kernel,
        grid=(m // tm,),
        in_specs=[
            pl.BlockSpec((tm, D_MODEL), lambda i: (i, 0)),
            _resident((None, 1, D_MODEL), lambda i: (layer, 0, 0)),
            _resident((D_MODEL, IN_COLS), lambda i: (0, 0)),
            pl.BlockSpec((tm, RET_KEY_DIM // 2), lambda i: (i % cos_blocks, 0)),
            pl.BlockSpec((tm, RET_KEY_DIM // 2), lambda i: (i % cos_blocks, 0)),
        ],
        out_specs=[
            pl.BlockSpec((tm, PROJ_COLS), lambda i: (i, 0)),
            pl.BlockSpec((tm, SIDE_COLS), lambda i: (i, 0)),
        ],
        out_shape=[jax.ShapeDtypeStruct((m, PROJ_COLS), BF16),
                   jax.ShapeDtypeStruct((m, SIDE_COLS), F32)],
        scratch_shapes=[pltpu.VMEM((tm, D_MODEL), BF16)],
        compiler_params=pltpu.CompilerParams(
            dimension_semantics=("arbitrary",), vmem_limit_bytes=VMEM_LIMIT_BYTES),
        name="proj",
    )(h, g, w_in_bf, cos, sin)


def _proj_cast_kernel(h_ref, g_ref, w_ref, cos_ref, sin_ref, proj_ref, side_ref, wbf_ref, xn_ref):
    j = pl.program_id(0)

    @pl.when(j == 0)
    def _():
        xn_ref[...] = (_rms_scale(h_ref[...]) * g_ref[...]).astype(BF16)

    w_bf = w_ref[...].astype(BF16)
    wbf_ref[...] = w_bf
    acc = _dot(xn_ref[...], w_bf)
    outs = {"proj": proj_ref, "side": side_ref}
    for t, (dst, col, mode) in enumerate(PROJ_TILES):
        @pl.when(j == t)
        def _():
            _proj_epilogue(acc, mode, cos_ref, sin_ref, outs[dst], col)


def _proj_cast_call(h, g, layer, w_in, cos, sin):
    m = h.shape[0]
    full = lambda shape: _resident(shape, lambda j: (0,) * len(shape))
    return pl.pallas_call(
        _proj_cast_kernel,
        grid=(len(PROJ_TILES),),
        in_specs=[
            full((m, D_MODEL)),
            _resident((None, 1, D_MODEL), lambda j: (layer, 0, 0)),
            pl.BlockSpec((None, D_MODEL, PROJ_TN), lambda j: (layer, 0, j)),
            full((m, RET_KEY_DIM // 2)),
            full((m, RET_KEY_DIM // 2)),
        ],
        out_specs=[
            pl.BlockSpec((m, PROJ_COLS), lambda j: (0, 0)),
            pl.BlockSpec((m, SIDE_COLS), lambda j: (0, 0)),
            pl.BlockSpec((D_MODEL, PROJ_TN), lambda j: (0, j)),
        ],
        out_shape=[jax.ShapeDtypeStruct((m, PROJ_COLS), BF16),
                   jax.ShapeDtypeStruct((m, SIDE_COLS), F32),
                   jax.ShapeDtypeStruct((D_MODEL, IN_COLS), BF16)],
        scratch_shapes=[pltpu.VMEM((m, D_MODEL), BF16)],
        compiler_params=pltpu.CompilerParams(
            dimension_semantics=("arbitrary",), vmem_limit_bytes=VMEM_LIMIT_BYTES),
        name="proj_cast",
    )(h, g, w_in, cos, sin)


def _softmax_sink(s, sink):
    m = jnp.maximum(jnp.max(s, axis=-1, keepdims=True), sink)
    p = jnp.exp(s - m)
    return p, 1.0 / (jnp.sum(p, axis=-1, keepdims=True) + jnp.exp(sink - m))


def _gate_out(o, g):
    return (_rms_scale(o) * _silu(g)).astype(BF16)


WO_BATCHES = 2


def _mix_prompt_kernel(sink_ref, cdec_ref, qs_ref, qr_ref, vr_ref, kr_ref, g_ref, kvc_ref,
                       bias_ref, dmask_ref, qdec_ref, kdec_ref, wo_ref, h_ref, out_ref, st_ref, cat_ref, kvp_ref):
    n = pl.program_id(0)

    @pl.when(n == 0)
    def _():
        st_ref[...] = jnp.zeros(st_ref.shape, F32)
        kvp_ref[...] = jnp.zeros(kvp_ref.shape, BF16)

    scale = SWA_HEAD_DIM ** -0.5
    v_off = SWA_KV_HEADS * SWA_HEAD_DIM

    def wo_rows(b0):
        bs = slice(b0, b0 + WO_BATCHES)
        cat = cat_ref[bs].reshape(WO_BATCHES * BLOCK, D_MODEL)
        out_ref[bs] = h_ref[bs] + _dot(cat, wo_ref[...]).reshape(WO_BATCHES, BLOCK, D_MODEL)

    def attention(b, kh):
        c0 = kh * SWA_HEAD_DIM
        k_cur = kvc_ref[b, :, c0:c0 + SWA_HEAD_DIM].astype(BF16)
        v_cur = kvc_ref[b, :, v_off + c0:v_off + c0 + SWA_HEAD_DIM].astype(BF16)
        k_cat = jnp.concatenate([kvp_ref[b, :, c0:c0 + SWA_HEAD_DIM], k_cur], axis=0)
        v_cat = jnp.concatenate([kvp_ref[b, :, v_off + c0:v_off + c0 + SWA_HEAD_DIM], v_cur], axis=0)
        kvp_ref[b, :, c0:c0 + SWA_HEAD_DIM] = k_cur
        kvp_ref[b, :, v_off + c0:v_off + c0 + SWA_HEAD_DIM] = v_cur
        for gq in range(SWA_GROUP):
            h = kh * SWA_GROUP + gq
            q = qs_ref[b, :, h * SWA_HEAD_DIM:(h + 1) * SWA_HEAD_DIM]
            s = _dot_nt(q, k_cat) * scale + bias_ref[h]
            p, inv = _softmax_sink(s, sink_ref[h])
            o = _dot(p.astype(BF16), v_cat) * inv
            cat_ref[b, :, h * SWA_HEAD_DIM:(h + 1) * SWA_HEAD_DIM] = o.astype(BF16)

    def retention(b, h):
        cs = slice(h * RET_KEY_DIM, (h + 1) * RET_KEY_DIM)
        q = qr_ref[b, :, cs]
        k32 = kr_ref[b, :, cs]
        v = vr_ref[b, :, cs]
        s = _dot_nt(q, k32.astype(BF16)) * dmask_ref[h]
        o = _dot(s.astype(BF16), v)
        st = st_ref[b, h]
        o = o + _dot(q, st.astype(BF16)) * qdec_ref[:, cs]
        kd = (k32 * kdec_ref[:, cs]).astype(BF16)
        st_ref[b, h] = cdec_ref[h] * st + _dot_tn(kd, v)
        cat_ref[b, :, SWA_WIDTH + h * RET_VAL_DIM:SWA_WIDTH + (h + 1) * RET_VAL_DIM] = (
            _gate_out(o, g_ref[b, :, cs]))

    for b in range(BATCH):
        for kh in range(SWA_KV_HEADS):
            attention(b, kh)
        for h in range(RET_HEADS):
            retention(b, h)
        if (b + 1) % WO_BATCHES == 0:
            wo_rows(b + 1 - WO_BATCHES)


def _mix_prompt_call(proj, side, h, w_o_bf, sinks_l, cdec, bias, dmask, qdec, kdec):
    nblk = SEQ // BLOCK
    proj3 = proj.reshape(BATCH, SEQ, PROJ_COLS)
    side3 = side.reshape(BATCH, SEQ, SIDE_COLS)
    h3 = h.reshape(BATCH, SEQ, D_MODEL)
    smem = pl.BlockSpec(memory_space=pltpu.SMEM)
    full = lambda shape: _resident(shape, lambda n: (0,) * len(shape))
    cur = lambda n: n
    prv = lambda n: n
    out, st = pl.pallas_call(
        _mix_prompt_kernel,
        grid=(nblk,),
        in_specs=[
            smem, smem,
            pl.BlockSpec((BATCH, BLOCK, 1024), lambda n: (0, cur(n), 0)),
            pl.BlockSpec((BATCH, BLOCK, 1024), lambda n: (0, cur(n), 1)),
            pl.BlockSpec((BATCH, BLOCK, 1024), lambda n: (0, cur(n), 2)),
            pl.BlockSpec((BATCH, BLOCK, 1024), lambda n: (0, cur(n), 0)),
            pl.BlockSpec((BATCH, BLOCK, 1024), lambda n: (0, cur(n), 1)),
            pl.BlockSpec((BATCH, BLOCK, 512), lambda n: (0, cur(n), 4)),
            pl.BlockSpec((None, SWA_HEADS, BLOCK, 2 * BLOCK), lambda n: (jnp.minimum(n, 1), 0, 0, 0)),
            full((RET_HEADS, RET_CHUNK, RET_CHUNK)),
            full((RET_CHUNK, RET_WIDTH)),
            full((RET_CHUNK, RET_WIDTH)),
            full((D_MODEL, D_MODEL)),
            pl.BlockSpec((BATCH, BLOCK, D_MODEL), lambda n: (0, prv(n), 0)),
        ],
        out_specs=[
            pl.BlockSpec((BATCH, BLOCK, D_MODEL), lambda n: (0, prv(n), 0)),
            _resident((BATCH, RET_HEADS, RET_KEY_DIM, RET_VAL_DIM), lambda n: (0, 0, 0, 0)),
        ],
        out_shape=[jax.ShapeDtypeStruct((BATCH, SEQ, D_MODEL), F32),
                   jax.ShapeDtypeStruct((BATCH, RET_HEADS, RET_KEY_DIM, RET_VAL_DIM), F32)],
        scratch_shapes=[pltpu.VMEM((BATCH, BLOCK, D_MODEL), BF16),
                        pltpu.VMEM((BATCH, BLOCK, 2 * SWA_KV_HEADS * SWA_HEAD_DIM), BF16)],
        compiler_params=pltpu.CompilerParams(
            dimension_semantics=("arbitrary",), vmem_limit_bytes=VMEM_LIMIT_BYTES),
        name="mix_prompt",
    )(sinks_l, cdec, proj3, proj3, proj3, side3, side3, side3, bias, dmask, qdec, kdec, w_o_bf, h3)
    return out.reshape(BATCH * SEQ, D_MODEL), st


SAMPLE_BB = 4
SAMPLE_ROWS = SAMPLE_BB * DEC_SEQ
N_SAMPLE = DEC_BATCH * DEC_SEQ
CACHE_ROWS = WINDOW * SWA_KV_HEADS


def _mix_sample_kernel(sink_ref, cdec_ref, prow_ref, srow_ref, pall_ref, sall_ref, ck_ref, cv_ref,
                       knew_ref, vnew_ref, st_ref, bias_ref, dmask_ref, qdec_ref, kdec_ref, *rest, layer):
    cat_ref, stout_ref, ckout_ref, cvout_ref = rest[-4:]
    if layer == 0:
        for ref in (stout_ref, ckout_ref, cvout_ref):
            ref[1:] = jnp.zeros((DEPTH - 1,) + ref.shape[1:], F32)
        stout_ref, ckout_ref, cvout_ref = stout_ref.at[0], ckout_ref.at[0], cvout_ref.at[0]
    r = SAMPLE_ROWS
    row_b = lax.broadcasted_iota(jnp.int32, (r, 1), 0) // DEC_SEQ
    row_b4 = lax.broadcasted_iota(jnp.int32, (SWA_GROUP * r, 1), 0) % r // DEC_SEQ
    scale = SWA_HEAD_DIM ** -0.5
    v_off = SWA_KV_HEADS * SWA_HEAD_DIM
    new_rows = DEC_SEQ * SWA_KV_HEADS

    for bi in range(SAMPLE_BB):
        ckout_ref[bi, :CACHE_ROWS - new_rows] = ck_ref[bi, new_rows:]
        ckout_ref[bi, CACHE_ROWS - new_rows:] = knew_ref[bi]
        cvout_ref[bi, :CACHE_ROWS - new_rows] = cv_ref[bi, new_rows:]
        cvout_ref[bi, CACHE_ROWS - new_rows:] = vnew_ref[bi]

    for kh in range(SWA_KV_HEADS):
        c0 = kh * SWA_HEAD_DIM
        q4 = jnp.concatenate(
            [prow_ref[:, (kh * SWA_GROUP + gq) * SWA_HEAD_DIM:(kh * SWA_GROUP + gq + 1) * SWA_HEAD_DIM]
             for gq in range(SWA_GROUP)], axis=0)
        k_new = sall_ref[:, KV_COL + c0:KV_COL + c0 + SWA_HEAD_DIM].astype(BF16)
        v_new = sall_ref[:, KV_COL + v_off + c0:KV_COL + v_off + c0 + SWA_HEAD_DIM].astype(BF16)
        s_cache = jnp.zeros((SWA_GROUP * r, CACHE_ROWS), F32)
        for bi in range(SAMPLE_BB):
            s_cache = jnp.where(row_b4 == bi, _dot_nt(q4, ck_ref[bi].astype(BF16)), s_cache)
        s_new = _dot_nt(q4, k_new)
        s4 = jnp.concatenate([s_cache, s_new], axis=1) * scale
        p_parts, inv_parts = [], []
        for gq in range(SWA_GROUP):
            h = kh * SWA_GROUP + gq
            s = s4[gq * r:(gq + 1) * r] + bias_ref[kh, h]
            p, inv = _softmax_sink(s, sink_ref[h])
            p_parts.append(p)
            inv_parts.append(inv)
        p4 = jnp.concatenate(p_parts, axis=0)
        inv4 = jnp.concatenate(inv_parts, axis=0)
        p_cache = p4[:, :CACHE_ROWS]
        o4 = _dot(p4[:, CACHE_ROWS:].astype(BF16), v_new)
        for bi in range(SAMPLE_BB):
            o4 = o4 + _dot(jnp.where(row_b4 == bi, p_cache, 0.0).astype(BF16), cv_ref[bi].astype(BF16))
        o4 = o4 * inv4
        for gq in range(SWA_GROUP):
            h = kh * SWA_GROUP + gq
            cat_ref[:, h * SWA_HEAD_DIM:(h + 1) * SWA_HEAD_DIM] = o4[gq * r:(gq + 1) * r].astype(BF16)

    for h in range(RET_HEADS):
        cs = slice(h * RET_KEY_DIM, (h + 1) * RET_KEY_DIM)
        q = prow_ref[:, 1024 + h * RET_KEY_DIM:1024 + (h + 1) * RET_KEY_DIM]
        k_all = sall_ref[:, cs].astype(BF16)
        v_all = pall_ref[:, 2048 + h * RET_VAL_DIM:2048 + (h + 1) * RET_VAL_DIM]
        s = _dot_nt(q, k_all) * dmask_ref[h]
        o = _dot(s.astype(BF16), v_all)
        k32 = srow_ref[:, cs]
        v = prow_ref[:, 2048 + h * RET_VAL_DIM:2048 + (h + 1) * RET_VAL_DIM]
        kd = k32 * kdec_ref[:, cs]
        cross = jnp.zeros((r, RET_VAL_DIM), F32)
        for bi in range(SAMPLE_BB):
            st = st_ref[bi, h]
            cross = jnp.where(row_b == bi, _dot(q, st.astype(BF16)), cross)
            kd_b = jnp.where(row_b == bi, kd, 0.0).astype(BF16)
            stout_ref[bi, h] = cdec_ref[h] * st + _dot_tn(kd_b, v)
        o = o + cross * qdec_ref[:, cs]
        g = srow_ref[:, 1024 + h * RET_VAL_DIM:1024 + (h + 1) * RET_VAL_DIM]
        cat_ref[:, SWA_WIDTH + h * RET_VAL_DIM:SWA_WIDTH + (h + 1) * RET_VAL_DIM] = _gate_out(o, g)


def _mix_sample_call(proj, side, cache_k, cache_v, state_ret, layer, sinks_l, cdec, bias, dmask, qdec, kdec,
                     stacked):
    r = SAMPLE_ROWS
    new_rows = DEC_SEQ * SWA_KV_HEADS
    kv_w = SWA_KV_HEADS * SWA_HEAD_DIM
    k_new = side[:, KV_COL:KV_COL + kv_w].reshape(DEC_BATCH, new_rows, SWA_HEAD_DIM)
    v_new = side[:, KV_COL + kv_w:KV_COL + 2 * kv_w].reshape(DEC_BATCH, new_rows, SWA_HEAD_DIM)
    smem = pl.BlockSpec(memory_space=pltpu.SMEM)
    full = lambda shape: pl.BlockSpec(shape, lambda c: (0,) * len(shape))
    cache_in = pl.BlockSpec((None, SAMPLE_BB, CACHE_ROWS, SWA_HEAD_DIM), lambda c: (layer, c, 0, 0))
    state_dims = (SAMPLE_BB, RET_HEADS, RET_KEY_DIM, RET_VAL_DIM)
    cache_dims = (SAMPLE_BB, CACHE_ROWS, SWA_HEAD_DIM)
    if layer == 0:
        out_block = lambda dims: pl.BlockSpec((DEPTH,) + dims, lambda c: (0, c) + (0,) * (len(dims) - 1))
    else:
        out_block = lambda dims: pl.BlockSpec((None,) + dims, lambda c: (layer, c) + (0,) * (len(dims) - 1))
    new_in = pl.BlockSpec((SAMPLE_BB, new_rows, SWA_HEAD_DIM), lambda c: (c, 0, 0))
    in_specs = [
        smem, smem,
        pl.BlockSpec((r, PROJ_COLS), lambda c: (c, 0)),
        pl.BlockSpec((r, SIDE_COLS), lambda c: (c, 0)),
        full((N_SAMPLE, PROJ_COLS)),
        full((N_SAMPLE, SIDE_COLS)),
        cache_in, cache_in, new_in, new_in,
        pl.BlockSpec((None, SAMPLE_BB, RET_HEADS, RET_KEY_DIM, RET_VAL_DIM), lambda c: (layer, c, 0, 0, 0)),
        pl.BlockSpec((SWA_KV_HEADS, SWA_HEADS, r, CACHE_ROWS + N_SAMPLE), lambda c: (0, 0, c, 0)),
        pl.BlockSpec((RET_HEADS, r, N_SAMPLE), lambda c: (0, c, 0)),
        pl.BlockSpec((r, RET_WIDTH), lambda c: (c, 0)),
        pl.BlockSpec((r, RET_WIDTH), lambda c: (c, 0)),
    ]
    args = [sinks_l, cdec, proj, side, proj, side, cache_k, cache_v, k_new, v_new, state_ret, bias, dmask, qdec,
            kdec]
    aliases = {}
    if stacked is not None:
        for k, buf in enumerate(stacked):
            aliases[len(args)] = 1 + k
            in_specs.append(pl.BlockSpec(memory_space=pl.ANY))
            args.append(buf)
    cache_shape = jax.ShapeDtypeStruct((DEPTH, DEC_BATCH, CACHE_ROWS, SWA_HEAD_DIM), F32)
    cat, *new_stacked = pl.pallas_call(
        functools.partial(_mix_sample_kernel, layer=layer),
        grid=(DEC_BATCH // SAMPLE_BB,),
        in_specs=in_specs,
        out_specs=[
            pl.BlockSpec((r, D_MODEL), lambda c: (c, 0)),
            out_block(state_dims), out_block(cache_dims), out_block(cache_dims),
        ],
        out_shape=[jax.ShapeDtypeStruct((N_SAMPLE, D_MODEL), BF16),
                   jax.ShapeDtypeStruct((DEPTH, DEC_BATCH, RET_HEADS, RET_KEY_DIM, RET_VAL_DIM), F32),
                   cache_shape, cache_shape],
        input_output_aliases=aliases,
        compiler_params=pltpu.CompilerParams(
            dimension_semantics=("arbitrary",), vmem_limit_bytes=VMEM_LIMIT_BYTES),
        name="mix_sample",
    )(*args)
    return cat, tuple(new_stacked)


WO_TN = 512


def _wo_cast_kernel(cat_ref, w_ref, h_ref, out_ref, wbf_ref):
    w_bf = w_ref[...].astype(BF16)
    wbf_ref[...] = w_bf
    out_ref[...] = h_ref[...] + _dot(cat_ref[...], w_bf)


def _wo_cast_call(cat, w_o, layer, h):
    m = h.shape[0]
    return pl.pallas_call(
        _wo_cast_kernel,
        grid=(D_MODEL // WO_TN,),
        in_specs=[
            _resident((m, D_MODEL), lambda j: (0, 0)),
            pl.BlockSpec((None, D_MODEL, WO_TN), lambda j: (layer, 0, j)),
            pl.BlockSpec((m, WO_TN), lambda j: (0, j)),
        ],
        out_specs=[pl.BlockSpec((m, WO_TN), lambda j: (0, j)),
                   pl.BlockSpec((D_MODEL, WO_TN), lambda j: (0, j))],
        out_shape=[jax.ShapeDtypeStruct((m, D_MODEL), F32),
                   jax.ShapeDtypeStruct((D_MODEL, D_MODEL), BF16)],
        compiler_params=pltpu.CompilerParams(
            dimension_semantics=("arbitrary",), vmem_limit_bytes=VMEM_LIMIT_BYTES),
        name="wo_cast",
    )(cat, w_o, h)


def _ffn_step(f, n_f, h_ref, g_ref, weights, gfin_ref, out_ref, xn_ref, final_norm):
    @pl.when(f == 0)
    def _():
        x = h_ref[...]
        xn_ref[...] = (_rms_scale(x) * g_ref[...]).astype(BF16)
        out_ref[...] = x

    w_gate, w_up, w_down = weights()
    xn = xn_ref[...]
    a = _dot(xn, w_gate)
    b = _dot(xn, w_up)
    act = (_silu(a) * b).astype(BF16)
    out_ref[...] += _dot(act, w_down)

    if final_norm:
        @pl.when(f == n_f - 1)
        def _():
            out_ref[...] = _rms_scale(out_ref[...]) * gfin_ref[...]


def _ffn_kernel(h_ref, g_ref, wg_ref, wu_ref, wd_ref, gfin_ref, out_ref, xn_ref, *, final_norm):
    _ffn_step(pl.program_id(1), pl.num_programs(1), h_ref, g_ref,
              lambda: (wg_ref[...], wu_ref[...], wd_ref[...]), gfin_ref, out_ref, xn_ref, final_norm)


def _ffn_call(h, g, w_g_bf, w_u_bf, w_d_bf, g_final, layer, tm, tf):
    m = h.shape[0]
    return pl.pallas_call(
        functools.partial(_ffn_kernel, final_norm=(layer == DEPTH - 1)),
        grid=(m // tm, D_FF // tf),
        in_specs=[
            pl.BlockSpec((tm, D_MODEL), lambda i, f: (i, 0)),
            pl.BlockSpec((None, 1, D_MODEL), lambda i, f: (layer, 0, 0)),
            pl.BlockSpec((D_MODEL, tf), lambda i, f: (0, f)),
            pl.BlockSpec((D_MODEL, tf), lambda i, f: (0, f)),
            pl.BlockSpec((tf, D_MODEL), lambda i, f: (f, 0)),
            pl.BlockSpec((1, D_MODEL), lambda i, f: (0, 0)),
        ],
        out_specs=pl.BlockSpec((tm, D_MODEL), lambda i, f: (i, 0)),
        out_shape=jax.ShapeDtypeStruct((m, D_MODEL), F32),
        scratch_shapes=[pltpu.VMEM((tm, D_MODEL), BF16)],
        compiler_params=pltpu.CompilerParams(
            dimension_semantics=("arbitrary", "arbitrary"), vmem_limit_bytes=VMEM_LIMIT_BYTES),
        name="ffn",
    )(h, g, w_g_bf, w_u_bf, w_d_bf, g_final)


def _ffn_cast_kernel(h_ref, g_ref, wg_ref, wu_ref, wd_ref, gfin_ref, out_ref, wgbf_ref, wubf_ref, wdbf_ref,
                     xn_ref, *, final_norm):
    def weights():
        wgbf_ref[...] = wg_ref[...].astype(BF16)
        wubf_ref[...] = wu_ref[...].astype(BF16)
        wdbf_ref[...] = wd_ref[...].astype(BF16)
        return wgbf_ref[...], wubf_ref[...], wdbf_ref[...]

    _ffn_step(pl.program_id(0), pl.num_programs(0), h_ref, g_ref, weights, gfin_ref, out_ref, xn_ref, final_norm)


def _ffn_cast_call(h, g, w_gate_up, w_down, g_final, layer, tf):
    m = h.shape[0]
    nf = D_FF // tf
    return pl.pallas_call(
        functools.partial(_ffn_cast_kernel, final_norm=(layer == DEPTH - 1)),
        grid=(nf,),
        in_specs=[
            _resident((m, D_MODEL), lambda f: (0, 0)),
            _resident((None, 1, D_MODEL), lambda f: (layer, 0, 0)),
            pl.BlockSpec((None, D_MODEL, tf), lambda f: (layer, 0, f)),
            pl.BlockSpec((None, D_MODEL, tf), lambda f: (layer, 0, f + nf)),
            pl.BlockSpec((None, tf, D_MODEL), lambda f: (layer, f, 0)),
            _resident((1, D_MODEL), lambda f: (0, 0)),
        ],
        out_specs=[
            pl.BlockSpec((m, D_MODEL), lambda f: (0, 0)),
            pl.BlockSpec((D_MODEL, tf), lambda f: (0, f)),
            pl.BlockSpec((D_MODEL, tf), lambda f: (0, f)),
            pl.BlockSpec((tf, D_MODEL), lambda f: (f, 0)),
        ],
        out_shape=[jax.ShapeDtypeStruct((m, D_MODEL), F32),
                   jax.ShapeDtypeStruct((D_MODEL, D_FF), BF16),
                   jax.ShapeDtypeStruct((D_MODEL, D_FF), BF16),
                   jax.ShapeDtypeStruct((D_FF, D_MODEL), BF16)],
        scratch_shapes=[pltpu.VMEM((m, D_MODEL), BF16)],
        compiler_params=pltpu.CompilerParams(
            dimension_semantics=("arbitrary",), vmem_limit_bytes=VMEM_LIMIT_BYTES),
        name="ffn_cast",
    )(h, g, w_gate_up, w_gate_up, w_down, g_final)


def _rope_tables(pos):
    half = RET_KEY_DIM // 2
    inv = 1.0 / (ROPE_BASE ** jnp.linspace(0.0, 1.0, half, dtype=F32))
    ang = pos.astype(F32)[:, None] * inv[None, :]
    return jnp.cos(ang), jnp.sin(ang)


def _decay_tables(c):
    lg = jnp.log(1.0 - jnp.exp2(-5.0 - jnp.arange(RET_HEADS, dtype=F32)))
    idx = jnp.arange(c, dtype=F32)
    diff = idx[:, None] - idx[None, :]
    dmask = jnp.where(diff[None] >= 0, jnp.exp(jnp.maximum(diff, 0.0)[None] * lg[:, None, None]), 0.0)
    q_decay = jnp.exp((idx[:, None] + 1.0) * lg[None, :])
    k_decay = jnp.exp((c - 1.0 - idx)[:, None] * lg[None, :])
    c_decay = jnp.exp(c * lg)
    return dmask, q_decay, k_decay, c_decay


def _per_head_cols(t):
    return jnp.repeat(t, RET_KEY_DIM, axis=1)


def kernel(x_prompt, x_sample, cache_k_win, cache_v_win, state_ret, rel_bias, w_in, sinks, w_o,
           norm_mix, norm_ffn, w_gate_up, w_down, norm_final):
    norm_mix3 = norm_mix.reshape(DEPTH, 1, D_MODEL)
    norm_ffn3 = norm_ffn.reshape(DEPTH, 1, D_MODEL)
    norm_final2 = norm_final.reshape(1, D_MODEL)
    cache_k = cache_k_win.reshape(DEPTH, DEC_BATCH, CACHE_ROWS, SWA_HEAD_DIM)
    cache_v = cache_v_win.reshape(DEPTH, DEC_BATCH, CACHE_ROWS, SWA_HEAD_DIM)

    cos_p, sin_p = _rope_tables(jnp.arange(SEQ, dtype=jnp.int32))
    cos_4, sin_4 = _rope_tables(PAST_LEN + jnp.arange(DEC_SEQ, dtype=jnp.int32))
    cos_s = jnp.tile(cos_4, (DEC_BATCH, 1))
    sin_s = jnp.tile(sin_4, (DEC_BATCH, 1))

    dmask_p, qd_p, kd_p, cdec_p = _decay_tables(RET_CHUNK)
    qdec_p = _per_head_cols(qd_p)
    kdec_p = _per_head_cols(kd_p)
    dmask_4, qd_4, kd_4, cdec_s = _decay_tables(DEC_SEQ)
    eye_b = jnp.eye(DEC_BATCH, dtype=F32)
    dmask_s = jax.vmap(lambda d: jnp.kron(eye_b, d))(dmask_4)
    qdec_s = jnp.tile(_per_head_cols(qd_4), (DEC_BATCH, 1))
    kdec_s = jnp.tile(_per_head_cols(kd_4), (DEC_BATCH, 1))

    qi = np.arange(BLOCK)[:, None]
    kj = np.arange(2 * BLOCK)[None, :]
    delta_p = qi + BLOCK - kj
    in_window = (delta_p >= 0) & (delta_p < WINDOW)
    valid_p = np.stack([in_window & (kj >= BLOCK), in_window]).astype(np.float32)
    rows = np.arange(N_SAMPLE)
    rb, rt = rows // DEC_SEQ, rows % DEC_SEQ
    cache_pos = np.arange(CACHE_ROWS) // SWA_KV_HEADS
    cache_head = np.arange(CACHE_ROWS) % SWA_KV_HEADS
    delta_cache = (WINDOW + rt)[:, None] - cache_pos[None, :]
    delta_new = rt[:, None] - rt[None, :]
    same_b = rb[:, None] == rb[None, :]
    delta_s = np.concatenate([delta_cache, delta_new], axis=1)
    valid_s = np.stack([
        np.concatenate([(delta_cache < WINDOW) & (cache_head == kh)[None, :], same_b & (delta_new >= 0)], axis=1)
        for kh in range(SWA_KV_HEADS)]).astype(np.float32)
    rel_bias_t = rel_bias.T
    bias_p = _expand_bias(rel_bias_t, jnp.asarray(_t5_bucket_np(delta_p)), jnp.asarray(valid_p))
    bias_s = _expand_bias(rel_bias_t, jnp.asarray(_t5_bucket_np(delta_s)), jnp.asarray(valid_s))

    hp = x_prompt.reshape(BATCH * SEQ, D_MODEL)
    hs = x_sample.reshape(N_SAMPLE, D_MODEL)
    tm_proj, tm_ffn, tf = 512, 1024, 512
    kp_new, vp_new, rp_new = [], [], []
    sample_new = None
    kv_w = SWA_KV_HEADS * SWA_HEAD_DIM
    for l in range(DEPTH):
        proj, side, w_in_bf = _proj_cast_call(hs, norm_mix3, l, w_in, cos_s, sin_s)
        cat, sample_new = _mix_sample_call(proj, side, cache_k, cache_v, state_ret, l, sinks[l], cdec_s,
                                           bias_s, dmask_s, qdec_s, kdec_s, sample_new)
        hs, w_o_bf = _wo_cast_call(cat, w_o, l, hs)
        hs, w_g_bf, w_u_bf, w_d_bf = _ffn_cast_call(hs, norm_ffn3, w_gate_up, w_down, norm_final2, l, tf)

        proj, side = _proj_call(hp, norm_mix3, l, w_in_bf, cos_p, sin_p, tm_proj)
        hp, st_p = _mix_prompt_call(proj, side, hp, w_o_bf, sinks[l], cdec_p, bias_p, dmask_p, qdec_p, kdec_p)
        hp = _ffn_call(hp, norm_ffn3, w_g_bf, w_u_bf, w_d_bf, norm_final2, l, tm_ffn, tf)
        kv_tail = side.reshape(BATCH, SEQ, SIDE_COLS)[:, SEQ - WINDOW:, KV_COL:KV_COL + 2 * kv_w]
        kp_new.append(kv_tail[..., :kv_w].reshape(BATCH, WINDOW, SWA_KV_HEADS, SWA_HEAD_DIM))
        vp_new.append(kv_tail[..., kv_w:].reshape(BATCH, WINDOW, SWA_KV_HEADS, SWA_HEAD_DIM))
        rp_new.append(st_p)

    y_prompt = hp.reshape(BATCH, SEQ, D_MODEL)
    y_sample = hs.reshape(DEC_BATCH, DEC_SEQ, D_MODEL)
    rs_new, ks_new, vs_new = sample_new
    return (y_prompt, y_sample,
            jnp.stack(kp_new), jnp.stack(vp_new), jnp.stack(rp_new),
            ks_new.reshape(cache_k_win.shape), vs_new.reshape(cache_v_win.shape), rs_new)
```

```python
import functools
import math

import numpy as np
import jax
import jax.numpy as jnp
from jax import lax
from jax.experimental import pallas as pl
from jax.experimental.pallas import tpu as pltpu

D_MODEL = 2048
BATCH = 4
SEQ = 2048
DEPTH = 4
DEC_BATCH = 32
DEC_SEQ = 4
PAST_LEN = 16384

SWA_WIDTH = 1024
RET_WIDTH = 1024
SWA_HEADS = 8
SWA_KV_HEADS = 2
SWA_GROUP = SWA_HEADS // SWA_KV_HEADS
SWA_HEAD_DIM = 128
WINDOW = 128
BLOCK = WINDOW
RET_HEADS = 4
RET_KEY_DIM = 256
RET_VAL_DIM = 256
RET_CHUNK = 128
ROPE_BASE = 10000.0
N_BUCKETS = 32
MAX_DISTANCE = 128
EPS = 1e-6
D_FF = 5632
IN_COLS = 5632

F32 = jnp.float32
BF16 = jnp.bfloat16

VMEM_LIMIT_BYTES = 60 * 1024 * 1024

PROJ_TN = 512
PROJ_COLS = 3072
SIDE_COLS = 2560
KV_COL = 2048
PROJ_TILES = (
    ("proj", 0, "plain"), ("proj", 512, "plain"),
    ("side", KV_COL, "plain"),
    ("proj", 1024, "rotary"), ("proj", 1536, "rotary"),
    ("side", 0, "rotary_k"), ("side", 512, "rotary_k"),
    ("proj", 2048, "plain"), ("proj", 2560, "plain"),
    ("side", 1024, "plain"), ("side", 1536, "plain"),
)


def _rms_scale(x):
    return x * lax.rsqrt(jnp.mean(x * x, axis=-1, keepdims=True) + EPS)


def _silu(x):
    return x * jax.nn.sigmoid(x)


def _dot(a, b):
    return jnp.dot(a, b, preferred_element_type=F32)


def _dot_nt(a, b):
    return lax.dot_general(a, b, (((1,), (1,)), ((), ())), preferred_element_type=F32)


def _dot_tn(a, b):
    return lax.dot_general(a, b, (((0,), (0,)), ((), ())), preferred_element_type=F32)


def _resident(block_shape, index_map):
    return pl.BlockSpec(block_shape, index_map, pipeline_mode=pl.Buffered(1))


MASKED = -1e30


def _bias_kernel(rbt_ref, idx_ref, valid_ref, out_ref):
    h = pl.program_id(1)
    idx = idx_ref[...]
    acc = jnp.zeros(idx.shape, F32)
    for b in range(N_BUCKETS):
        acc = jnp.where(idx == b, rbt_ref[h, b], acc)
    out_ref[...] = jnp.where(valid_ref[...] > 0.5, acc, MASKED)


def _expand_bias(rel_bias_t, bucket_idx, valid):
    nv, rows, cols = valid.shape
    return pl.pallas_call(
        _bias_kernel,
        grid=(nv, SWA_HEADS),
        in_specs=[pl.BlockSpec(memory_space=pltpu.SMEM),
                  pl.BlockSpec((rows, cols), lambda v, h: (0, 0)),
                  pl.BlockSpec((None, rows, cols), lambda v, h: (v, 0, 0))],
        out_specs=pl.BlockSpec((None, None, rows, cols), lambda v, h: (v, h, 0, 0)),
        out_shape=jax.ShapeDtypeStruct((nv, SWA_HEADS, rows, cols), F32),
        name="bias_expand",
    )(rel_bias_t, bucket_idx, valid)


def _t5_bucket_np(delta):
    n = np.maximum(delta, 0)
    max_exact = N_BUCKETS // 2
    nf = np.maximum(n, 1).astype(np.float64)
    large = max_exact + (np.log(nf / max_exact) / math.log(MAX_DISTANCE / max_exact)
                         * (N_BUCKETS - max_exact)).astype(np.int32)
    large = np.minimum(large, N_BUCKETS - 1)
    return np.where(n < max_exact, n, large).astype(np.int32)


def _proj_epilogue(acc, mode, cos_ref, sin_ref, out_ref, col):
    if mode == "plain":
        out_ref[:, col:col + PROJ_TN] = acc.astype(out_ref.dtype)
        return
    half = RET_KEY_DIM // 2
    cos = cos_ref[...]
    sin = sin_ref[...]
    for c0 in range(0, PROJ_TN, RET_KEY_DIM):
        x1 = acc[:, c0:c0 + half]
        x2 = acc[:, c0 + half:c0 + RET_KEY_DIM]
        o1 = x1 * cos - x2 * sin
        o2 = x1 * sin + x2 * cos
        if mode == "rotary_k":
            o1 = o1 * (RET_KEY_DIM ** -0.5)
            o2 = o2 * (RET_KEY_DIM ** -0.5)
        out_ref[:, col + c0:col + c0 + half] = o1.astype(out_ref.dtype)
        out_ref[:, col + c0 + half:col + c0 + RET_KEY_DIM] = o2.astype(out_ref.dtype)


def _proj_kernel(h_ref, g_ref, w_ref, cos_ref, sin_ref, wgu_ref, wd_ref,
                 proj_ref, side_ref, wgubf_ref, wdbf_ref, xn_ref):
    xn_ref[...] = (_rms_scale(h_ref[...]) * g_ref[...]).astype(BF16)
    outs = {"proj": proj_ref, "side": side_ref}
    n_t = len(PROJ_TILES)
    gu_cols = wgu_ref.shape[1] // n_t
    d_rows = wd_ref.shape[0] // n_t
    for t, (dst, col, mode) in enumerate(PROJ_TILES):
        acc = _dot(xn_ref[...], w_ref[:, t * PROJ_TN:(t + 1) * PROJ_TN])
        _proj_epilogue(acc, mode, cos_ref, sin_ref, outs[dst], col)
        gc = slice(t * gu_cols, (t + 1) * gu_cols)
        dr = slice(t * d_rows, (t + 1) * d_rows)
        wgubf_ref[:, gc] = wgu_ref[:, gc].astype(BF16)
        wdbf_ref[dr] = wd_ref[dr].astype(BF16)


def _proj_call(h, g, layer, w_in_bf, cos, sin, w_gate_up, w_down, tm):
    m = h.shape[0]
    n_steps = m // tm
    cos_blocks = cos.shape[0] // tm
    gu_rows = D_MODEL // n_steps
    d_rows = D_FF // n_steps
    assert D_MODEL % n_steps == 0 and D_FF % n_steps == 0 and d_rows % (16 * len(PROJ_TILES)) == 0
    return pl.pallas_call(
        _proj_kernel,
        grid=(n_steps,),
        in_specs=[
            pl.BlockSpec((tm, D_MODEL), lambda i: (i, 0)),
            _resident((None, 1, D_MODEL), lambda i: (layer, 0, 0)),
            _resident((D_MODEL, IN_COLS), lambda i: (0, 0)),
            pl.BlockSpec((tm, RET_KEY_DIM // 2), lambda i: (i % cos_blocks, 0)),
            pl.BlockSpec((tm, RET_KEY_DIM // 2), lambda i: (i % cos_blocks, 0)),
            pl.BlockSpec((None, gu_rows, 2 * D_FF), lambda i: (layer, i, 0)),
            pl.BlockSpec((None, d_rows, D_MODEL), lambda i: (layer, i, 0)),
        ],
        out_specs=[
            pl.BlockSpec((tm, PROJ_COLS), lambda i: (i, 0)),
            pl.BlockSpec((tm, SIDE_COLS), lambda i: (i, 0)),
            pl.BlockSpec((gu_rows, 2 * D_FF), lambda i: (i, 0)),
            pl.BlockSpec((d_rows, D_MODEL), lambda i: (i, 0)),
        ],
        out_shape=[jax.ShapeDtypeStruct((m, PROJ_COLS), BF16),
                   jax.ShapeDtypeStruct((m, SIDE_COLS), F32),
                   jax.ShapeDtypeStruct((D_MODEL, 2 * D_FF), BF16),
                   jax.ShapeDtypeStruct((D_FF, D_MODEL), BF16)],
        scratch_shapes=[pltpu.VMEM((tm, D_MODEL), BF16)],
        compiler_params=pltpu.CompilerParams(
            dimension_semantics=("arbitrary",), vmem_limit_bytes=VMEM_LIMIT_BYTES),
        name="proj",
    )(h, g, w_in_bf, cos, sin, w_gate_up, w_down)


def _proj_cast_kernel(h_ref, g_ref, w_ref, cos_ref, sin_ref, proj_ref, side_ref, wbf_ref, xn_ref):
    j = pl.program_id(0)

    @pl.when(j == 0)
    def _():
        xn_ref[...] = (_rms_scale(h_ref[...]) * g_ref[...]).astype(BF16)

    w_bf = w_ref[...].astype(BF16)
    wbf_ref[...] = w_bf
    acc = _dot(xn_ref[...], w_bf)
    outs = {"proj": proj_ref, "side": side_ref}
    for t, (dst, col, mode) in enumerate(PROJ_TILES):
        @pl.when(j == t)
        def _():
            _proj_epilogue(acc, mode, cos_ref, sin_ref, outs[dst], col)


def _proj_cast_call(h, g, layer, w_in, cos, sin):
    m = h.shape[0]
    full = lambda shape: _resident(shape, lambda j: (0,) * len(shape))
    return pl.pallas_call(
        _proj_cast_kernel,
        grid=(len(PROJ_TILES),),
        in_specs=[
            full((m, D_MODEL)),
            _resident((None, 1, D_MODEL), lambda j: (layer, 0, 0)),
            pl.BlockSpec((None, D_MODEL, PROJ_TN), lambda j: (layer, 0, j)),
            full((m, RET_KEY_DIM // 2)),
            full((m, RET_KEY_DIM // 2)),
        ],
        out_specs=[
            pl.BlockSpec((m, PROJ_COLS), lambda j: (0, 0)),
            pl.BlockSpec((m, SIDE_COLS), lambda j: (0, 0)),
            pl.BlockSpec((D_MODEL, PROJ_TN), lambda j: (0, j)),
        ],
        out_shape=[jax.ShapeDtypeStruct((m, PROJ_COLS), BF16),
                   jax.ShapeDtypeStruct((m, SIDE_COLS), F32),
                   jax.ShapeDtypeStruct((D_MODEL, IN_COLS), BF16)],
        scratch_shapes=[pltpu.VMEM((m, D_MODEL), BF16)],
        compiler_params=pltpu.CompilerParams(
            dimension_semantics=("arbitrary",), vmem_limit_bytes=VMEM_LIMIT_BYTES),
        name="proj_cast",
    )(h, g, w_in, cos, sin)


def _softmax_sink(s, sink):
    m = jnp.maximum(jnp.max(s, axis=-1, keepdims=True), sink)
    p = jnp.exp(s - m)
    return p, 1.0 / (jnp.sum(p, axis=-1, keepdims=True) + jnp.exp(sink - m))


def _gate_out(o, g):
    return (_rms_scale(o) * _silu(g)).astype(BF16)


WO_BATCHES = 2


def _mix_prompt_kernel(sink_ref, cdec_ref, qs_ref, qr_ref, vr_ref, kr_ref, g_ref, kvc_ref,
                       bias_ref, dmask_ref, qdec_ref, kdec_ref, wo_ref, h_ref, out_ref, st_ref, kvp_ref, *cat_refs):
    n = pl.program_id(0)

    @pl.when(n == 0)
    def _():
        st_ref[...] = jnp.zeros(st_ref.shape, F32)
        kvp_ref[...] = jnp.zeros(kvp_ref.shape, BF16)

    scale = SWA_HEAD_DIM ** -0.5
    v_off = SWA_KV_HEADS * SWA_HEAD_DIM

    def cat_of(b):
        return cat_refs[b // WO_BATCHES].at[b % WO_BATCHES]

    def wo_rows(b0):
        bs = slice(b0, b0 + WO_BATCHES)
        cat = cat_refs[b0 // WO_BATCHES][...].reshape(WO_BATCHES * BLOCK, D_MODEL)
        out_ref[bs] = h_ref[bs] + _dot(cat, wo_ref[...]).reshape(WO_BATCHES, BLOCK, D_MODEL)

    def attention(b, kh):
        c0 = kh * SWA_HEAD_DIM
        k_cur = kvc_ref[b, :, c0:c0 + SWA_HEAD_DIM].astype(BF16)
        v_cur = kvc_ref[b, :, v_off + c0:v_off + c0 + SWA_HEAD_DIM].astype(BF16)
        k_cat = jnp.concatenate([kvp_ref[b, :, c0:c0 + SWA_HEAD_DIM], k_cur], axis=0)
        v_cat = jnp.concatenate([kvp_ref[b, :, v_off + c0:v_off + c0 + SWA_HEAD_DIM], v_cur], axis=0)
        kvp_ref[b, :, c0:c0 + SWA_HEAD_DIM] = k_cur
        kvp_ref[b, :, v_off + c0:v_off + c0 + SWA_HEAD_DIM] = v_cur
        for gq in range(SWA_GROUP):
            h = kh * SWA_GROUP + gq
            q = qs_ref[b, :, h * SWA_HEAD_DIM:(h + 1) * SWA_HEAD_DIM]
            s = _dot_nt(q, k_cat) * scale + bias_ref[h]
            p, inv = _softmax_sink(s, sink_ref[h])
            o = _dot(p.astype(BF16), v_cat) * inv
            cat_of(b)[:, h * SWA_HEAD_DIM:(h + 1) * SWA_HEAD_DIM] = o.astype(BF16)

    def retention(b, h):
        cs = slice(h * RET_KEY_DIM, (h + 1) * RET_KEY_DIM)
        q = qr_ref[b, :, cs]
        k32 = kr_ref[b, :, cs]
        v = vr_ref[b, :, cs]
        s = _dot_nt(q, k32.astype(BF16)) * dmask_ref[h]
        o = _dot(s.astype(BF16), v)
        st = st_ref[b, h]
        o = o + _dot(q, st.astype(BF16)) * qdec_ref[:, cs]
        kd = (k32 * kdec_ref[:, cs]).astype(BF16)
        st_ref[b, h] = cdec_ref[h] * st + _dot_tn(kd, v)
        cat_of(b)[:, SWA_WIDTH + h * RET_VAL_DIM:SWA_WIDTH + (h + 1) * RET_VAL_DIM] = (
            _gate_out(o, g_ref[b, :, cs]))

    for b in range(BATCH):
        for kh in range(SWA_KV_HEADS):
            attention(b, kh)
        for h in range(RET_HEADS):
            retention(b, h)
        if (b + 1) % WO_BATCHES == 0:
            wo_rows(b + 1 - WO_BATCHES)


def _mix_prompt_call(proj, side, h, w_o_bf, sinks_l, cdec, bias, dmask, qdec, kdec):
    nblk = SEQ // BLOCK
    proj3 = proj.reshape(BATCH, SEQ, PROJ_COLS)
    side3 = side.reshape(BATCH, SEQ, SIDE_COLS)
    h3 = h.reshape(BATCH, SEQ, D_MODEL)
    smem = pl.BlockSpec(memory_space=pltpu.SMEM)
    full = lambda shape: _resident(shape, lambda n: (0,) * len(shape))
    cur = lambda n: n
    prv = lambda n: n
    out, st = pl.pallas_call(
        _mix_prompt_kernel,
        grid=(nblk,),
        in_specs=[
            smem, smem,
            pl.BlockSpec((BATCH, BLOCK, 1024), lambda n: (0, cur(n), 0)),
            pl.BlockSpec((BATCH, BLOCK, 1024), lambda n: (0, cur(n), 1)),
            pl.BlockSpec((BATCH, BLOCK, 1024), lambda n: (0, cur(n), 2)),
            pl.BlockSpec((BATCH, BLOCK, 1024), lambda n: (0, cur(n), 0)),
            pl.BlockSpec((BATCH, BLOCK, 1024), lambda n: (0, cur(n), 1)),
            pl.BlockSpec((BATCH, BLOCK, 512), lambda n: (0, cur(n), 4)),
            pl.BlockSpec((None, SWA_HEADS, BLOCK, 2 * BLOCK), lambda n: (jnp.minimum(n, 1), 0, 0, 0)),
            full((RET_HEADS, RET_CHUNK, RET_CHUNK)),
            full((RET_CHUNK, RET_WIDTH)),
            full((RET_CHUNK, RET_WIDTH)),
            full((D_MODEL, D_MODEL)),
            pl.BlockSpec((BATCH, BLOCK, D_MODEL), lambda n: (0, prv(n), 0)),
        ],
        out_specs=[
            pl.BlockSpec((BATCH, BLOCK, D_MODEL), lambda n: (0, prv(n), 0)),
            _resident((BATCH, RET_HEADS, RET_KEY_DIM, RET_VAL_DIM), lambda n: (0, 0, 0, 0)),
        ],
        out_shape=[jax.ShapeDtypeStruct((BATCH, SEQ, D_MODEL), F32),
                   jax.ShapeDtypeStruct((BATCH, RET_HEADS, RET_KEY_DIM, RET_VAL_DIM), F32)],
        scratch_shapes=[pltpu.VMEM((BATCH, BLOCK, 2 * SWA_KV_HEADS * SWA_HEAD_DIM), BF16)]
        + [pltpu.VMEM((WO_BATCHES, BLOCK, D_MODEL), BF16)] * (BATCH // WO_BATCHES),
        compiler_params=pltpu.CompilerParams(
            dimension_semantics=("arbitrary",), vmem_limit_bytes=VMEM_LIMIT_BYTES),
        name="mix_prompt",
    )(sinks_l, cdec, proj3, proj3, proj3, side3, side3, side3, bias, dmask, qdec, kdec, w_o_bf, h3)
    return out.reshape(BATCH * SEQ, D_MODEL), st


SAMPLE_BB = 4
SAMPLE_ROWS = SAMPLE_BB * DEC_SEQ
N_SAMPLE = DEC_BATCH * DEC_SEQ
CACHE_ROWS = WINDOW * SWA_KV_HEADS


def _mix_sample_kernel(sink_ref, cdec_ref, prow_ref, srow_ref, pall_ref, sall_ref, ck_ref, cv_ref,
                       knew_ref, vnew_ref, st_ref, bias_ref, dmask_ref, qdec_ref, kdec_ref, *rest, layer):
    cat_ref, stout_ref, ckout_ref, cvout_ref = rest[-4:]
    if layer == 0:
        for ref in (stout_ref, ckout_ref, cvout_ref):
            ref[1:] = jnp.zeros((DEPTH - 1,) + ref.shape[1:], F32)
        stout_ref, ckout_ref, cvout_ref = stout_ref.at[0], ckout_ref.at[0], cvout_ref.at[0]
    r = SAMPLE_ROWS
    row_b = lax.broadcasted_iota(jnp.int32, (r, 1), 0) // DEC_SEQ
    row_b4 = lax.broadcasted_iota(jnp.int32, (SWA_GROUP * r, 1), 0) % r // DEC_SEQ
    scale = SWA_HEAD_DIM ** -0.5
    v_off = SWA_KV_HEADS * SWA_HEAD_DIM
    new_rows = DEC_SEQ * SWA_KV_HEADS

    for bi in range(SAMPLE_BB):
        ckout_ref[bi, :CACHE_ROWS - new_rows] = ck_ref[bi, new_rows:]
        ckout_ref[bi, CACHE_ROWS - new_rows:] = knew_ref[bi]
        cvout_ref[bi, :CACHE_ROWS - new_rows] = cv_ref[bi, new_rows:]
        cvout_ref[bi, CACHE_ROWS - new_rows:] = vnew_ref[bi]

    for kh in range(SWA_KV_HEADS):
        c0 = kh * SWA_HEAD_DIM
        q4 = jnp.concatenate(
            [prow_ref[:, (kh * SWA_GROUP + gq) * SWA_HEAD_DIM:(kh * SWA_GROUP + gq + 1) * SWA_HEAD_DIM]
             for gq in range(SWA_GROUP)], axis=0)
        k_new = sall_ref[:, KV_COL + c0:KV_COL + c0 + SWA_HEAD_DIM].astype(BF16)
        v_new = sall_ref[:, KV_COL + v_off + c0:KV_COL + v_off + c0 + SWA_HEAD_DIM].astype(BF16)
        s_cache = jnp.zeros((SWA_GROUP * r, CACHE_ROWS), F32)
        for bi in range(SAMPLE_BB):
            s_cache = jnp.where(row_b4 == bi, _dot_nt(q4, ck_ref[bi].astype(BF16)), s_cache)
        s_new = _dot_nt(q4, k_new)
        s4 = jnp.concatenate([s_cache, s_new], axis=1) * scale
        p_parts, inv_parts = [], []
        for gq in range(SWA_GROUP):
            h = kh * SWA_GROUP + gq
            s = s4[gq * r:(gq + 1) * r] + bias_ref[kh, h]
            p, inv = _softmax_sink(s, sink_ref[h])
            p_parts.append(p)
            inv_parts.append(inv)
        p4 = jnp.concatenate(p_parts, axis=0)
        inv4 = jnp.concatenate(inv_parts, axis=0)
        p_cache = p4[:, :CACHE_ROWS]
        o4 = _dot(p4[:, CACHE_ROWS:].astype(BF16), v_new)
        for bi in range(SAMPLE_BB):
            o4 = o4 + _dot(jnp.where(row_b4 == bi, p_cache, 0.0).astype(BF16), cv_ref[bi].astype(BF16))
        o4 = o4 * inv4
        for gq in range(SWA_GROUP):
            h = kh * SWA_GROUP + gq
            cat_ref[:, h * SWA_HEAD_DIM:(h + 1) * SWA_HEAD_DIM] = o4[gq * r:(gq + 1) * r].astype(BF16)

    for h in range(RET_HEADS):
        cs = slice(h * RET_KEY_DIM, (h + 1) * RET_KEY_DIM)
        q = prow_ref[:, 1024 + h * RET_KEY_DIM:1024 + (h + 1) * RET_KEY_DIM]
        k_all = sall_ref[:, cs].astype(BF16)
        v_all = pall_ref[:, 2048 + h * RET_VAL_DIM:2048 + (h + 1) * RET_VAL_DIM]
        s = _dot_nt(q, k_all) * dmask_ref[h]
        o = _dot(s.astype(BF16), v_all)
        k32 = srow_ref[:, cs]
        v = prow_ref[:, 2048 + h * RET_VAL_DIM:2048 + (h + 1) * RET_VAL_DIM]
        kd = k32 * kdec_ref[:, cs]
        cross = jnp.zeros((r, RET_VAL_DIM), F32)
        for bi in range(SAMPLE_BB):
            st = st_ref[bi, h]
            cross = jnp.where(row_b == bi, _dot(q, st.astype(BF16)), cross)
            kd_b = jnp.where(row_b == bi, kd, 0.0).astype(BF16)
            stout_ref[bi, h] = cdec_ref[h] * st + _dot_tn(kd_b, v)
        o = o + cross * qdec_ref[:, cs]
        g = srow_ref[:, 1024 + h * RET_VAL_DIM:1024 + (h + 1) * RET_VAL_DIM]
        cat_ref[:, SWA_WIDTH + h * RET_VAL_DIM:SWA_WIDTH + (h + 1) * RET_VAL_DIM] = _gate_out(o, g)


def _mix_sample_call(proj, side, cache_k, cache_v, state_ret, layer, sinks_l, cdec, bias, dmask, qdec, kdec,
                     stacked):
    r = SAMPLE_ROWS
    new_rows = DEC_SEQ * SWA_KV_HEADS
    kv_w = SWA_KV_HEADS * SWA_HEAD_DIM
    k_new = side[:, KV_COL:KV_COL + kv_w].reshape(DEC_BATCH, new_rows, SWA_HEAD_DIM)
    v_new = side[:, KV_COL + kv_w:KV_COL + 2 * kv_w].reshape(DEC_BATCH, new_rows, SWA_HEAD_DIM)
    smem = pl.BlockSpec(memory_space=pltpu.SMEM)
    full = lambda shape: pl.BlockSpec(shape, lambda c: (0,) * len(shape))
    cache_in = pl.BlockSpec((None, SAMPLE_BB, CACHE_ROWS, SWA_HEAD_DIM), lambda c: (layer, c, 0, 0))
    state_dims = (SAMPLE_BB, RET_HEADS, RET_KEY_DIM, RET_VAL_DIM)
    cache_dims = (SAMPLE_BB, CACHE_ROWS, SWA_HEAD_DIM)
    if layer == 0:
        out_block = lambda dims: pl.BlockSpec((DEPTH,) + dims, lambda c: (0, c) + (0,) * (len(dims) - 1))
    else:
        out_block = lambda dims: pl.BlockSpec((None,) + dims, lambda c: (layer, c) + (0,) * (len(dims) - 1))
    new_in = pl.BlockSpec((SAMPLE_BB, new_rows, SWA_HEAD_DIM), lambda c: (c, 0, 0))
    in_specs = [
        smem, smem,
        pl.BlockSpec((r, PROJ_COLS), lambda c: (c, 0)),
        pl.BlockSpec((r, SIDE_COLS), lambda c: (c, 0)),
        full((N_SAMPLE, PROJ_COLS)),
        full((N_SAMPLE, SIDE_COLS)),
        cache_in, cache_in, new_in, new_in,
        pl.BlockSpec((None, SAMPLE_BB, RET_HEADS, RET_KEY_DIM, RET_VAL_DIM), lambda c: (layer, c, 0, 0, 0)),
        pl.BlockSpec((SWA_KV_HEADS, SWA_HEADS, r, CACHE_ROWS + N_SAMPLE), lambda c: (0, 0, c, 0)),
        pl.BlockSpec((RET_HEADS, r, N_SAMPLE), lambda c: (0, c, 0)),
        pl.BlockSpec((r, RET_WIDTH), lambda c: (c, 0)),
        pl.BlockSpec((r, RET_WIDTH), lambda c: (c, 0)),
    ]
    args = [sinks_l, cdec, proj, side, proj, side, cache_k, cache_v, k_new, v_new, state_ret, bias, dmask, qdec,
            kdec]
    aliases = {}
    if stacked is not None:
        for k, buf in enumerate(stacked):
            aliases[len(args)] = 1 + k
            in_specs.append(pl.BlockSpec(memory_space=pl.ANY))
            args.append(buf)
    cache_shape = jax.ShapeDtypeStruct((DEPTH, DEC_BATCH, CACHE_ROWS, SWA_HEAD_DIM), F32)
    cat, *new_stacked = pl.pallas_call(
        functools.partial(_mix_sample_kernel, layer=layer),
        grid=(DEC_BATCH // SAMPLE_BB,),
        in_specs=in_specs,
        out_specs=[
            pl.BlockSpec((r, D_MODEL), lambda c: (c, 0)),
            out_block(state_dims), out_block(cache_dims), out_block(cache_dims),
        ],
        out_shape=[jax.ShapeDtypeStruct((N_SAMPLE, D_MODEL), BF16),
                   jax.ShapeDtypeStruct((DEPTH, DEC_BATCH, RET_HEADS, RET_KEY_DIM, RET_VAL_DIM), F32),
                   cache_shape, cache_shape],
        input_output_aliases=aliases,
        compiler_params=pltpu.CompilerParams(
            dimension_semantics=("arbitrary",), vmem_limit_bytes=VMEM_LIMIT_BYTES),
        name="mix_sample",
    )(*args)
    return cat, tuple(new_stacked)


WO_TN = 512


def _wo_cast_kernel(cat_ref, w_ref, h_ref, out_ref, wbf_ref):
    w_bf = w_ref[...].astype(BF16)
    wbf_ref[...] = w_bf
    out_ref[...] = h_ref[...] + _dot(cat_ref[...], w_bf)


def _wo_cast_call(cat, w_o, layer, h):
    m = h.shape[0]
    return pl.pallas_call(
        _wo_cast_kernel,
        grid=(D_MODEL // WO_TN,),
        in_specs=[
            _resident((m, D_MODEL), lambda j: (0, 0)),
            pl.BlockSpec((None, D_MODEL, WO_TN), lambda j: (layer, 0, j)),
            pl.BlockSpec((m, WO_TN), lambda j: (0, j)),
        ],
        out_specs=[pl.BlockSpec((m, WO_TN), lambda j: (0, j)),
                   pl.BlockSpec((D_MODEL, WO_TN), lambda j: (0, j))],
        out_shape=[jax.ShapeDtypeStruct((m, D_MODEL), F32),
                   jax.ShapeDtypeStruct((D_MODEL, D_MODEL), BF16)],
        compiler_params=pltpu.CompilerParams(
            dimension_semantics=("arbitrary",), vmem_limit_bytes=VMEM_LIMIT_BYTES),
        name="wo_cast",
    )(cat, w_o, h)


def _ffn_kernel(h_ref, g_ref, wg_ref, wu_ref, wd_ref, gfin_ref, out_ref, xn_ref, *, final_norm):
    f = pl.program_id(1)

    @pl.when(f == 0)
    def _():
        x = h_ref[...]
        xn_ref[...] = (_rms_scale(x) * g_ref[...]).astype(BF16)
        out_ref[...] = x

    xn = xn_ref[...]
    a = _dot(xn, wg_ref[...])
    b = _dot(xn, wu_ref[...])
    act = (_silu(a) * b).astype(BF16)
    out_ref[...] += _dot(act, wd_ref[...])

    if final_norm:
        @pl.when(f == pl.num_programs(1) - 1)
        def _():
            out_ref[...] = _rms_scale(out_ref[...]) * gfin_ref[...]


def _ffn_call(h, g, w_gu_bf, w_d_bf, g_final, layer, tm, tf):
    m = h.shape[0]
    nf = D_FF // tf
    return pl.pallas_call(
        functools.partial(_ffn_kernel, final_norm=(layer == DEPTH - 1)),
        grid=(m // tm, nf),
        in_specs=[
            pl.BlockSpec((tm, D_MODEL), lambda i, f: (i, 0)),
            pl.BlockSpec((None, 1, D_MODEL), lambda i, f: (layer, 0, 0)),
            pl.BlockSpec((D_MODEL, tf), lambda i, f: (0, f)),
            pl.BlockSpec((D_MODEL, tf), lambda i, f: (0, f + nf)),
            pl.BlockSpec((tf, D_MODEL), lambda i, f: (f, 0)),
            pl.BlockSpec((1, D_MODEL), lambda i, f: (0, 0)),
        ],
        out_specs=pl.BlockSpec((tm, D_MODEL), lambda i, f: (i, 0)),
        out_shape=jax.ShapeDtypeStruct((m, D_MODEL), F32),
        scratch_shapes=[pltpu.VMEM((tm, D_MODEL), BF16)],
        compiler_params=pltpu.CompilerParams(
            dimension_semantics=("arbitrary", "arbitrary"), vmem_limit_bytes=VMEM_LIMIT_BYTES),
        name="ffn",
    )(h, g, w_gu_bf, w_gu_bf, w_d_bf, g_final)


def _rope_tables(pos):
    half = RET_KEY_DIM // 2
    inv = 1.0 / (ROPE_BASE ** jnp.linspace(0.0, 1.0, half, dtype=F32))
    ang = pos.astype(F32)[:, None] * inv[None, :]
    return jnp.cos(ang), jnp.sin(ang)


def _decay_tables(c):
    lg = jnp.log(1.0 - jnp.exp2(-5.0 - jnp.arange(RET_HEADS, dtype=F32)))
    idx = jnp.arange(c, dtype=F32)
    diff = idx[:, None] - idx[None, :]
    dmask = jnp.where(diff[None] >= 0, jnp.exp(jnp.maximum(diff, 0.0)[None] * lg[:, None, None]), 0.0)
    q_decay = jnp.exp((idx[:, None] + 1.0) * lg[None, :])
    k_decay = jnp.exp((c - 1.0 - idx)[:, None] * lg[None, :])
    c_decay = jnp.exp(c * lg)
    return dmask, q_decay, k_decay, c_decay


def _per_head_cols(t):
    return jnp.repeat(t, RET_KEY_DIM, axis=1)


def kernel(x_prompt, x_sample, cache_k_win, cache_v_win, state_ret, rel_bias, w_in, sinks, w_o,
           norm_mix, norm_ffn, w_gate_up, w_down, norm_final):
    norm_mix3 = norm_mix.reshape(DEPTH, 1, D_MODEL)
    norm_ffn3 = norm_ffn.reshape(DEPTH, 1, D_MODEL)
    norm_final2 = norm_final.reshape(1, D_MODEL)
    cache_k = cache_k_win.reshape(DEPTH, DEC_BATCH, CACHE_ROWS, SWA_HEAD_DIM)
    cache_v = cache_v_win.reshape(DEPTH, DEC_BATCH, CACHE_ROWS, SWA_HEAD_DIM)

    cos_p, sin_p = _rope_tables(jnp.arange(SEQ, dtype=jnp.int32))
    cos_4, sin_4 = _rope_tables(PAST_LEN + jnp.arange(DEC_SEQ, dtype=jnp.int32))
    cos_s = jnp.tile(cos_4, (DEC_BATCH, 1))
    sin_s = jnp.tile(sin_4, (DEC_BATCH, 1))

    dmask_p, qd_p, kd_p, cdec_p = _decay_tables(RET_CHUNK)
    qdec_p = _per_head_cols(qd_p)
    kdec_p = _per_head_cols(kd_p)
    dmask_4, qd_4, kd_4, cdec_s = _decay_tables(DEC_SEQ)
    eye_b = jnp.eye(DEC_BATCH, dtype=F32)
    dmask_s = jax.vmap(lambda d: jnp.kron(eye_b, d))(dmask_4)
    qdec_s = jnp.tile(_per_head_cols(qd_4), (DEC_BATCH, 1))
    kdec_s = jnp.tile(_per_head_cols(kd_4), (DEC_BATCH, 1))

    qi = np.arange(BLOCK)[:, None]
    kj = np.arange(2 * BLOCK)[None, :]
    delta_p = qi + BLOCK - kj
    in_window = (delta_p >= 0) & (delta_p < WINDOW)
    valid_p = np.stack([in_window & (kj >= BLOCK), in_window]).astype(np.float32)
    rows = np.arange(N_SAMPLE)
    rb, rt = rows // DEC_SEQ, rows % DEC_SEQ
    cache_pos = np.arange(CACHE_ROWS) // SWA_KV_HEADS
    cache_head = np.arange(CACHE_ROWS) % SWA_KV_HEADS
    delta_cache = (WINDOW + rt)[:, None] - cache_pos[None, :]
    delta_new = rt[:, None] - rt[None, :]
    same_b = rb[:, None] == rb[None, :]
    delta_s = np.concatenate([delta_cache, delta_new], axis=1)
    valid_s = np.stack([
        np.concatenate([(delta_cache < WINDOW) & (cache_head == kh)[None, :], same_b & (delta_new >= 0)], axis=1)
        for kh in range(SWA_KV_HEADS)]).astype(np.float32)
    rel_bias_t = rel_bias.T
    bias_p = _expand_bias(rel_bias_t, jnp.asarray(_t5_bucket_np(delta_p)), jnp.asarray(valid_p))
    bias_s = _expand_bias(rel_bias_t, jnp.asarray(_t5_bucket_np(delta_s)), jnp.asarray(valid_s))

    hp = x_prompt.reshape(BATCH * SEQ, D_MODEL)
    hs = x_sample.reshape(N_SAMPLE, D_MODEL)
    tm_proj, tm_ffn, tf = 256, 1024, 512
    kp_new, vp_new, rp_new = [], [], []
    sample_new = None
    kv_w = SWA_KV_HEADS * SWA_HEAD_DIM
    for l in range(DEPTH):
        proj_s, side_s, w_in_bf = _proj_cast_call(hs, norm_mix3, l, w_in, cos_s, sin_s)
        proj, side, w_gu_bf, w_d_bf = _proj_call(hp, norm_mix3, l, w_in_bf, cos_p, sin_p, w_gate_up, w_down,
                                                 tm_proj)

        cat, sample_new = _mix_sample_call(proj_s, side_s, cache_k, cache_v, state_ret, l, sinks[l], cdec_s,
                                           bias_s, dmask_s, qdec_s, kdec_s, sample_new)
        hs, w_o_bf = _wo_cast_call(cat, w_o, l, hs)
        hs = _ffn_call(hs, norm_ffn3, w_gu_bf, w_d_bf, norm_final2, l, N_SAMPLE, tf)

        hp, st_p = _mix_prompt_call(proj, side, hp, w_o_bf, sinks[l], cdec_p, bias_p, dmask_p, qdec_p, kdec_p)
        hp = _ffn_call(hp, norm_ffn3, w_gu_bf, w_d_bf, norm_final2, l, tm_ffn, tf)
        kv_tail = side.reshape(BATCH, SEQ, SIDE_COLS)[:, SEQ - WINDOW:, KV_COL:KV_COL + 2 * kv_w]
        kp_new.append(kv_tail[..., :kv_w].reshape(BATCH, WINDOW, SWA_KV_HEADS, SWA_HEAD_DIM))
        vp_new.append(kv_tail[..., kv_w:].reshape(BATCH, WINDOW, SWA_KV_HEADS, SWA_HEAD_DIM))
        rp_new.append(st_p)

    y_prompt = hp.reshape(BATCH, SEQ, D_MODEL)
    y_sample = hs.reshape(DEC_BATCH, DEC_SEQ, D_MODEL)
    rs_new, ks_new, vs_new = sample_new
    return (y_prompt, y_sample,
            jnp.stack(kp_new), jnp.stack(vp_new), jnp.stack(rp_new),
            ks_new.reshape(cache_k_win.shape), vs_new.reshape(cache_v_win.shape), rs_new)
```

```python
import functools
import math

import numpy as np
import jax
import jax.numpy as jnp
from jax import lax
from jax.experimental import pallas as pl
from jax.experimental.pallas import tpu as pltpu

D_MODEL = 2048
BATCH = 4
SEQ = 2048
DEPTH = 4
DEC_BATCH = 32
DEC_SEQ = 4
PAST_LEN = 16384

SWA_WIDTH = 1024
RET_WIDTH = 1024
SWA_HEADS = 8
SWA_KV_HEADS = 2
SWA_GROUP = SWA_HEADS // SWA_KV_HEADS
SWA_HEAD_DIM = 128
WINDOW = 128
BLOCK = WINDOW
RET_HEADS = 4
RET_KEY_DIM = 256
RET_VAL_DIM = 256
RET_CHUNK = 128
ROPE_BASE = 10000.0
N_BUCKETS = 32
MAX_DISTANCE = 128
EPS = 1e-6
D_FF = 5632
IN_COLS = 5632

F32 = jnp.float32
BF16 = jnp.bfloat16

VMEM_LIMIT_BYTES = 60 * 1024 * 1024

FFN_TF = 512
FFN_TILES = D_FF // FFN_TF
PROJ_TN = 512
PROJ_COLS = 3072
SIDE_COLS = 2560
KV_COL = 2048
PROJ_TILES = (
    ("proj", 0, "plain"), ("proj", 512, "plain"),
    ("side", KV_COL, "plain"),
    ("proj", 1024, "rotary"), ("proj", 1536, "rotary"),
    ("side", 0, "rotary_k"), ("side", 512, "rotary_k"),
    ("proj", 2048, "plain"), ("proj", 2560, "plain"),
    ("side", 1024, "plain"), ("side", 1536, "plain"),
)


def _rms_scale(x):
    return x * lax.rsqrt(jnp.mean(x * x, axis=-1, keepdims=True) + EPS)


def _silu(x):
    return x * jax.nn.sigmoid(x)


def _dot(a, b):
    return jnp.dot(a, b, preferred_element_type=F32)


def _dot_nt(a, b):
    return lax.dot_general(a, b, (((1,), (1,)), ((), ())), preferred_element_type=F32)


def _dot_tn(a, b):
    return lax.dot_general(a, b, (((0,), (0,)), ((), ())), preferred_element_type=F32)


def _resident(block_shape, index_map):
    return pl.BlockSpec(block_shape, index_map, pipeline_mode=pl.Buffered(1))


MASKED = -1e30


def _bias_kernel(rbt_ref, idx_ref, valid_ref, out_ref):
    h = pl.program_id(1)
    idx = idx_ref[...]
    acc = jnp.zeros(idx.shape, F32)
    for b in range(N_BUCKETS):
        acc = jnp.where(idx == b, rbt_ref[h, b], acc)
    out_ref[...] = jnp.where(valid_ref[...] > 0.5, acc, MASKED)


def _expand_bias(rel_bias_t, bucket_idx, valid):
    nv, rows, cols = valid.shape
    return pl.pallas_call(
        _bias_kernel,
        grid=(nv, SWA_HEADS),
        in_specs=[pl.BlockSpec(memory_space=pltpu.SMEM),
                  pl.BlockSpec((rows, cols), lambda v, h: (0, 0)),
                  pl.BlockSpec((None, rows, cols), lambda v, h: (v, 0, 0))],
        out_specs=pl.BlockSpec((None, None, rows, cols), lambda v, h: (v, h, 0, 0)),
        out_shape=jax.ShapeDtypeStruct((nv, SWA_HEADS, rows, cols), F32),
        name="bias_expand",
    )(rel_bias_t, bucket_idx, valid)


def _t5_bucket_np(delta):
    n = np.maximum(delta, 0)
    max_exact = N_BUCKETS // 2
    nf = np.maximum(n, 1).astype(np.float64)
    large = max_exact + (np.log(nf / max_exact) / math.log(MAX_DISTANCE / max_exact)
                         * (N_BUCKETS - max_exact)).astype(np.int32)
    large = np.minimum(large, N_BUCKETS - 1)
    return np.where(n < max_exact, n, large).astype(np.int32)


def _proj_epilogue(acc, mode, cos_ref, sin_ref, out_ref, col):
    if mode == "plain":
        out_ref[:, col:col + PROJ_TN] = acc.astype(out_ref.dtype)
        return
    half = RET_KEY_DIM // 2
    cos = cos_ref[...]
    sin = sin_ref[...]
    for c0 in range(0, PROJ_TN, RET_KEY_DIM):
        x1 = acc[:, c0:c0 + half]
        x2 = acc[:, c0 + half:c0 + RET_KEY_DIM]
        o1 = x1 * cos - x2 * sin
        o2 = x1 * sin + x2 * cos
        if mode == "rotary_k":
            o1 = o1 * (RET_KEY_DIM ** -0.5)
            o2 = o2 * (RET_KEY_DIM ** -0.5)
        out_ref[:, col + c0:col + c0 + half] = o1.astype(out_ref.dtype)
        out_ref[:, col + c0 + half:col + c0 + RET_KEY_DIM] = o2.astype(out_ref.dtype)


def _proj_kernel(h_ref, g_ref, w_ref, cos_ref, sin_ref, *refs, n_jobs):
    src = refs[:n_jobs]
    proj_ref, side_ref = refs[n_jobs:n_jobs + 2]
    dst_refs = refs[n_jobs + 2:2 * n_jobs + 2]
    xn_ref = refs[-1]
    xn_ref[...] = (_rms_scale(h_ref[...]) * g_ref[...]).astype(BF16)
    outs = {"proj": proj_ref, "side": side_ref}
    n_t = len(PROJ_TILES)
    for t, (dst, col, mode) in enumerate(PROJ_TILES):
        acc = _dot(xn_ref[...], w_ref[:, t * PROJ_TN:(t + 1) * PROJ_TN])
        _proj_epilogue(acc, mode, cos_ref, sin_ref, outs[dst], col)
        if n_jobs >= 2:
            wgu_ref, wd_ref = src[:2]
            wgubf_ref, wdbf_ref = dst_refs[:2]
            gu_cols = wgu_ref.shape[1] // n_t
            for j in range(t * gu_cols // FFN_TF, (t + 1) * gu_cols // FFN_TF):
                c = (j % FFN_TILES) * 2 * FFN_TF + (j // FFN_TILES) * FFN_TF
                wgubf_ref[:, c:c + FFN_TF] = wgu_ref[:, j * FFN_TF:(j + 1) * FFN_TF].astype(BF16)
            d_rows = wd_ref.shape[0] // n_t
            wdbf_ref[t * d_rows:(t + 1) * d_rows] = wd_ref[t * d_rows:(t + 1) * d_rows].astype(BF16)
        if n_jobs == 4:
            win_ref, wo_ref = src[2:]
            winbf_ref, wobf_ref = dst_refs[2:]
            winbf_ref[:, t * PROJ_TN:(t + 1) * PROJ_TN] = win_ref[:, t * PROJ_TN:(t + 1) * PROJ_TN].astype(BF16)
            if t == 0:
                wobf_ref[...] = wo_ref[...].astype(BF16)


def _proj_call(h, g, layer, w_in_bf, cos, sin, tm, convert=()):
    m = h.shape[0]
    n_steps = m // tm
    cos_blocks = cos.shape[0] // tm
    n_jobs = len(convert)
    assert n_jobs in (0, 2, 4)
    slab_in, slab_out, slab_shape = [], [], []
    for w, w_layer in convert:
        rows, cols = w.shape[1] // n_steps, w.shape[2]
        assert w.shape[1] % n_steps == 0 and rows % 16 == 0
        slab_in.append(pl.BlockSpec((None, rows, cols), lambda i, w_layer=w_layer: (w_layer, i, 0)))
        slab_out.append(pl.BlockSpec((rows, cols), lambda i: (i, 0)))
        slab_shape.append(jax.ShapeDtypeStruct(w.shape[1:], BF16))
    return pl.pallas_call(
        functools.partial(_proj_kernel, n_jobs=n_jobs),
        grid=(n_steps,),
        in_specs=[
            pl.BlockSpec((tm, D_MODEL), lambda i: (i, 0)),
            _resident((None, 1, D_MODEL), lambda i: (layer, 0, 0)),
            _resident((D_MODEL, IN_COLS), lambda i: (0, 0)),
            pl.BlockSpec((tm, RET_KEY_DIM // 2), lambda i: (i % cos_blocks, 0)),
            pl.BlockSpec((tm, RET_KEY_DIM // 2), lambda i: (i % cos_blocks, 0)),
        ] + slab_in,
        out_specs=[
            pl.BlockSpec((tm, PROJ_COLS), lambda i: (i, 0)),
            pl.BlockSpec((tm, SIDE_COLS), lambda i: (i, 0)),
        ] + slab_out,
        out_shape=[jax.ShapeDtypeStruct((m, PROJ_COLS), BF16),
                   jax.ShapeDtypeStruct((m, SIDE_COLS), F32)] + slab_shape,
        scratch_shapes=[pltpu.VMEM((tm, D_MODEL), BF16)],
        compiler_params=pltpu.CompilerParams(
            dimension_semantics=("arbitrary",), vmem_limit_bytes=VMEM_LIMIT_BYTES),
        name="proj",
    )(h, g, w_in_bf, cos, sin, *[w for w, _ in convert])


def _proj_cast_kernel(h_ref, g_ref, w_ref, cos_ref, sin_ref, proj_ref, side_ref, wbf_ref, xn_ref):
    j = pl.program_id(0)

    @pl.when(j == 0)
    def _():
        xn_ref[...] = (_rms_scale(h_ref[...]) * g_ref[...]).astype(BF16)

    w_bf = w_ref[...].astype(BF16)
    wbf_ref[...] = w_bf
    acc = _dot(xn_ref[...], w_bf)
    outs = {"proj": proj_ref, "side": side_ref}
    for t, (dst, col, mode) in enumerate(PROJ_TILES):
        @pl.when(j == t)
        def _():
            _proj_epilogue(acc, mode, cos_ref, sin_ref, outs[dst], col)


def _proj_cast_call(h, g, layer, w_in, cos, sin):
    m = h.shape[0]
    full = lambda shape: _resident(shape, lambda j: (0,) * len(shape))
    return pl.pallas_call(
        _proj_cast_kernel,
        grid=(len(PROJ_TILES),),
        in_specs=[
            full((m, D_MODEL)),
            _resident((None, 1, D_MODEL), lambda j: (layer, 0, 0)),
            pl.BlockSpec((None, D_MODEL, PROJ_TN), lambda j: (layer, 0, j)),
            full((m, RET_KEY_DIM // 2)),
            full((m, RET_KEY_DIM // 2)),
        ],
        out_specs=[
            pl.BlockSpec((m, PROJ_COLS), lambda j: (0, 0)),
            pl.BlockSpec((m, SIDE_COLS), lambda j: (0, 0)),
            pl.BlockSpec((D_MODEL, PROJ_TN), lambda j: (0, j)),
        ],
        out_shape=[jax.ShapeDtypeStruct((m, PROJ_COLS), BF16),
                   jax.ShapeDtypeStruct((m, SIDE_COLS), F32),
                   jax.ShapeDtypeStruct((D_MODEL, IN_COLS), BF16)],
        scratch_shapes=[pltpu.VMEM((m, D_MODEL), BF16)],
        compiler_params=pltpu.CompilerParams(
            dimension_semantics=("arbitrary",), vmem_limit_bytes=VMEM_LIMIT_BYTES),
        name="proj_cast",
    )(h, g, w_in, cos, sin)


def _softmax_sink(s, sink):
    m = jnp.maximum(jnp.max(s, axis=-1, keepdims=True), sink)
    p = jnp.exp(s - m)
    return p, 1.0 / (jnp.sum(p, axis=-1, keepdims=True) + jnp.exp(sink - m))


def _gate_out(o, g):
    return (_rms_scale(o) * _silu(g)).astype(BF16)


WO_BATCHES = 2


def _mix_prompt_kernel(sink_ref, cdec_ref, qs_ref, qr_ref, vr_ref, kr_ref, g_ref, kvc_ref,
                       bias_ref, dmask_ref, qdec_ref, kdec_ref, wo_ref, h_ref, out_ref, st_ref, kvp_ref, *cat_refs):
    n = pl.program_id(0)

    @pl.when(n == 0)
    def _():
        st_ref[...] = jnp.zeros(st_ref.shape, F32)
        kvp_ref[...] = jnp.zeros(kvp_ref.shape, BF16)

    scale = SWA_HEAD_DIM ** -0.5
    v_off = SWA_KV_HEADS * SWA_HEAD_DIM

    def cat_of(b):
        return cat_refs[b // WO_BATCHES].at[b % WO_BATCHES]

    def wo_rows(b0):
        bs = slice(b0, b0 + WO_BATCHES)
        cat = cat_refs[b0 // WO_BATCHES][...].reshape(WO_BATCHES * BLOCK, D_MODEL)
        out_ref[bs] = h_ref[bs] + _dot(cat, wo_ref[...]).reshape(WO_BATCHES, BLOCK, D_MODEL)

    def attention(b, kh):
        c0 = kh * SWA_HEAD_DIM
        k_cur = kvc_ref[b, :, c0:c0 + SWA_HEAD_DIM].astype(BF16)
        v_cur = kvc_ref[b, :, v_off + c0:v_off + c0 + SWA_HEAD_DIM].astype(BF16)
        k_cat = jnp.concatenate([kvp_ref[b, :, c0:c0 + SWA_HEAD_DIM], k_cur], axis=0)
        v_cat = jnp.concatenate([kvp_ref[b, :, v_off + c0:v_off + c0 + SWA_HEAD_DIM], v_cur], axis=0)
        kvp_ref[b, :, c0:c0 + SWA_HEAD_DIM] = k_cur
        kvp_ref[b, :, v_off + c0:v_off + c0 + SWA_HEAD_DIM] = v_cur
        for gq in range(SWA_GROUP):
            h = kh * SWA_GROUP + gq
            q = qs_ref[b, :, h * SWA_HEAD_DIM:(h + 1) * SWA_HEAD_DIM]
            s = _dot_nt(q, k_cat) * scale + bias_ref[h]
            p, inv = _softmax_sink(s, sink_ref[h])
            o = _dot(p.astype(BF16), v_cat) * inv
            cat_of(b)[:, h * SWA_HEAD_DIM:(h + 1) * SWA_HEAD_DIM] = o.astype(BF16)

    def retention(b, h):
        cs = slice(h * RET_KEY_DIM, (h + 1) * RET_KEY_DIM)
        q = qr_ref[b, :, cs]
        k32 = kr_ref[b, :, cs]
        v = vr_ref[b, :, cs]
        s = _dot_nt(q, k32.astype(BF16)) * dmask_ref[h]
        o = _dot(s.astype(BF16), v)
        st = st_ref[b, h]
        o = o + _dot(q, st.astype(BF16)) * qdec_ref[:, cs]
        kd = (k32 * kdec_ref[:, cs]).astype(BF16)
        st_ref[b, h] = cdec_ref[h] * st + _dot_tn(kd, v)
        cat_of(b)[:, SWA_WIDTH + h * RET_VAL_DIM:SWA_WIDTH + (h + 1) * RET_VAL_DIM] = (
            _gate_out(o, g_ref[b, :, cs]))

    for b in range(BATCH):
        for kh in range(SWA_KV_HEADS):
            attention(b, kh)
        for h in range(RET_HEADS):
            retention(b, h)
        if (b + 1) % WO_BATCHES == 0:
            wo_rows(b + 1 - WO_BATCHES)


def _mix_prompt_call(proj, side, h, w_o_bf, sinks_l, cdec, bias, dmask, qdec, kdec):
    nblk = SEQ // BLOCK
    proj3 = proj.reshape(BATCH, SEQ, PROJ_COLS)
    side3 = side.reshape(BATCH, SEQ, SIDE_COLS)
    h3 = h.reshape(BATCH, SEQ, D_MODEL)
    smem = pl.BlockSpec(memory_space=pltpu.SMEM)
    full = lambda shape: _resident(shape, lambda n: (0,) * len(shape))
    cur = lambda n: n
    prv = lambda n: n
    out, st = pl.pallas_call(
        _mix_prompt_kernel,
        grid=(nblk,),
        in_specs=[
            smem, smem,
            pl.BlockSpec((BATCH, BLOCK, 1024), lambda n: (0, cur(n), 0)),
            pl.BlockSpec((BATCH, BLOCK, 1024), lambda n: (0, cur(n), 1)),
            pl.BlockSpec((BATCH, BLOCK, 1024), lambda n: (0, cur(n), 2)),
            pl.BlockSpec((BATCH, BLOCK, 1024), lambda n: (0, cur(n), 0)),
            pl.BlockSpec((BATCH, BLOCK, 1024), lambda n: (0, cur(n), 1)),
            pl.BlockSpec((BATCH, BLOCK, 512), lambda n: (0, cur(n), 4)),
            pl.BlockSpec((None, SWA_HEADS, BLOCK, 2 * BLOCK), lambda n: (jnp.minimum(n, 1), 0, 0, 0)),
            full((RET_HEADS, RET_CHUNK, RET_CHUNK)),
            full((RET_CHUNK, RET_WIDTH)),
            full((RET_CHUNK, RET_WIDTH)),
            full((D_MODEL, D_MODEL)),
            pl.BlockSpec((BATCH, BLOCK, D_MODEL), lambda n: (0, prv(n), 0)),
        ],
        out_specs=[
            pl.BlockSpec((BATCH, BLOCK, D_MODEL), lambda n: (0, prv(n), 0)),
            _resident((BATCH, RET_HEADS, RET_KEY_DIM, RET_VAL_DIM), lambda n: (0, 0, 0, 0)),
        ],
        out_shape=[jax.ShapeDtypeStruct((BATCH, SEQ, D_MODEL), F32),
                   jax.ShapeDtypeStruct((BATCH, RET_HEADS, RET_KEY_DIM, RET_VAL_DIM), F32)],
        scratch_shapes=[pltpu.VMEM((BATCH, BLOCK, 2 * SWA_KV_HEADS * SWA_HEAD_DIM), BF16)]
        + [pltpu.VMEM((WO_BATCHES, BLOCK, D_MODEL), BF16)] * (BATCH // WO_BATCHES),
        compiler_params=pltpu.CompilerParams(
            dimension_semantics=("arbitrary",), vmem_limit_bytes=VMEM_LIMIT_BYTES),
        name="mix_prompt",
    )(sinks_l, cdec, proj3, proj3, proj3, side3, side3, side3, bias, dmask, qdec, kdec, w_o_bf, h3)
    return out.reshape(BATCH * SEQ, D_MODEL), st


SAMPLE_BB = 4
SAMPLE_ROWS = SAMPLE_BB * DEC_SEQ
N_SAMPLE = DEC_BATCH * DEC_SEQ
CACHE_ROWS = WINDOW * SWA_KV_HEADS


def _mix_sample_kernel(sink_ref, cdec_ref, prow_ref, srow_ref, pall_ref, sall_ref, ck_ref, cv_ref,
                       knew_ref, vnew_ref, st_ref, bias_ref, dmask_ref, qdec_ref, kdec_ref, *rest, layer):
    cat_ref, stout_ref, ckout_ref, cvout_ref = rest[-4:]
    if layer == 0:
        for ref in (stout_ref, ckout_ref, cvout_ref):
            ref[1:] = jnp.zeros((DEPTH - 1,) + ref.shape[1:], F32)
        stout_ref, ckout_ref, cvout_ref = stout_ref.at[0], ckout_ref.at[0], cvout_ref.at[0]
    r = SAMPLE_ROWS
    row_b = lax.broadcasted_iota(jnp.int32, (r, 1), 0) // DEC_SEQ
    row_b4 = lax.broadcasted_iota(jnp.int32, (SWA_GROUP * r, 1), 0) % r // DEC_SEQ
    scale = SWA_HEAD_DIM ** -0.5
    v_off = SWA_KV_HEADS * SWA_HEAD_DIM
    new_rows = DEC_SEQ * SWA_KV_HEADS

    for bi in range(SAMPLE_BB):
        ckout_ref[bi, :CACHE_ROWS - new_rows] = ck_ref[bi, new_rows:]
        ckout_ref[bi, CACHE_ROWS - new_rows:] = knew_ref[bi]
        cvout_ref[bi, :CACHE_ROWS - new_rows] = cv_ref[bi, new_rows:]
        cvout_ref[bi, CACHE_ROWS - new_rows:] = vnew_ref[bi]

    for kh in range(SWA_KV_HEADS):
        c0 = kh * SWA_HEAD_DIM
        q4 = jnp.concatenate(
            [prow_ref[:, (kh * SWA_GROUP + gq) * SWA_HEAD_DIM:(kh * SWA_GROUP + gq + 1) * SWA_HEAD_DIM]
             for gq in range(SWA_GROUP)], axis=0)
        k_new = sall_ref[:, KV_COL + c0:KV_COL + c0 + SWA_HEAD_DIM].astype(BF16)
        v_new = sall_ref[:, KV_COL + v_off + c0:KV_COL + v_off + c0 + SWA_HEAD_DIM].astype(BF16)
        s_cache = jnp.zeros((SWA_GROUP * r, CACHE_ROWS), F32)
        for bi in range(SAMPLE_BB):
            s_cache = jnp.where(row_b4 == bi, _dot_nt(q4, ck_ref[bi].astype(BF16)), s_cache)
        s_new = _dot_nt(q4, k_new)
        s4 = jnp.concatenate([s_cache, s_new], axis=1) * scale
        p_parts, inv_parts = [], []
        for gq in range(SWA_GROUP):
            h = kh * SWA_GROUP + gq
            s = s4[gq * r:(gq + 1) * r] + bias_ref[kh, h]
            p, inv = _softmax_sink(s, sink_ref[h])
            p_parts.append(p)
            inv_parts.append(inv)
        p4 = jnp.concatenate(p_parts, axis=0)
        inv4 = jnp.concatenate(inv_parts, axis=0)
        p_cache = p4[:, :CACHE_ROWS]
        o4 = _dot(p4[:, CACHE_ROWS:].astype(BF16), v_new)
        for bi in range(SAMPLE_BB):
            o4 = o4 + _dot(jnp.where(row_b4 == bi, p_cache, 0.0).astype(BF16), cv_ref[bi].astype(BF16))
        o4 = o4 * inv4
        for gq in range(SWA_GROUP):
            h = kh * SWA_GROUP + gq
            cat_ref[:, h * SWA_HEAD_DIM:(h + 1) * SWA_HEAD_DIM] = o4[gq * r:(gq + 1) * r].astype(BF16)

    for h in range(RET_HEADS):
        cs = slice(h * RET_KEY_DIM, (h + 1) * RET_KEY_DIM)
        q = prow_ref[:, 1024 + h * RET_KEY_DIM:1024 + (h + 1) * RET_KEY_DIM]
        k_all = sall_ref[:, cs].astype(BF16)
        v_all = pall_ref[:, 2048 + h * RET_VAL_DIM:2048 + (h + 1) * RET_VAL_DIM]
        s = _dot_nt(q, k_all) * dmask_ref[h]
        o = _dot(s.astype(BF16), v_all)
        k32 = srow_ref[:, cs]
        v = prow_ref[:, 2048 + h * RET_VAL_DIM:2048 + (h + 1) * RET_VAL_DIM]
        kd = k32 * kdec_ref[:, cs]
        cross = jnp.zeros((r, RET_VAL_DIM), F32)
        for bi in range(SAMPLE_BB):
            st = st_ref[bi, h]
            cross = jnp.where(row_b == bi, _dot(q, st.astype(BF16)), cross)
            kd_b = jnp.where(row_b == bi, kd, 0.0).astype(BF16)
            stout_ref[bi, h] = cdec_ref[h] * st + _dot_tn(kd_b, v)
        o = o + cross * qdec_ref[:, cs]
        g = srow_ref[:, 1024 + h * RET_VAL_DIM:1024 + (h + 1) * RET_VAL_DIM]
        cat_ref[:, SWA_WIDTH + h * RET_VAL_DIM:SWA_WIDTH + (h + 1) * RET_VAL_DIM] = _gate_out(o, g)


def _mix_sample_call(proj, side, cache_k, cache_v, state_ret, layer, sinks_l, cdec, bias, dmask, qdec, kdec,
                     stacked):
    r = SAMPLE_ROWS
    new_rows = DEC_SEQ * SWA_KV_HEADS
    kv_w = SWA_KV_HEADS * SWA_HEAD_DIM
    k_new = side[:, KV_COL:KV_COL + kv_w].reshape(DEC_BATCH, new_rows, SWA_HEAD_DIM)
    v_new = side[:, KV_COL + kv_w:KV_COL + 2 * kv_w].reshape(DEC_BATCH, new_rows, SWA_HEAD_DIM)
    smem = pl.BlockSpec(memory_space=pltpu.SMEM)
    full = lambda shape: pl.BlockSpec(shape, lambda c: (0,) * len(shape))
    cache_in = pl.BlockSpec((None, SAMPLE_BB, CACHE_ROWS, SWA_HEAD_DIM), lambda c: (layer, c, 0, 0))
    state_dims = (SAMPLE_BB, RET_HEADS, RET_KEY_DIM, RET_VAL_DIM)
    cache_dims = (SAMPLE_BB, CACHE_ROWS, SWA_HEAD_DIM)
    if layer == 0:
        out_block = lambda dims: pl.BlockSpec((DEPTH,) + dims, lambda c: (0, c) + (0,) * (len(dims) - 1))
    else:
        out_block = lambda dims: pl.BlockSpec((None,) + dims, lambda c: (layer, c) + (0,) * (len(dims) - 1))
    new_in = pl.BlockSpec((SAMPLE_BB, new_rows, SWA_HEAD_DIM), lambda c: (c, 0, 0))
    in_specs = [
        smem, smem,
        pl.BlockSpec((r, PROJ_COLS), lambda c: (c, 0)),
        pl.BlockSpec((r, SIDE_COLS), lambda c: (c, 0)),
        full((N_SAMPLE, PROJ_COLS)),
        full((N_SAMPLE, SIDE_COLS)),
        cache_in, cache_in, new_in, new_in,
        pl.BlockSpec((None, SAMPLE_BB, RET_HEADS, RET_KEY_DIM, RET_VAL_DIM), lambda c: (layer, c, 0, 0, 0)),
        pl.BlockSpec((SWA_KV_HEADS, SWA_HEADS, r, CACHE_ROWS + N_SAMPLE), lambda c: (0, 0, c, 0)),
        pl.BlockSpec((RET_HEADS, r, N_SAMPLE), lambda c: (0, c, 0)),
        pl.BlockSpec((r, RET_WIDTH), lambda c: (c, 0)),
        pl.BlockSpec((r, RET_WIDTH), lambda c: (c, 0)),
    ]
    args = [sinks_l, cdec, proj, side, proj, side, cache_k, cache_v, k_new, v_new, state_ret, bias, dmask, qdec,
            kdec]
    aliases = {}
    if stacked is not None:
        for k, buf in enumerate(stacked):
            aliases[len(args)] = 1 + k
            in_specs.append(pl.BlockSpec(memory_space=pl.ANY))
            args.append(buf)
    cache_shape = jax.ShapeDtypeStruct((DEPTH, DEC_BATCH, CACHE_ROWS, SWA_HEAD_DIM), F32)
    cat, *new_stacked = pl.pallas_call(
        functools.partial(_mix_sample_kernel, layer=layer),
        grid=(DEC_BATCH // SAMPLE_BB,),
        in_specs=in_specs,
        out_specs=[
            pl.BlockSpec((r, D_MODEL), lambda c: (c, 0)),
            out_block(state_dims), out_block(cache_dims), out_block(cache_dims),
        ],
        out_shape=[jax.ShapeDtypeStruct((N_SAMPLE, D_MODEL), BF16),
                   jax.ShapeDtypeStruct((DEPTH, DEC_BATCH, RET_HEADS, RET_KEY_DIM, RET_VAL_DIM), F32),
                   cache_shape, cache_shape],
        input_output_aliases=aliases,
        compiler_params=pltpu.CompilerParams(
            dimension_semantics=("arbitrary",), vmem_limit_bytes=VMEM_LIMIT_BYTES),
        name="mix_sample",
    )(*args)
    return cat, tuple(new_stacked)


WO_TN = 512


def _wo_cast_kernel(cat_ref, w_ref, h_ref, out_ref, wbf_ref):
    w_bf = w_ref[...].astype(BF16)
    wbf_ref[...] = w_bf
    out_ref[...] = h_ref[...] + _dot(cat_ref[...], w_bf)


def _wo_kernel(cat_ref, w_ref, h_ref, out_ref):
    out_ref[...] = h_ref[...] + _dot(cat_ref[...], w_ref[...])


def _wo_call(cat, w_o_bf, h):
    m = h.shape[0]
    return pl.pallas_call(
        _wo_kernel,
        grid=(D_MODEL // WO_TN,),
        in_specs=[
            _resident((m, D_MODEL), lambda j: (0, 0)),
            pl.BlockSpec((D_MODEL, WO_TN), lambda j: (0, j)),
            pl.BlockSpec((m, WO_TN), lambda j: (0, j)),
        ],
        out_specs=pl.BlockSpec((m, WO_TN), lambda j: (0, j)),
        out_shape=jax.ShapeDtypeStruct((m, D_MODEL), F32),
        compiler_params=pltpu.CompilerParams(
            dimension_semantics=("arbitrary",), vmem_limit_bytes=VMEM_LIMIT_BYTES),
        name="wo",
    )(cat, w_o_bf, h)


def _wo_cast_call(cat, w_o, layer, h):
    m = h.shape[0]
    return pl.pallas_call(
        _wo_cast_kernel,
        grid=(D_MODEL // WO_TN,),
        in_specs=[
            _resident((m, D_MODEL), lambda j: (0, 0)),
            pl.BlockSpec((None, D_MODEL, WO_TN), lambda j: (layer, 0, j)),
            pl.BlockSpec((m, WO_TN), lambda j: (0, j)),
        ],
        out_specs=[pl.BlockSpec((m, WO_TN), lambda j: (0, j)),
                   pl.BlockSpec((D_MODEL, WO_TN), lambda j: (0, j))],
        out_shape=[jax.ShapeDtypeStruct((m, D_MODEL), F32),
                   jax.ShapeDtypeStruct((D_MODEL, D_MODEL), BF16)],
        compiler_params=pltpu.CompilerParams(
            dimension_semantics=("arbitrary",), vmem_limit_bytes=VMEM_LIMIT_BYTES),
        name="wo_cast",
    )(cat, w_o, h)


def _ffn_kernel(h_ref, g_ref, wgu_ref, wd_ref, gfin_ref, out_ref, xn_ref, *, final_norm):
    f = pl.program_id(1)

    @pl.when(f == 0)
    def _():
        x = h_ref[...]
        xn_ref[...] = (_rms_scale(x) * g_ref[...]).astype(BF16)
        out_ref[...] = x

    ab = _dot(xn_ref[...], wgu_ref[...])
    act = (_silu(ab[:, :FFN_TF]) * ab[:, FFN_TF:]).astype(BF16)
    out_ref[...] += _dot(act, wd_ref[...])

    if final_norm:
        @pl.when(f == pl.num_programs(1) - 1)
        def _():
            out_ref[...] = _rms_scale(out_ref[...]) * gfin_ref[...]


def _ffn_call(h, g, w_gu_bf, w_d_bf, g_final, layer, tm):
    m = h.shape[0]
    return pl.pallas_call(
        functools.partial(_ffn_kernel, final_norm=(layer == DEPTH - 1)),
        grid=(m // tm, FFN_TILES),
        in_specs=[
            pl.BlockSpec((tm, D_MODEL), lambda i, f: (i, 0)),
            pl.BlockSpec((None, 1, D_MODEL), lambda i, f: (layer, 0, 0)),
            pl.BlockSpec((D_MODEL, 2 * FFN_TF), lambda i, f: (0, f)),
            pl.BlockSpec((FFN_TF, D_MODEL), lambda i, f: (f, 0)),
            pl.BlockSpec((1, D_MODEL), lambda i, f: (0, 0)),
        ],
        out_specs=pl.BlockSpec((tm, D_MODEL), lambda i, f: (i, 0)),
        out_shape=jax.ShapeDtypeStruct((m, D_MODEL), F32),
        scratch_shapes=[pltpu.VMEM((tm, D_MODEL), BF16)],
        compiler_params=pltpu.CompilerParams(
            dimension_semantics=("arbitrary", "arbitrary"), vmem_limit_bytes=VMEM_LIMIT_BYTES),
        name="ffn",
    )(h, g, w_gu_bf, w_d_bf, g_final)


def _rope_tables(pos):
    half = RET_KEY_DIM // 2
    inv = 1.0 / (ROPE_BASE ** jnp.linspace(0.0, 1.0, half, dtype=F32))
    ang = pos.astype(F32)[:, None] * inv[None, :]
    return jnp.cos(ang), jnp.sin(ang)


def _decay_tables(c):
    lg = jnp.log(1.0 - jnp.exp2(-5.0 - jnp.arange(RET_HEADS, dtype=F32)))
    idx = jnp.arange(c, dtype=F32)
    diff = idx[:, None] - idx[None, :]
    dmask = jnp.where(diff[None] >= 0, jnp.exp(jnp.maximum(diff, 0.0)[None] * lg[:, None, None]), 0.0)
    q_decay = jnp.exp((idx[:, None] + 1.0) * lg[None, :])
    k_decay = jnp.exp((c - 1.0 - idx)[:, None] * lg[None, :])
    c_decay = jnp.exp(c * lg)
    return dmask, q_decay, k_decay, c_decay


def _per_head_cols(t):
    return jnp.repeat(t, RET_KEY_DIM, axis=1)


def kernel(x_prompt, x_sample, cache_k_win, cache_v_win, state_ret, rel_bias, w_in, sinks, w_o,
           norm_mix, norm_ffn, w_gate_up, w_down, norm_final):
    norm_mix3 = norm_mix.reshape(DEPTH, 1, D_MODEL)
    norm_ffn3 = norm_ffn.reshape(DEPTH, 1, D_MODEL)
    norm_final2 = norm_final.reshape(1, D_MODEL)
    cache_k = cache_k_win.reshape(DEPTH, DEC_BATCH, CACHE_ROWS, SWA_HEAD_DIM)
    cache_v = cache_v_win.reshape(DEPTH, DEC_BATCH, CACHE_ROWS, SWA_HEAD_DIM)

    cos_p, sin_p = _rope_tables(jnp.arange(SEQ, dtype=jnp.int32))
    cos_4, sin_4 = _rope_tables(PAST_LEN + jnp.arange(DEC_SEQ, dtype=jnp.int32))
    cos_s = jnp.tile(cos_4, (DEC_BATCH, 1))
    sin_s = jnp.tile(sin_4, (DEC_BATCH, 1))

    dmask_p, qd_p, kd_p, cdec_p = _decay_tables(RET_CHUNK)
    qdec_p = _per_head_cols(qd_p)
    kdec_p = _per_head_cols(kd_p)
    dmask_4, qd_4, kd_4, cdec_s = _decay_tables(DEC_SEQ)
    eye_b = jnp.eye(DEC_BATCH, dtype=F32)
    dmask_s = jax.vmap(lambda d: jnp.kron(eye_b, d))(dmask_4)
    qdec_s = jnp.tile(_per_head_cols(qd_4), (DEC_BATCH, 1))
    kdec_s = jnp.tile(_per_head_cols(kd_4), (DEC_BATCH, 1))

    qi = np.arange(BLOCK)[:, None]
    kj = np.arange(2 * BLOCK)[None, :]
    delta_p = qi + BLOCK - kj
    in_window = (delta_p >= 0) & (delta_p < WINDOW)
    valid_p = np.stack([in_window & (kj >= BLOCK), in_window]).astype(np.float32)
    rows = np.arange(N_SAMPLE)
    rb, rt = rows // DEC_SEQ, rows % DEC_SEQ
    cache_pos = np.arange(CACHE_ROWS) // SWA_KV_HEADS
    cache_head = np.arange(CACHE_ROWS) % SWA_KV_HEADS
    delta_cache = (WINDOW + rt)[:, None] - cache_pos[None, :]
    delta_new = rt[:, None] - rt[None, :]
    same_b = rb[:, None] == rb[None, :]
    delta_s = np.concatenate([delta_cache, delta_new], axis=1)
    valid_s = np.stack([
        np.concatenate([(delta_cache < WINDOW) & (cache_head == kh)[None, :], same_b & (delta_new >= 0)], axis=1)
        for kh in range(SWA_KV_HEADS)]).astype(np.float32)
    rel_bias_t = rel_bias.T
    bias_p = _expand_bias(rel_bias_t, jnp.asarray(_t5_bucket_np(delta_p)), jnp.asarray(valid_p))
    bias_s = _expand_bias(rel_bias_t, jnp.asarray(_t5_bucket_np(delta_s)), jnp.asarray(valid_s))

    hp = x_prompt.reshape(BATCH * SEQ, D_MODEL)
    hs = x_sample.reshape(N_SAMPLE, D_MODEL)
    tm_proj, tm_ffn = 256, 1024
    kp_new, vp_new, rp_new = [], [], []
    sample_new = None
    kv_w = SWA_KV_HEADS * SWA_HEAD_DIM
    for l in range(DEPTH):
        if l == 0:
            proj_s, side_s, w_in_bf = _proj_cast_call(hs, norm_mix3, l, w_in, cos_s, sin_s)
        else:
            proj_s, side_s = _proj_call(hs, norm_mix3, l, w_in_bf, cos_s, sin_s, N_SAMPLE)
        convert = [(w_gate_up, l), (w_down, l)]
        if l + 1 < DEPTH:
            convert += [(w_in, l + 1), (w_o, l + 1)]
        proj, side, w_gu_bf, w_d_bf, *next_bf = _proj_call(hp, norm_mix3, l, w_in_bf, cos_p, sin_p, tm_proj,
                                                            convert)

        cat, sample_new = _mix_sample_call(proj_s, side_s, cache_k, cache_v, state_ret, l, sinks[l], cdec_s,
                                           bias_s, dmask_s, qdec_s, kdec_s, sample_new)
        if l == 0:
            hs, w_o_bf = _wo_cast_call(cat, w_o, l, hs)
        else:
            hs = _wo_call(cat, w_o_bf, hs)
        hs = _ffn_call(hs, norm_ffn3, w_gu_bf, w_d_bf, norm_final2, l, N_SAMPLE)

        hp, st_p = _mix_prompt_call(proj, side, hp, w_o_bf, sinks[l], cdec_p, bias_p, dmask_p, qdec_p, kdec_p)
        hp = _ffn_call(hp, norm_ffn3, w_gu_bf, w_d_bf, norm_final2, l, tm_ffn)
        kv_tail = side.reshape(BATCH, SEQ, SIDE_COLS)[:, SEQ - WINDOW:, KV_COL:KV_COL + 2 * kv_w]
        kp_new.append(kv_tail[..., :kv_w].reshape(BATCH, WINDOW, SWA_KV_HEADS, SWA_HEAD_DIM))
        vp_new.append(kv_tail[..., kv_w:].reshape(BATCH, WINDOW, SWA_KV_HEADS, SWA_HEAD_DIM))
        rp_new.append(st_p)
        if next_bf:
            w_in_bf, w_o_bf = next_bf

    y_prompt = hp.reshape(BATCH, SEQ, D_MODEL)
    y_sample = hs.reshape(DEC_BATCH, DEC_SEQ, D_MODEL)
    rs_new, ks_new, vs_new = sample_new
    return (y_prompt, y_sample,
            jnp.stack(kp_new), jnp.stack(vp_new), jnp.stack(rp_new),
            ks_new.reshape(cache_k_win.shape), vs_new.reshape(cache_v_win.shape), rs_new)
```

```python
import functools
import math

import numpy as np
import jax
import jax.numpy as jnp
from jax import lax
from jax.experimental import pallas as pl
from jax.experimental.pallas import tpu as pltpu

D_MODEL = 2048
BATCH = 4
SEQ = 2048
DEPTH = 4
DEC_BATCH = 32
DEC_SEQ = 4
PAST_LEN = 16384

SWA_WIDTH = 1024
RET_WIDTH = 1024
SWA_HEADS = 8
SWA_KV_HEADS = 2
SWA_GROUP = SWA_HEADS // SWA_KV_HEADS
SWA_HEAD_DIM = 128
WINDOW = 128
BLOCK = WINDOW
RET_HEADS = 4
RET_KEY_DIM = 256
RET_VAL_DIM = 256
RET_CHUNK = 128
ROPE_BASE = 10000.0
N_BUCKETS = 32
MAX_DISTANCE = 128
EPS = 1e-6
D_FF = 5632
IN_COLS = 5632

F32 = jnp.float32
BF16 = jnp.bfloat16

VMEM_LIMIT_BYTES = 60 * 1024 * 1024

FFN_TF = 512
FFN_TILES = D_FF // FFN_TF
PROJ_TN = 512
PROJ_COLS = 3072
SIDE_COLS = 2560
KV_COL = 2048
PROJ_TILES = (
    ("proj", 0, "plain"), ("proj", 512, "plain"),
    ("side", KV_COL, "plain"),
    ("proj", 1024, "rotary"), ("proj", 1536, "rotary"),
    ("side", 0, "rotary_k"), ("side", 512, "rotary_k"),
    ("proj", 2048, "plain"), ("proj", 2560, "plain"),
    ("side", 1024, "plain"), ("side", 1536, "plain"),
)


def _rms_scale(x):
    return x * lax.rsqrt(jnp.mean(x * x, axis=-1, keepdims=True) + EPS)


def _silu(x):
    return x * jax.nn.sigmoid(x)


def _dot(a, b):
    return jnp.dot(a, b, preferred_element_type=F32)


def _dot_nt(a, b):
    return lax.dot_general(a, b, (((1,), (1,)), ((), ())), preferred_element_type=F32)


def _dot_tn(a, b):
    return lax.dot_general(a, b, (((0,), (0,)), ((), ())), preferred_element_type=F32)


def _resident(block_shape, index_map):
    return pl.BlockSpec(block_shape, index_map, pipeline_mode=pl.Buffered(1))


MASKED = -1e30


def _bias_kernel(rbt_ref, idx_ref, valid_ref, out_ref):
    h = pl.program_id(1)
    idx = idx_ref[...]
    acc = jnp.zeros(idx.shape, F32)
    for b in range(N_BUCKETS):
        acc = jnp.where(idx == b, rbt_ref[h, b], acc)
    out_ref[...] = jnp.where(valid_ref[...] > 0.5, acc, MASKED)


def _expand_bias(rel_bias_t, bucket_idx, valid):
    nv, rows, cols = valid.shape
    return pl.pallas_call(
        _bias_kernel,
        grid=(nv, SWA_HEADS),
        in_specs=[pl.BlockSpec(memory_space=pltpu.SMEM),
                  pl.BlockSpec((rows, cols), lambda v, h: (0, 0)),
                  pl.BlockSpec((None, rows, cols), lambda v, h: (v, 0, 0))],
        out_specs=pl.BlockSpec((None, None, rows, cols), lambda v, h: (v, h, 0, 0)),
        out_shape=jax.ShapeDtypeStruct((nv, SWA_HEADS, rows, cols), F32),
        name="bias_expand",
    )(rel_bias_t, bucket_idx, valid)


def _t5_bucket_np(delta):
    n = np.maximum(delta, 0)
    max_exact = N_BUCKETS // 2
    nf = np.maximum(n, 1).astype(np.float64)
    large = max_exact + (np.log(nf / max_exact) / math.log(MAX_DISTANCE / max_exact)
                         * (N_BUCKETS - max_exact)).astype(np.int32)
    large = np.minimum(large, N_BUCKETS - 1)
    return np.where(n < max_exact, n, large).astype(np.int32)


def _proj_epilogue(acc, mode, cos_ref, sin_ref, out_ref, col):
    if mode == "plain":
        out_ref[:, col:col + PROJ_TN] = acc.astype(out_ref.dtype)
        return
    half = RET_KEY_DIM // 2
    cos = cos_ref[...]
    sin = sin_ref[...]
    for c0 in range(0, PROJ_TN, RET_KEY_DIM):
        x1 = acc[:, c0:c0 + half]
        x2 = acc[:, c0 + half:c0 + RET_KEY_DIM]
        o1 = x1 * cos - x2 * sin
        o2 = x1 * sin + x2 * cos
        if mode == "rotary_k":
            o1 = o1 * (RET_KEY_DIM ** -0.5)
            o2 = o2 * (RET_KEY_DIM ** -0.5)
        out_ref[:, col + c0:col + c0 + half] = o1.astype(out_ref.dtype)
        out_ref[:, col + c0 + half:col + c0 + RET_KEY_DIM] = o2.astype(out_ref.dtype)


def _proj_kernel(h_ref, g_ref, w_ref, cos_ref, sin_ref, *refs, n_jobs):
    src = refs[:n_jobs]
    proj_ref, side_ref = refs[n_jobs:n_jobs + 2]
    dst_refs = refs[n_jobs + 2:2 * n_jobs + 2]
    xn_ref = refs[-1]
    xn_ref[...] = (_rms_scale(h_ref[...]) * g_ref[...]).astype(BF16)
    outs = {"proj": proj_ref, "side": side_ref}
    n_t = len(PROJ_TILES)
    for t, (dst, col, mode) in enumerate(PROJ_TILES):
        acc = _dot(xn_ref[...], w_ref[:, t * PROJ_TN:(t + 1) * PROJ_TN])
        _proj_epilogue(acc, mode, cos_ref, sin_ref, outs[dst], col)
        if n_jobs >= 2:
            wgu_ref, wd_ref = src[:2]
            wgubf_ref, wdbf_ref = dst_refs[:2]
            gu_cols = wgu_ref.shape[1] // n_t
            for j in range(t * gu_cols // FFN_TF, (t + 1) * gu_cols // FFN_TF):
                c = (j % FFN_TILES) * 2 * FFN_TF + (j // FFN_TILES) * FFN_TF
                wgubf_ref[:, c:c + FFN_TF] = wgu_ref[:, j * FFN_TF:(j + 1) * FFN_TF].astype(BF16)
            d_rows = wd_ref.shape[0] // n_t
            wdbf_ref[t * d_rows:(t + 1) * d_rows] = wd_ref[t * d_rows:(t + 1) * d_rows].astype(BF16)
        if n_jobs == 4:
            win_ref, wo_ref = src[2:]
            winbf_ref, wobf_ref = dst_refs[2:]
            winbf_ref[:, t * PROJ_TN:(t + 1) * PROJ_TN] = win_ref[:, t * PROJ_TN:(t + 1) * PROJ_TN].astype(BF16)
            if t == 0:
                wobf_ref[...] = wo_ref[...].astype(BF16)


def _proj_call(h, g, layer, w_in_bf, cos, sin, tm, convert=()):
    m = h.shape[0]
    n_steps = m // tm
    cos_blocks = cos.shape[0] // tm
    n_jobs = len(convert)
    assert n_jobs in (0, 2, 4)
    slab_in, slab_out, slab_shape = [], [], []
    for w, w_layer in convert:
        rows, cols = w.shape[1] // n_steps, w.shape[2]
        assert w.shape[1] % n_steps == 0 and rows % 16 == 0
        slab_in.append(pl.BlockSpec((None, rows, cols), lambda i, w_layer=w_layer: (w_layer, i, 0)))
        slab_out.append(pl.BlockSpec((rows, cols), lambda i: (i, 0)))
        slab_shape.append(jax.ShapeDtypeStruct(w.shape[1:], BF16))
    return pl.pallas_call(
        functools.partial(_proj_kernel, n_jobs=n_jobs),
        grid=(n_steps,),
        in_specs=[
            pl.BlockSpec((tm, D_MODEL), lambda i: (i, 0)),
            _resident((None, 1, D_MODEL), lambda i: (layer, 0, 0)),
            _resident((D_MODEL, IN_COLS), lambda i: (0, 0)),
            pl.BlockSpec((tm, RET_KEY_DIM // 2), lambda i: (i % cos_blocks, 0)),
            pl.BlockSpec((tm, RET_KEY_DIM // 2), lambda i: (i % cos_blocks, 0)),
        ] + slab_in,
        out_specs=[
            pl.BlockSpec((tm, PROJ_COLS), lambda i: (i, 0)),
            pl.BlockSpec((tm, SIDE_COLS), lambda i: (i, 0)),
        ] + slab_out,
        out_shape=[jax.ShapeDtypeStruct((m, PROJ_COLS), BF16),
                   jax.ShapeDtypeStruct((m, SIDE_COLS), F32)] + slab_shape,
        scratch_shapes=[pltpu.VMEM((tm, D_MODEL), BF16)],
        compiler_params=pltpu.CompilerParams(
            dimension_semantics=("arbitrary",), vmem_limit_bytes=VMEM_LIMIT_BYTES),
        name="proj",
    )(h, g, w_in_bf, cos, sin, *[w for w, _ in convert])


def _proj_cast_kernel(h_ref, g_ref, w_ref, cos_ref, sin_ref, proj_ref, side_ref, wbf_ref, xn_ref):
    j = pl.program_id(0)

    @pl.when(j == 0)
    def _():
        xn_ref[...] = (_rms_scale(h_ref[...]) * g_ref[...]).astype(BF16)

    w_bf = w_ref[...].astype(BF16)
    wbf_ref[...] = w_bf
    acc = _dot(xn_ref[...], w_bf)
    outs = {"proj": proj_ref, "side": side_ref}
    for t, (dst, col, mode) in enumerate(PROJ_TILES):
        @pl.when(j == t)
        def _():
            _proj_epilogue(acc, mode, cos_ref, sin_ref, outs[dst], col)


def _proj_cast_call(h, g, layer, w_in, cos, sin):
    m = h.shape[0]
    full = lambda shape: _resident(shape, lambda j: (0,) * len(shape))
    return pl.pallas_call(
        _proj_cast_kernel,
        grid=(len(PROJ_TILES),),
        in_specs=[
            full((m, D_MODEL)),
            _resident((None, 1, D_MODEL), lambda j: (layer, 0, 0)),
            pl.BlockSpec((None, D_MODEL, PROJ_TN), lambda j: (layer, 0, j)),
            full((m, RET_KEY_DIM // 2)),
            full((m, RET_KEY_DIM // 2)),
        ],
        out_specs=[
            pl.BlockSpec((m, PROJ_COLS), lambda j: (0, 0)),
            pl.BlockSpec((m, SIDE_COLS), lambda j: (0, 0)),
            pl.BlockSpec((D_MODEL, PROJ_TN), lambda j: (0, j)),
        ],
        out_shape=[jax.ShapeDtypeStruct((m, PROJ_COLS), BF16),
                   jax.ShapeDtypeStruct((m, SIDE_COLS), F32),
                   jax.ShapeDtypeStruct((D_MODEL, IN_COLS), BF16)],
        scratch_shapes=[pltpu.VMEM((m, D_MODEL), BF16)],
        compiler_params=pltpu.CompilerParams(
            dimension_semantics=("arbitrary",), vmem_limit_bytes=VMEM_LIMIT_BYTES),
        name="proj_cast",
    )(h, g, w_in, cos, sin)


def _softmax_sink(s, sink):
    m = jnp.maximum(jnp.max(s, axis=-1, keepdims=True), sink)
    p = jnp.exp(s - m)
    return p, 1.0 / (jnp.sum(p, axis=-1, keepdims=True) + jnp.exp(sink - m))


def _gate_out(o, g):
    return (_rms_scale(o) * _silu(g)).astype(BF16)


WO_BATCHES = 2


def _mix_prompt_kernel(sink_ref, cdec_ref, qs_ref, qr_ref, vr_ref, kr_ref, g_ref, kvc_ref,
                       bias_ref, dmask_ref, qdec_ref, kdec_ref, wo_ref, h_ref, out_ref, st_ref, kvp_ref, *cat_refs):
    n = pl.program_id(0)

    @pl.when(n == 0)
    def _():
        st_ref[...] = jnp.zeros(st_ref.shape, F32)
        kvp_ref[...] = jnp.zeros(kvp_ref.shape, BF16)

    scale = SWA_HEAD_DIM ** -0.5
    v_off = SWA_KV_HEADS * SWA_HEAD_DIM

    def cat_of(b):
        return cat_refs[b // WO_BATCHES].at[b % WO_BATCHES]

    def wo_rows(b0):
        bs = slice(b0, b0 + WO_BATCHES)
        cat = cat_refs[b0 // WO_BATCHES][...].reshape(WO_BATCHES * BLOCK, D_MODEL)
        out_ref[bs] = h_ref[bs] + _dot(cat, wo_ref[...]).reshape(WO_BATCHES, BLOCK, D_MODEL)

    def attention(b, kh):
        c0 = kh * SWA_HEAD_DIM
        k_cur = kvc_ref[b, :, c0:c0 + SWA_HEAD_DIM].astype(BF16)
        v_cur = kvc_ref[b, :, v_off + c0:v_off + c0 + SWA_HEAD_DIM].astype(BF16)
        k_cat = jnp.concatenate([kvp_ref[b, :, c0:c0 + SWA_HEAD_DIM], k_cur], axis=0)
        v_cat = jnp.concatenate([kvp_ref[b, :, v_off + c0:v_off + c0 + SWA_HEAD_DIM], v_cur], axis=0)
        kvp_ref[b, :, c0:c0 + SWA_HEAD_DIM] = k_cur
        kvp_ref[b, :, v_off + c0:v_off + c0 + SWA_HEAD_DIM] = v_cur
        for gq in range(SWA_GROUP):
            h = kh * SWA_GROUP + gq
            q = qs_ref[b, :, h * SWA_HEAD_DIM:(h + 1) * SWA_HEAD_DIM]
            s = _dot_nt(q, k_cat) * scale + bias_ref[h]
            p, inv = _softmax_sink(s, sink_ref[h])
            o = _dot(p.astype(BF16), v_cat) * inv
            cat_of(b)[:, h * SWA_HEAD_DIM:(h + 1) * SWA_HEAD_DIM] = o.astype(BF16)

    def retention(b, h):
        cs = slice(h * RET_KEY_DIM, (h + 1) * RET_KEY_DIM)
        q = qr_ref[b, :, cs]
        k32 = kr_ref[b, :, cs]
        v = vr_ref[b, :, cs]
        s = _dot_nt(q, k32.astype(BF16)) * dmask_ref[h]
        o = _dot(s.astype(BF16), v)
        st = st_ref[b, h]
        o = o + _dot(q, st.astype(BF16)) * qdec_ref[:, cs]
        kd = (k32 * kdec_ref[:, cs]).astype(BF16)
        st_ref[b, h] = cdec_ref[h] * st + _dot_tn(kd, v)
        cat_of(b)[:, SWA_WIDTH + h * RET_VAL_DIM:SWA_WIDTH + (h + 1) * RET_VAL_DIM] = (
            _gate_out(o, g_ref[b, :, cs]))

    for b in range(BATCH):
        for kh in range(SWA_KV_HEADS):
            attention(b, kh)
        for h in range(RET_HEADS):
            retention(b, h)
        if (b + 1) % WO_BATCHES == 0:
            wo_rows(b + 1 - WO_BATCHES)


def _mix_prompt_call(proj, side, h, w_o_bf, sinks_l, cdec, bias, dmask, qdec, kdec):
    nblk = SEQ // BLOCK
    proj3 = proj.reshape(BATCH, SEQ, PROJ_COLS)
    side3 = side.reshape(BATCH, SEQ, SIDE_COLS)
    h3 = h.reshape(BATCH, SEQ, D_MODEL)
    smem = pl.BlockSpec(memory_space=pltpu.SMEM)
    full = lambda shape: _resident(shape, lambda n: (0,) * len(shape))
    cur = lambda n: n
    prv = lambda n: n
    out, st = pl.pallas_call(
        _mix_prompt_kernel,
        grid=(nblk,),
        in_specs=[
            smem, smem,
            pl.BlockSpec((BATCH, BLOCK, 1024), lambda n: (0, cur(n), 0)),
            pl.BlockSpec((BATCH, BLOCK, 1024), lambda n: (0, cur(n), 1)),
            pl.BlockSpec((BATCH, BLOCK, 1024), lambda n: (0, cur(n), 2)),
            pl.BlockSpec((BATCH, BLOCK, 1024), lambda n: (0, cur(n), 0)),
            pl.BlockSpec((BATCH, BLOCK, 1024), lambda n: (0, cur(n), 1)),
            pl.BlockSpec((BATCH, BLOCK, 512), lambda n: (0, cur(n), 4)),
            pl.BlockSpec((None, SWA_HEADS, BLOCK, 2 * BLOCK), lambda n: (jnp.minimum(n, 1), 0, 0, 0)),
            full((RET_HEADS, RET_CHUNK, RET_CHUNK)),
            full((RET_CHUNK, RET_WIDTH)),
            full((RET_CHUNK, RET_WIDTH)),
            full((D_MODEL, D_MODEL)),
            pl.BlockSpec((BATCH, BLOCK, D_MODEL), lambda n: (0, prv(n), 0)),
        ],
        out_specs=[
            pl.BlockSpec((BATCH, BLOCK, D_MODEL), lambda n: (0, prv(n), 0)),
            _resident((BATCH, RET_HEADS, RET_KEY_DIM, RET_VAL_DIM), lambda n: (0, 0, 0, 0)),
        ],
        out_shape=[jax.ShapeDtypeStruct((BATCH, SEQ, D_MODEL), F32),
                   jax.ShapeDtypeStruct((BATCH, RET_HEADS, RET_KEY_DIM, RET_VAL_DIM), F32)],
        scratch_shapes=[pltpu.VMEM((BATCH, BLOCK, 2 * SWA_KV_HEADS * SWA_HEAD_DIM), BF16)]
        + [pltpu.VMEM((WO_BATCHES, BLOCK, D_MODEL), BF16)] * (BATCH // WO_BATCHES),
        compiler_params=pltpu.CompilerParams(
            dimension_semantics=("arbitrary",), vmem_limit_bytes=VMEM_LIMIT_BYTES),
        name="mix_prompt",
    )(sinks_l, cdec, proj3, proj3, proj3, side3, side3, side3, bias, dmask, qdec, kdec, w_o_bf, h3)
    return out.reshape(BATCH * SEQ, D_MODEL), st


SAMPLE_BB = 4
SAMPLE_ROWS = SAMPLE_BB * DEC_SEQ
N_SAMPLE = DEC_BATCH * DEC_SEQ
CACHE_ROWS = WINDOW * SWA_KV_HEADS


def _mix_sample_kernel(sink_ref, cdec_ref, prow_ref, srow_ref, pall_ref, sall_ref, ck_ref, cv_ref,
                       knew_ref, vnew_ref, st_ref, bias_ref, dmask_ref, qdec_ref, kdec_ref, *rest, layer):
    cat_ref, stout_ref, ckout_ref, cvout_ref = rest[-4:]
    if layer == 0:
        for ref in (stout_ref, ckout_ref, cvout_ref):
            ref[1:] = jnp.zeros((DEPTH - 1,) + ref.shape[1:], F32)
        stout_ref, ckout_ref, cvout_ref = stout_ref.at[0], ckout_ref.at[0], cvout_ref.at[0]
    r = SAMPLE_ROWS
    row_b = lax.broadcasted_iota(jnp.int32, (r, 1), 0) // DEC_SEQ
    row_b4 = lax.broadcasted_iota(jnp.int32, (SWA_GROUP * r, 1), 0) % r // DEC_SEQ
    scale = SWA_HEAD_DIM ** -0.5
    v_off = SWA_KV_HEADS * SWA_HEAD_DIM
    new_rows = DEC_SEQ * SWA_KV_HEADS

    for bi in range(SAMPLE_BB):
        ckout_ref[bi, :CACHE_ROWS - new_rows] = ck_ref[bi, new_rows:]
        ckout_ref[bi, CACHE_ROWS - new_rows:] = knew_ref[bi]
        cvout_ref[bi, :CACHE_ROWS - new_rows] = cv_ref[bi, new_rows:]
        cvout_ref[bi, CACHE_ROWS - new_rows:] = vnew_ref[bi]

    for kh in range(SWA_KV_HEADS):
        c0 = kh * SWA_HEAD_DIM
        q4 = jnp.concatenate(
            [prow_ref[:, (kh * SWA_GROUP + gq) * SWA_HEAD_DIM:(kh * SWA_GROUP + gq + 1) * SWA_HEAD_DIM]
             for gq in range(SWA_GROUP)], axis=0)
        k_new = sall_ref[:, KV_COL + c0:KV_COL + c0 + SWA_HEAD_DIM].astype(BF16)
        v_new = sall_ref[:, KV_COL + v_off + c0:KV_COL + v_off + c0 + SWA_HEAD_DIM].astype(BF16)
        s_cache = jnp.zeros((SWA_GROUP * r, CACHE_ROWS), F32)
        for bi in range(SAMPLE_BB):
            s_cache = jnp.where(row_b4 == bi, _dot_nt(q4, ck_ref[bi].astype(BF16)), s_cache)
        s_new = _dot_nt(q4, k_new)
        s4 = jnp.concatenate([s_cache, s_new], axis=1) * scale
        p_parts, inv_parts = [], []
        for gq in range(SWA_GROUP):
            h = kh * SWA_GROUP + gq
            s = s4[gq * r:(gq + 1) * r] + bias_ref[kh, h]
            p, inv = _softmax_sink(s, sink_ref[h])
            p_parts.append(p)
            inv_parts.append(inv)
        p4 = jnp.concatenate(p_parts, axis=0)
        inv4 = jnp.concatenate(inv_parts, axis=0)
        p_cache = p4[:, :CACHE_ROWS]
        o4 = _dot(p4[:, CACHE_ROWS:].astype(BF16), v_new)
        for bi in range(SAMPLE_BB):
            o4 = o4 + _dot(jnp.where(row_b4 == bi, p_cache, 0.0).astype(BF16), cv_ref[bi].astype(BF16))
        o4 = o4 * inv4
        for gq in range(SWA_GROUP):
            h = kh * SWA_GROUP + gq
            cat_ref[:, h * SWA_HEAD_DIM:(h + 1) * SWA_HEAD_DIM] = o4[gq * r:(gq + 1) * r].astype(BF16)

    for h in range(RET_HEADS):
        cs = slice(h * RET_KEY_DIM, (h + 1) * RET_KEY_DIM)
        q = prow_ref[:, 1024 + h * RET_KEY_DIM:1024 + (h + 1) * RET_KEY_DIM]
        k_all = sall_ref[:, cs].astype(BF16)
        v_all = pall_ref[:, 2048 + h * RET_VAL_DIM:2048 + (h + 1) * RET_VAL_DIM]
        s = _dot_nt(q, k_all) * dmask_ref[h]
        o = _dot(s.astype(BF16), v_all)
        k32 = srow_ref[:, cs]
        v = prow_ref[:, 2048 + h * RET_VAL_DIM:2048 + (h + 1) * RET_VAL_DIM]
        kd = k32 * kdec_ref[:, cs]
        cross = jnp.zeros((r, RET_VAL_DIM), F32)
        for bi in range(SAMPLE_BB):
            st = st_ref[bi, h]
            cross = jnp.where(row_b == bi, _dot(q, st.astype(BF16)), cross)
            kd_b = jnp.where(row_b == bi, kd, 0.0).astype(BF16)
            stout_ref[bi, h] = cdec_ref[h] * st + _dot_tn(kd_b, v)
        o = o + cross * qdec_ref[:, cs]
        g = srow_ref[:, 1024 + h * RET_VAL_DIM:1024 + (h + 1) * RET_VAL_DIM]
        cat_ref[:, SWA_WIDTH + h * RET_VAL_DIM:SWA_WIDTH + (h + 1) * RET_VAL_DIM] = _gate_out(o, g)


def _mix_sample_call(proj, side, cache_k, cache_v, state_ret, layer, sinks_l, cdec, bias, dmask, qdec, kdec,
                     stacked):
    r = SAMPLE_ROWS
    new_rows = DEC_SEQ * SWA_KV_HEADS
    kv_w = SWA_KV_HEADS * SWA_HEAD_DIM
    k_new = side[:, KV_COL:KV_COL + kv_w].reshape(DEC_BATCH, new_rows, SWA_HEAD_DIM)
    v_new = side[:, KV_COL + kv_w:KV_COL + 2 * kv_w].reshape(DEC_BATCH, new_rows, SWA_HEAD_DIM)
    smem = pl.BlockSpec(memory_space=pltpu.SMEM)
    full = lambda shape: pl.BlockSpec(shape, lambda c: (0,) * len(shape))
    cache_in = pl.BlockSpec((None, SAMPLE_BB, CACHE_ROWS, SWA_HEAD_DIM), lambda c: (layer, c, 0, 0))
    state_dims = (SAMPLE_BB, RET_HEADS, RET_KEY_DIM, RET_VAL_DIM)
    cache_dims = (SAMPLE_BB, CACHE_ROWS, SWA_HEAD_DIM)
    if layer == 0:
        out_block = lambda dims: pl.BlockSpec((DEPTH,) + dims, lambda c: (0, c) + (0,) * (len(dims) - 1))
    else:
        out_block = lambda dims: pl.BlockSpec((None,) + dims, lambda c: (layer, c) + (0,) * (len(dims) - 1))
    new_in = pl.BlockSpec((SAMPLE_BB, new_rows, SWA_HEAD_DIM), lambda c: (c, 0, 0))
    in_specs = [
        smem, smem,
        pl.BlockSpec((r, PROJ_COLS), lambda c: (c, 0)),
        pl.BlockSpec((r, SIDE_COLS), lambda c: (c, 0)),
        full((N_SAMPLE, PROJ_COLS)),
        full((N_SAMPLE, SIDE_COLS)),
        cache_in, cache_in, new_in, new_in,
        pl.BlockSpec((None, SAMPLE_BB, RET_HEADS, RET_KEY_DIM, RET_VAL_DIM), lambda c: (layer, c, 0, 0, 0)),
        pl.BlockSpec((SWA_KV_HEADS, SWA_HEADS, r, CACHE_ROWS + N_SAMPLE), lambda c: (0, 0, c, 0)),
        pl.BlockSpec((RET_HEADS, r, N_SAMPLE), lambda c: (0, c, 0)),
        pl.BlockSpec((r, RET_WIDTH), lambda c: (c, 0)),
        pl.BlockSpec((r, RET_WIDTH), lambda c: (c, 0)),
    ]
    args = [sinks_l, cdec, proj, side, proj, side, cache_k, cache_v, k_new, v_new, state_ret, bias, dmask, qdec,
            kdec]
    aliases = {}
    if stacked is not None:
        for k, buf in enumerate(stacked):
            aliases[len(args)] = 1 + k
            in_specs.append(pl.BlockSpec(memory_space=pl.ANY))
            args.append(buf)
    cache_shape = jax.ShapeDtypeStruct((DEPTH, DEC_BATCH, CACHE_ROWS, SWA_HEAD_DIM), F32)
    cat, *new_stacked = pl.pallas_call(
        functools.partial(_mix_sample_kernel, layer=layer),
        grid=(DEC_BATCH // SAMPLE_BB,),
        in_specs=in_specs,
        out_specs=[
            pl.BlockSpec((r, D_MODEL), lambda c: (c, 0)),
            out_block(state_dims), out_block(cache_dims), out_block(cache_dims),
        ],
        out_shape=[jax.ShapeDtypeStruct((N_SAMPLE, D_MODEL), BF16),
                   jax.ShapeDtypeStruct((DEPTH, DEC_BATCH, RET_HEADS, RET_KEY_DIM, RET_VAL_DIM), F32),
                   cache_shape, cache_shape],
        input_output_aliases=aliases,
        compiler_params=pltpu.CompilerParams(
            dimension_semantics=("arbitrary",), vmem_limit_bytes=VMEM_LIMIT_BYTES),
        name="mix_sample",
    )(*args)
    return cat, tuple(new_stacked)


WO_TN = 512


def _wo_cast_kernel(cat_ref, w_ref, h_ref, out_ref, wbf_ref):
    w_bf = w_ref[...].astype(BF16)
    wbf_ref[...] = w_bf
    out_ref[...] = h_ref[...] + _dot(cat_ref[...], w_bf)


def _wo_kernel(cat_ref, w_ref, h_ref, out_ref):
    out_ref[...] = h_ref[...] + _dot(cat_ref[...], w_ref[...])


def _wo_call(cat, w_o_bf, h):
    m = h.shape[0]
    return pl.pallas_call(
        _wo_kernel,
        grid=(D_MODEL // WO_TN,),
        in_specs=[
            _resident((m, D_MODEL), lambda j: (0, 0)),
            pl.BlockSpec((D_MODEL, WO_TN), lambda j: (0, j)),
            pl.BlockSpec((m, WO_TN), lambda j: (0, j)),
        ],
        out_specs=pl.BlockSpec((m, WO_TN), lambda j: (0, j)),
        out_shape=jax.ShapeDtypeStruct((m, D_MODEL), F32),
        compiler_params=pltpu.CompilerParams(
            dimension_semantics=("arbitrary",), vmem_limit_bytes=VMEM_LIMIT_BYTES),
        name="wo",
    )(cat, w_o_bf, h)


def _wo_cast_call(cat, w_o, layer, h):
    m = h.shape[0]
    return pl.pallas_call(
        _wo_cast_kernel,
        grid=(D_MODEL // WO_TN,),
        in_specs=[
            _resident((m, D_MODEL), lambda j: (0, 0)),
            pl.BlockSpec((None, D_MODEL, WO_TN), lambda j: (layer, 0, j)),
            pl.BlockSpec((m, WO_TN), lambda j: (0, j)),
        ],
        out_specs=[pl.BlockSpec((m, WO_TN), lambda j: (0, j)),
                   pl.BlockSpec((D_MODEL, WO_TN), lambda j: (0, j))],
        out_shape=[jax.ShapeDtypeStruct((m, D_MODEL), F32),
                   jax.ShapeDtypeStruct((D_MODEL, D_MODEL), BF16)],
        compiler_params=pltpu.CompilerParams(
            dimension_semantics=("arbitrary",), vmem_limit_bytes=VMEM_LIMIT_BYTES),
        name="wo_cast",
    )(cat, w_o, h)


def _ffn_tile(f, n_f, h_ref, g_ref, wgu_ref, wd_ref, gfin_ref, out_ref, xn_ref, final_norm):
    @pl.when(f == 0)
    def _():
        x = h_ref[...]
        xn_ref[...] = (_rms_scale(x) * g_ref[...]).astype(BF16)
        out_ref[...] = x

    ab = _dot(xn_ref[...], wgu_ref[...])
    act = (_silu(ab[:, :FFN_TF]) * ab[:, FFN_TF:]).astype(BF16)
    out_ref[...] += _dot(act, wd_ref[...])

    if final_norm:
        @pl.when(f == n_f - 1)
        def _():
            out_ref[...] = _rms_scale(out_ref[...]) * gfin_ref[...]


def _ffn_kernel(h_ref, g_ref, wgu_ref, wd_ref, gfin_ref, hs_ref, out_ref, outs_ref, xn_ref, xns_ref, *,
                final_norm):
    i, f, n_f = pl.program_id(0), pl.program_id(1), pl.num_programs(1)
    _ffn_tile(f, n_f, h_ref, g_ref, wgu_ref, wd_ref, gfin_ref, out_ref, xn_ref, final_norm)

    @pl.when(i == 0)
    def _():
        _ffn_tile(f, n_f, hs_ref, g_ref, wgu_ref, wd_ref, gfin_ref, outs_ref, xns_ref, final_norm)


FFN_VMEM_LIMIT_BYTES = 62 * 1024 * 1024


def _ffn_call(h, h_sample, g, w_gu_bf, w_d_bf, g_final, layer, tm):
    m, ms = h.shape[0], h_sample.shape[0]
    return pl.pallas_call(
        functools.partial(_ffn_kernel, final_norm=(layer == DEPTH - 1)),
        grid=(m // tm, FFN_TILES),
        in_specs=[
            pl.BlockSpec((tm, D_MODEL), lambda i, f: (i, 0)),
            pl.BlockSpec((None, 1, D_MODEL), lambda i, f: (layer, 0, 0)),
            pl.BlockSpec((D_MODEL, 2 * FFN_TF), lambda i, f: (0, f)),
            pl.BlockSpec((FFN_TF, D_MODEL), lambda i, f: (f, 0)),
            pl.BlockSpec((1, D_MODEL), lambda i, f: (0, 0)),
            _resident((ms, D_MODEL), lambda i, f: (0, 0)),
        ],
        out_specs=[pl.BlockSpec((tm, D_MODEL), lambda i, f: (i, 0)),
                   _resident((ms, D_MODEL), lambda i, f: (0, 0))],
        out_shape=[jax.ShapeDtypeStruct((m, D_MODEL), F32),
                   jax.ShapeDtypeStruct((ms, D_MODEL), F32)],
        scratch_shapes=[pltpu.VMEM((tm, D_MODEL), BF16), pltpu.VMEM((ms, D_MODEL), BF16)],
        compiler_params=pltpu.CompilerParams(
            dimension_semantics=("arbitrary", "arbitrary"), vmem_limit_bytes=FFN_VMEM_LIMIT_BYTES),
        name="ffn",
    )(h, g, w_gu_bf, w_d_bf, g_final, h_sample)


def _rope_tables(pos):
    half = RET_KEY_DIM // 2
    inv = 1.0 / (ROPE_BASE ** jnp.linspace(0.0, 1.0, half, dtype=F32))
    ang = pos.astype(F32)[:, None] * inv[None, :]
    return jnp.cos(ang), jnp.sin(ang)


def _decay_tables(c):
    lg = jnp.log(1.0 - jnp.exp2(-5.0 - jnp.arange(RET_HEADS, dtype=F32)))
    idx = jnp.arange(c, dtype=F32)
    diff = idx[:, None] - idx[None, :]
    dmask = jnp.where(diff[None] >= 0, jnp.exp(jnp.maximum(diff, 0.0)[None] * lg[:, None, None]), 0.0)
    q_decay = jnp.exp((idx[:, None] + 1.0) * lg[None, :])
    k_decay = jnp.exp((c - 1.0 - idx)[:, None] * lg[None, :])
    c_decay = jnp.exp(c * lg)
    return dmask, q_decay, k_decay, c_decay


def _per_head_cols(t):
    return jnp.repeat(t, RET_KEY_DIM, axis=1)


def kernel(x_prompt, x_sample, cache_k_win, cache_v_win, state_ret, rel_bias, w_in, sinks, w_o,
           norm_mix, norm_ffn, w_gate_up, w_down, norm_final):
    norm_mix3 = norm_mix.reshape(DEPTH, 1, D_MODEL)
    norm_ffn3 = norm_ffn.reshape(DEPTH, 1, D_MODEL)
    norm_final2 = norm_final.reshape(1, D_MODEL)
    cache_k = cache_k_win.reshape(DEPTH, DEC_BATCH, CACHE_ROWS, SWA_HEAD_DIM)
    cache_v = cache_v_win.reshape(DEPTH, DEC_BATCH, CACHE_ROWS, SWA_HEAD_DIM)

    cos_p, sin_p = _rope_tables(jnp.arange(SEQ, dtype=jnp.int32))
    cos_4, sin_4 = _rope_tables(PAST_LEN + jnp.arange(DEC_SEQ, dtype=jnp.int32))
    cos_s = jnp.tile(cos_4, (DEC_BATCH, 1))
    sin_s = jnp.tile(sin_4, (DEC_BATCH, 1))

    dmask_p, qd_p, kd_p, cdec_p = _decay_tables(RET_CHUNK)
    qdec_p = _per_head_cols(qd_p)
    kdec_p = _per_head_cols(kd_p)
    dmask_4, qd_4, kd_4, cdec_s = _decay_tables(DEC_SEQ)
    eye_b = jnp.eye(DEC_BATCH, dtype=F32)
    dmask_s = jax.vmap(lambda d: jnp.kron(eye_b, d))(dmask_4)
    qdec_s = jnp.tile(_per_head_cols(qd_4), (DEC_BATCH, 1))
    kdec_s = jnp.tile(_per_head_cols(kd_4), (DEC_BATCH, 1))

    qi = np.arange(BLOCK)[:, None]
    kj = np.arange(2 * BLOCK)[None, :]
    delta_p = qi + BLOCK - kj
    in_window = (delta_p >= 0) & (delta_p < WINDOW)
    valid_p = np.stack([in_window & (kj >= BLOCK), in_window]).astype(np.float32)
    rows = np.arange(N_SAMPLE)
    rb, rt = rows // DEC_SEQ, rows % DEC_SEQ
    cache_pos = np.arange(CACHE_ROWS) // SWA_KV_HEADS
    cache_head = np.arange(CACHE_ROWS) % SWA_KV_HEADS
    delta_cache = (WINDOW + rt)[:, None] - cache_pos[None, :]
    delta_new = rt[:, None] - rt[None, :]
    same_b = rb[:, None] == rb[None, :]
    delta_s = np.concatenate([delta_cache, delta_new], axis=1)
    valid_s = np.stack([
        np.concatenate([(delta_cache < WINDOW) & (cache_head == kh)[None, :], same_b & (delta_new >= 0)], axis=1)
        for kh in range(SWA_KV_HEADS)]).astype(np.float32)
    rel_bias_t = rel_bias.T
    bias_p = _expand_bias(rel_bias_t, jnp.asarray(_t5_bucket_np(delta_p)), jnp.asarray(valid_p))
    bias_s = _expand_bias(rel_bias_t, jnp.asarray(_t5_bucket_np(delta_s)), jnp.asarray(valid_s))

    hp = x_prompt.reshape(BATCH * SEQ, D_MODEL)
    hs = x_sample.reshape(N_SAMPLE, D_MODEL)
    tm_proj, tm_ffn = 256, 1024
    kp_new, vp_new, rp_new = [], [], []
    sample_new = None
    kv_w = SWA_KV_HEADS * SWA_HEAD_DIM
    for l in range(DEPTH):
        if l == 0:
            proj_s, side_s, w_in_bf = _proj_cast_call(hs, norm_mix3, l, w_in, cos_s, sin_s)
        else:
            proj_s, side_s = _proj_call(hs, norm_mix3, l, w_in_bf, cos_s, sin_s, N_SAMPLE)
        convert = [(w_gate_up, l), (w_down, l)]
        if l + 1 < DEPTH:
            convert += [(w_in, l + 1), (w_o, l + 1)]
        proj, side, w_gu_bf, w_d_bf, *next_bf = _proj_call(hp, norm_mix3, l, w_in_bf, cos_p, sin_p, tm_proj,
                                                            convert)

        cat, sample_new = _mix_sample_call(proj_s, side_s, cache_k, cache_v, state_ret, l, sinks[l], cdec_s,
                                           bias_s, dmask_s, qdec_s, kdec_s, sample_new)
        if l == 0:
            hs, w_o_bf = _wo_cast_call(cat, w_o, l, hs)
        else:
            hs = _wo_call(cat, w_o_bf, hs)

        hp, st_p = _mix_prompt_call(proj, side, hp, w_o_bf, sinks[l], cdec_p, bias_p, dmask_p, qdec_p, kdec_p)
        hp, hs = _ffn_call(hp, hs, norm_ffn3, w_gu_bf, w_d_bf, norm_final2, l, tm_ffn)
        kv_tail = side.reshape(BATCH, SEQ, SIDE_COLS)[:, SEQ - WINDOW:, KV_COL:KV_COL + 2 * kv_w]
        kp_new.append(kv_tail[..., :kv_w].reshape(BATCH, WINDOW, SWA_KV_HEADS, SWA_HEAD_DIM))
        vp_new.append(kv_tail[..., kv_w:].reshape(BATCH, WINDOW, SWA_KV_HEADS, SWA_HEAD_DIM))
        rp_new.append(st_p)
        if next_bf:
            w_in_bf, w_o_bf = next_bf

    y_prompt = hp.reshape(BATCH, SEQ, D_MODEL)
    y_sample = hs.reshape(DEC_BATCH, DEC_SEQ, D_MODEL)
    rs_new, ks_new, vs_new = sample_new
    return (y_prompt, y_sample,
            jnp.stack(kp_new), jnp.stack(vp_new), jnp.stack(rp_new),
            ks_new.reshape(cache_k_win.shape), vs_new.reshape(cache_v_win.shape), rs_new)
```

```python
import functools
import math

import numpy as np
import jax
import jax.numpy as jnp
from jax import lax
from jax.experimental import pallas as pl
from jax.experimental.pallas import tpu as pltpu

D_MODEL = 2048
BATCH = 4
SEQ = 2048
DEPTH = 4
DEC_BATCH = 32
DEC_SEQ = 4
PAST_LEN = 16384

SWA_WIDTH = 1024
RET_WIDTH = 1024
SWA_HEADS = 8
SWA_KV_HEADS = 2
SWA_GROUP = SWA_HEADS // SWA_KV_HEADS
SWA_HEAD_DIM = 128
WINDOW = 128
BLOCK = WINDOW
RET_HEADS = 4
RET_KEY_DIM = 256
RET_VAL_DIM = 256
RET_CHUNK = 128
ROPE_BASE = 10000.0
N_BUCKETS = 32
MAX_DISTANCE = 128
EPS = 1e-6
D_FF = 5632
IN_COLS = 5632

F32 = jnp.float32
BF16 = jnp.bfloat16

VMEM_LIMIT_BYTES = 60 * 1024 * 1024

FFN_TF = 512
FFN_TILES = D_FF // FFN_TF
PROJ_TN = 512
PROJ_COLS = 3072
SIDE_COLS = 2560
QR_COL, VR_COL = 1024, 2048
G_COL, KV_COL = 1024, 2048
PROJ_TILES = (
    ("proj", 0, "plain"), ("proj", 512, "plain"),
    ("side", KV_COL, "plain"),
    ("proj", 1024, "rotary"), ("proj", 1536, "rotary"),
    ("side", 0, "rotary_k"), ("side", 512, "rotary_k"),
    ("proj", 2048, "plain"), ("proj", 2560, "plain"),
    ("side", 1024, "plain"), ("side", 1536, "plain"),
)


def _rms_scale(x):
    return x * lax.rsqrt(jnp.mean(x * x, axis=-1, keepdims=True) + EPS)


def _silu(x):
    return x * jax.nn.sigmoid(x)


def _dot(a, b):
    return jnp.dot(a, b, preferred_element_type=F32)


def _dot_nt(a, b):
    return lax.dot_general(a, b, (((1,), (1,)), ((), ())), preferred_element_type=F32)


def _dot_tn(a, b):
    return lax.dot_general(a, b, (((0,), (0,)), ((), ())), preferred_element_type=F32)


def _resident(block_shape, index_map):
    return pl.BlockSpec(block_shape, index_map, pipeline_mode=pl.Buffered(1))


MASKED = -1e30


def _bias_kernel(rbt_ref, idx_ref, valid_ref, out_ref):
    h = pl.program_id(1)
    idx = idx_ref[...]
    acc = jnp.zeros(idx.shape, F32)
    for b in range(N_BUCKETS):
        acc = jnp.where(idx == b, rbt_ref[h, b], acc)
    out_ref[...] = jnp.where(valid_ref[...] > 0.5, acc, MASKED)


def _expand_bias(rel_bias_t, bucket_idx, valid):
    nv, rows, cols = valid.shape
    return pl.pallas_call(
        _bias_kernel,
        grid=(nv, SWA_HEADS),
        in_specs=[pl.BlockSpec(memory_space=pltpu.SMEM),
                  pl.BlockSpec((rows, cols), lambda v, h: (0, 0)),
                  pl.BlockSpec((None, rows, cols), lambda v, h: (v, 0, 0))],
        out_specs=pl.BlockSpec((None, None, rows, cols), lambda v, h: (v, h, 0, 0)),
        out_shape=jax.ShapeDtypeStruct((nv, SWA_HEADS, rows, cols), F32),
        name="bias_expand",
    )(rel_bias_t, bucket_idx, valid)


def _t5_bucket_np(delta):
    n = np.maximum(delta, 0)
    max_exact = N_BUCKETS // 2
    nf = np.maximum(n, 1).astype(np.float64)
    large = max_exact + (np.log(nf / max_exact) / math.log(MAX_DISTANCE / max_exact)
                         * (N_BUCKETS - max_exact)).astype(np.int32)
    large = np.minimum(large, N_BUCKETS - 1)
    return np.where(n < max_exact, n, large).astype(np.int32)


def _proj_epilogue(acc, mode, cos_ref, sin_ref, out_ref, col):
    if mode == "plain":
        out_ref[:, col:col + PROJ_TN] = acc.astype(out_ref.dtype)
        return
    half = RET_KEY_DIM // 2
    cos = cos_ref[...]
    sin = sin_ref[...]
    for c0 in range(0, PROJ_TN, RET_KEY_DIM):
        x1 = acc[:, c0:c0 + half]
        x2 = acc[:, c0 + half:c0 + RET_KEY_DIM]
        o1 = x1 * cos - x2 * sin
        o2 = x1 * sin + x2 * cos
        if mode == "rotary_k":
            o1 = o1 * (RET_KEY_DIM ** -0.5)
            o2 = o2 * (RET_KEY_DIM ** -0.5)
        out_ref[:, col + c0:col + c0 + half] = o1.astype(out_ref.dtype)
        out_ref[:, col + c0 + half:col + c0 + RET_KEY_DIM] = o2.astype(out_ref.dtype)


def _proj_kernel(h_ref, g_ref, w_ref, cos_ref, sin_ref, *refs, n_jobs):
    src = refs[:n_jobs]
    proj_ref, side_ref = refs[n_jobs:n_jobs + 2]
    dst_refs = refs[n_jobs + 2:2 * n_jobs + 2]
    xn_ref = refs[-1]
    xn_ref[...] = (_rms_scale(h_ref[...]) * g_ref[...]).astype(BF16)
    outs = {"proj": proj_ref, "side": side_ref}
    n_t = len(PROJ_TILES)
    for t, (dst, col, mode) in enumerate(PROJ_TILES):
        acc = _dot(xn_ref[...], w_ref[:, t * PROJ_TN:(t + 1) * PROJ_TN])
        _proj_epilogue(acc, mode, cos_ref, sin_ref, outs[dst], col)
        if n_jobs >= 2:
            wgu_ref, wd_ref = src[:2]
            wgubf_ref, wdbf_ref = dst_refs[:2]
            gu_cols = wgu_ref.shape[1] // n_t
            for j in range(t * gu_cols // FFN_TF, (t + 1) * gu_cols // FFN_TF):
                c = (j % FFN_TILES) * 2 * FFN_TF + (j // FFN_TILES) * FFN_TF
                wgubf_ref[:, c:c + FFN_TF] = wgu_ref[:, j * FFN_TF:(j + 1) * FFN_TF].astype(BF16)
            d_rows = wd_ref.shape[0] // n_t
            wdbf_ref[t * d_rows:(t + 1) * d_rows] = wd_ref[t * d_rows:(t + 1) * d_rows].astype(BF16)
        if n_jobs == 4:
            win_ref, wo_ref = src[2:]
            winbf_ref, wobf_ref = dst_refs[2:]
            winbf_ref[:, t * PROJ_TN:(t + 1) * PROJ_TN] = win_ref[:, t * PROJ_TN:(t + 1) * PROJ_TN].astype(BF16)
            if t == 0:
                wobf_ref[...] = wo_ref[...].astype(BF16)


def _proj_call(h, g, layer, w_in_bf, cos, sin, tm, convert=()):
    m = h.shape[0]
    n_steps = m // tm
    cos_blocks = cos.shape[0] // tm
    n_jobs = len(convert)
    assert n_jobs in (0, 2, 4)
    slab_in, slab_out, slab_shape = [], [], []
    for w, w_layer in convert:
        rows, cols = w.shape[1] // n_steps, w.shape[2]
        assert w.shape[1] % n_steps == 0 and rows % 16 == 0
        slab_in.append(pl.BlockSpec((None, rows, cols), lambda i, w_layer=w_layer: (w_layer, i, 0)))
        slab_out.append(pl.BlockSpec((rows, cols), lambda i: (i, 0)))
        slab_shape.append(jax.ShapeDtypeStruct(w.shape[1:], BF16))
    return pl.pallas_call(
        functools.partial(_proj_kernel, n_jobs=n_jobs),
        grid=(n_steps,),
        in_specs=[
            pl.BlockSpec((tm, D_MODEL), lambda i: (i, 0)),
            _resident((None, 1, D_MODEL), lambda i: (layer, 0, 0)),
            _resident((D_MODEL, IN_COLS), lambda i: (0, 0)),
            pl.BlockSpec((tm, RET_KEY_DIM // 2), lambda i: (i % cos_blocks, 0)),
            pl.BlockSpec((tm, RET_KEY_DIM // 2), lambda i: (i % cos_blocks, 0)),
        ] + slab_in,
        out_specs=[
            pl.BlockSpec((tm, PROJ_COLS), lambda i: (i, 0)),
            pl.BlockSpec((tm, SIDE_COLS), lambda i: (i, 0)),
        ] + slab_out,
        out_shape=[jax.ShapeDtypeStruct((m, PROJ_COLS), BF16),
                   jax.ShapeDtypeStruct((m, SIDE_COLS), F32)] + slab_shape,
        scratch_shapes=[pltpu.VMEM((tm, D_MODEL), BF16)],
        compiler_params=pltpu.CompilerParams(
            dimension_semantics=("arbitrary",), vmem_limit_bytes=VMEM_LIMIT_BYTES),
        name="proj",
    )(h, g, w_in_bf, cos, sin, *[w for w, _ in convert])


def _proj_cast_kernel(h_ref, g_ref, w_ref, cos_ref, sin_ref, proj_ref, side_ref, wbf_ref, xn_ref):
    j = pl.program_id(0)

    @pl.when(j == 0)
    def _():
        xn_ref[...] = (_rms_scale(h_ref[...]) * g_ref[...]).astype(BF16)

    w_bf = w_ref[...].astype(BF16)
    wbf_ref[...] = w_bf
    acc = _dot(xn_ref[...], w_bf)
    outs = {"proj": proj_ref, "side": side_ref}
    for t, (dst, col, mode) in enumerate(PROJ_TILES):
        @pl.when(j == t)
        def _():
            _proj_epilogue(acc, mode, cos_ref, sin_ref, outs[dst], col)


def _proj_cast_call(h, g, layer, w_in, cos, sin):
    m = h.shape[0]
    full = lambda shape: _resident(shape, lambda j: (0,) * len(shape))
    return pl.pallas_call(
        _proj_cast_kernel,
        grid=(len(PROJ_TILES),),
        in_specs=[
            full((m, D_MODEL)),
            _resident((None, 1, D_MODEL), lambda j: (layer, 0, 0)),
            pl.BlockSpec((None, D_MODEL, PROJ_TN), lambda j: (layer, 0, j)),
            full((m, RET_KEY_DIM // 2)),
            full((m, RET_KEY_DIM // 2)),
        ],
        out_specs=[
            pl.BlockSpec((m, PROJ_COLS), lambda j: (0, 0)),
            pl.BlockSpec((m, SIDE_COLS), lambda j: (0, 0)),
            pl.BlockSpec((D_MODEL, PROJ_TN), lambda j: (0, j)),
        ],
        out_shape=[jax.ShapeDtypeStruct((m, PROJ_COLS), BF16),
                   jax.ShapeDtypeStruct((m, SIDE_COLS), F32),
                   jax.ShapeDtypeStruct((D_MODEL, IN_COLS), BF16)],
        scratch_shapes=[pltpu.VMEM((m, D_MODEL), BF16)],
        compiler_params=pltpu.CompilerParams(
            dimension_semantics=("arbitrary",), vmem_limit_bytes=VMEM_LIMIT_BYTES),
        name="proj_cast",
    )(h, g, w_in, cos, sin)


def _softmax_sink(s, sink):
    m = jnp.maximum(jnp.max(s, axis=-1, keepdims=True), sink)
    p = jnp.exp(s - m)
    return p, 1.0 / (jnp.sum(p, axis=-1, keepdims=True) + jnp.exp(sink - m))


def _gate_out(o, g):
    return (_rms_scale(o) * _silu(g)).astype(BF16)


WO_BATCHES = 2


def _mix_prompt_kernel(sink_ref, cdec_ref, p_ref, s_ref,
                       bias_ref, dmask_ref, qdec_ref, kdec_ref, wo_ref, h_ref, out_ref, st_ref, kvp_ref, *cat_refs):
    n = pl.program_id(0)

    @pl.when(n == 0)
    def _():
        st_ref[...] = jnp.zeros(st_ref.shape, F32)
        kvp_ref[...] = jnp.zeros(kvp_ref.shape, BF16)

    scale = SWA_HEAD_DIM ** -0.5
    v_off = SWA_KV_HEADS * SWA_HEAD_DIM

    def cat_of(b):
        return cat_refs[b // WO_BATCHES].at[b % WO_BATCHES]

    def wo_rows(b0):
        bs = slice(b0, b0 + WO_BATCHES)
        cat = cat_refs[b0 // WO_BATCHES][...].reshape(WO_BATCHES * BLOCK, D_MODEL)
        out_ref[bs] = h_ref[bs] + _dot(cat, wo_ref[...]).reshape(WO_BATCHES, BLOCK, D_MODEL)

    def attention(b, kh):
        c0 = kh * SWA_HEAD_DIM
        k_cur = s_ref[b, :, KV_COL + c0:KV_COL + c0 + SWA_HEAD_DIM].astype(BF16)
        v_cur = s_ref[b, :, KV_COL + v_off + c0:KV_COL + v_off + c0 + SWA_HEAD_DIM].astype(BF16)
        k_cat = jnp.concatenate([kvp_ref[b, :, c0:c0 + SWA_HEAD_DIM], k_cur], axis=0)
        v_cat = jnp.concatenate([kvp_ref[b, :, v_off + c0:v_off + c0 + SWA_HEAD_DIM], v_cur], axis=0)
        kvp_ref[b, :, c0:c0 + SWA_HEAD_DIM] = k_cur
        kvp_ref[b, :, v_off + c0:v_off + c0 + SWA_HEAD_DIM] = v_cur
        for gq in range(SWA_GROUP):
            h = kh * SWA_GROUP + gq
            q = p_ref[b, :, h * SWA_HEAD_DIM:(h + 1) * SWA_HEAD_DIM]
            s = _dot_nt(q, k_cat) * scale + bias_ref[h]
            p, inv = _softmax_sink(s, sink_ref[h])
            o = _dot(p.astype(BF16), v_cat) * inv
            cat_of(b)[:, h * SWA_HEAD_DIM:(h + 1) * SWA_HEAD_DIM] = o.astype(BF16)

    def retention(b, h):
        cs = slice(h * RET_KEY_DIM, (h + 1) * RET_KEY_DIM)
        q = p_ref[b, :, QR_COL + h * RET_KEY_DIM:QR_COL + (h + 1) * RET_KEY_DIM]
        k32 = s_ref[b, :, cs]
        v = p_ref[b, :, VR_COL + h * RET_VAL_DIM:VR_COL + (h + 1) * RET_VAL_DIM]
        s = _dot_nt(q, k32.astype(BF16)) * dmask_ref[h]
        o = _dot(s.astype(BF16), v)
        st = st_ref[b, h]
        o = o + _dot(q, st.astype(BF16)) * qdec_ref[:, cs]
        kd = (k32 * kdec_ref[:, cs]).astype(BF16)
        st_ref[b, h] = cdec_ref[h] * st + _dot_tn(kd, v)
        cat_of(b)[:, SWA_WIDTH + h * RET_VAL_DIM:SWA_WIDTH + (h + 1) * RET_VAL_DIM] = (
            _gate_out(o, s_ref[b, :, G_COL + h * RET_VAL_DIM:G_COL + (h + 1) * RET_VAL_DIM]))

    for b in range(BATCH):
        for kh in range(SWA_KV_HEADS):
            attention(b, kh)
        for h in range(RET_HEADS):
            retention(b, h)
        if (b + 1) % WO_BATCHES == 0:
            wo_rows(b + 1 - WO_BATCHES)


def _mix_prompt_call(proj, side, h, w_o_bf, sinks_l, cdec, bias, dmask, qdec, kdec):
    nblk = SEQ // BLOCK
    proj3 = proj.reshape(BATCH, SEQ, PROJ_COLS)
    side3 = side.reshape(BATCH, SEQ, SIDE_COLS)
    h3 = h.reshape(BATCH, SEQ, D_MODEL)
    smem = pl.BlockSpec(memory_space=pltpu.SMEM)
    full = lambda shape: _resident(shape, lambda n: (0,) * len(shape))
    cur = lambda n: n
    prv = lambda n: n
    out, st = pl.pallas_call(
        _mix_prompt_kernel,
        grid=(nblk,),
        in_specs=[
            smem, smem,
            pl.BlockSpec((BATCH, BLOCK, PROJ_COLS), lambda n: (0, cur(n), 0)),
            pl.BlockSpec((BATCH, BLOCK, SIDE_COLS), lambda n: (0, cur(n), 0)),
            pl.BlockSpec((None, SWA_HEADS, BLOCK, 2 * BLOCK), lambda n: (jnp.minimum(n, 1), 0, 0, 0)),
            full((RET_HEADS, RET_CHUNK, RET_CHUNK)),
            full((RET_CHUNK, RET_WIDTH)),
            full((RET_CHUNK, RET_WIDTH)),
            full((D_MODEL, D_MODEL)),
            pl.BlockSpec((BATCH, BLOCK, D_MODEL), lambda n: (0, prv(n), 0)),
        ],
        out_specs=[
            pl.BlockSpec((BATCH, BLOCK, D_MODEL), lambda n: (0, prv(n), 0)),
            _resident((BATCH, RET_HEADS, RET_KEY_DIM, RET_VAL_DIM), lambda n: (0, 0, 0, 0)),
        ],
        out_shape=[jax.ShapeDtypeStruct((BATCH, SEQ, D_MODEL), F32),
                   jax.ShapeDtypeStruct((BATCH, RET_HEADS, RET_KEY_DIM, RET_VAL_DIM), F32)],
        scratch_shapes=[pltpu.VMEM((BATCH, BLOCK, 2 * SWA_KV_HEADS * SWA_HEAD_DIM), BF16)]
        + [pltpu.VMEM((WO_BATCHES, BLOCK, D_MODEL), BF16)] * (BATCH // WO_BATCHES),
        compiler_params=pltpu.CompilerParams(
            dimension_semantics=("arbitrary",), vmem_limit_bytes=VMEM_LIMIT_BYTES),
        name="mix_prompt",
    )(sinks_l, cdec, proj3, side3, bias, dmask, qdec, kdec, w_o_bf, h3)
    return out.reshape(BATCH * SEQ, D_MODEL), st


SAMPLE_BB = 4
SAMPLE_ROWS = SAMPLE_BB * DEC_SEQ
N_SAMPLE = DEC_BATCH * DEC_SEQ
CACHE_ROWS = WINDOW * SWA_KV_HEADS


def _mix_sample_kernel(sink_ref, cdec_ref, prow_ref, srow_ref, pall_ref, sall_ref, ck_ref, cv_ref,
                       knew_ref, vnew_ref, st_ref, bias_ref, dmask_ref, qdec_ref, kdec_ref, *rest, layer):
    cat_ref, stout_ref, ckout_ref, cvout_ref = rest[-4:]
    if layer == 0:
        for ref in (stout_ref, ckout_ref, cvout_ref):
            ref[1:] = jnp.zeros((DEPTH - 1,) + ref.shape[1:], F32)
        stout_ref, ckout_ref, cvout_ref = stout_ref.at[0], ckout_ref.at[0], cvout_ref.at[0]
    r = SAMPLE_ROWS
    row_b = lax.broadcasted_iota(jnp.int32, (r, 1), 0) // DEC_SEQ
    row_b4 = lax.broadcasted_iota(jnp.int32, (SWA_GROUP * r, 1), 0) % r // DEC_SEQ
    scale = SWA_HEAD_DIM ** -0.5
    v_off = SWA_KV_HEADS * SWA_HEAD_DIM
    new_rows = DEC_SEQ * SWA_KV_HEADS

    for bi in range(SAMPLE_BB):
        ckout_ref[bi, :CACHE_ROWS - new_rows] = ck_ref[bi, new_rows:]
        ckout_ref[bi, CACHE_ROWS - new_rows:] = knew_ref[bi]
        cvout_ref[bi, :CACHE_ROWS - new_rows] = cv_ref[bi, new_rows:]
        cvout_ref[bi, CACHE_ROWS - new_rows:] = vnew_ref[bi]

    for kh in range(SWA_KV_HEADS):
        c0 = kh * SWA_HEAD_DIM
        q4 = jnp.concatenate(
            [prow_ref[:, (kh * SWA_GROUP + gq) * SWA_HEAD_DIM:(kh * SWA_GROUP + gq + 1) * SWA_HEAD_DIM]
             for gq in range(SWA_GROUP)], axis=0)
        k_new = sall_ref[:, KV_COL + c0:KV_COL + c0 + SWA_HEAD_DIM].astype(BF16)
        v_new = sall_ref[:, KV_COL + v_off + c0:KV_COL + v_off + c0 + SWA_HEAD_DIM].astype(BF16)
        s_cache = jnp.zeros((SWA_GROUP * r, CACHE_ROWS), F32)
        for bi in range(SAMPLE_BB):
            s_cache = jnp.where(row_b4 == bi, _dot_nt(q4, ck_ref[bi].astype(BF16)), s_cache)
        s_new = _dot_nt(q4, k_new)
        s4 = jnp.concatenate([s_cache, s_new], axis=1) * scale
        p_parts, inv_parts = [], []
        for gq in range(SWA_GROUP):
            h = kh * SWA_GROUP + gq
            s = s4[gq * r:(gq + 1) * r] + bias_ref[kh, h]
            p, inv = _softmax_sink(s, sink_ref[h])
            p_parts.append(p)
            inv_parts.append(inv)
        p4 = jnp.concatenate(p_parts, axis=0)
        inv4 = jnp.concatenate(inv_parts, axis=0)
        p_cache = p4[:, :CACHE_ROWS]
        o4 = _dot(p4[:, CACHE_ROWS:].astype(BF16), v_new)
        for bi in range(SAMPLE_BB):
            o4 = o4 + _dot(jnp.where(row_b4 == bi, p_cache, 0.0).astype(BF16), cv_ref[bi].astype(BF16))
        o4 = o4 * inv4
        for gq in range(SWA_GROUP):
            h = kh * SWA_GROUP + gq
            cat_ref[:, h * SWA_HEAD_DIM:(h + 1) * SWA_HEAD_DIM] = o4[gq * r:(gq + 1) * r].astype(BF16)

    for h in range(RET_HEADS):
        cs = slice(h * RET_KEY_DIM, (h + 1) * RET_KEY_DIM)
        q = prow_ref[:, 1024 + h * RET_KEY_DIM:1024 + (h + 1) * RET_KEY_DIM]
        k_all = sall_ref[:, cs].astype(BF16)
        v_all = pall_ref[:, 2048 + h * RET_VAL_DIM:2048 + (h + 1) * RET_VAL_DIM]
        s = _dot_nt(q, k_all) * dmask_ref[h]
        o = _dot(s.astype(BF16), v_all)
        k32 = srow_ref[:, cs]
        v = prow_ref[:, 2048 + h * RET_VAL_DIM:2048 + (h + 1) * RET_VAL_DIM]
        kd = k32 * kdec_ref[:, cs]
        cross = jnp.zeros((r, RET_VAL_DIM), F32)
        for bi in range(SAMPLE_BB):
            st = st_ref[bi, h]
            cross = jnp.where(row_b == bi, _dot(q, st.astype(BF16)), cross)
            kd_b = jnp.where(row_b == bi, kd, 0.0).astype(BF16)
            stout_ref[bi, h] = cdec_ref[h] * st + _dot_tn(kd_b, v)
        o = o + cross * qdec_ref[:, cs]
        g = srow_ref[:, 1024 + h * RET_VAL_DIM:1024 + (h + 1) * RET_VAL_DIM]
        cat_ref[:, SWA_WIDTH + h * RET_VAL_DIM:SWA_WIDTH + (h + 1) * RET_VAL_DIM] = _gate_out(o, g)


def _mix_sample_call(proj, side, cache_k, cache_v, state_ret, layer, sinks_l, cdec, bias, dmask, qdec, kdec,
                     stacked):
    r = SAMPLE_ROWS
    new_rows = DEC_SEQ * SWA_KV_HEADS
    kv_w = SWA_KV_HEADS * SWA_HEAD_DIM
    k_new = side[:, KV_COL:KV_COL + kv_w].reshape(DEC_BATCH, new_rows, SWA_HEAD_DIM)
    v_new = side[:, KV_COL + kv_w:KV_COL + 2 * kv_w].reshape(DEC_BATCH, new_rows, SWA_HEAD_DIM)
    smem = pl.BlockSpec(memory_space=pltpu.SMEM)
    full = lambda shape: pl.BlockSpec(shape, lambda c: (0,) * len(shape))
    cache_in = pl.BlockSpec((None, SAMPLE_BB, CACHE_ROWS, SWA_HEAD_DIM), lambda c: (layer, c, 0, 0))
    state_dims = (SAMPLE_BB, RET_HEADS, RET_KEY_DIM, RET_VAL_DIM)
    cache_dims = (SAMPLE_BB, CACHE_ROWS, SWA_HEAD_DIM)
    if layer == 0:
        out_block = lambda dims: pl.BlockSpec((DEPTH,) + dims, lambda c: (0, c) + (0,) * (len(dims) - 1))
    else:
        out_block = lambda dims: pl.BlockSpec((None,) + dims, lambda c: (layer, c) + (0,) * (len(dims) - 1))
    new_in = pl.BlockSpec((SAMPLE_BB, new_rows, SWA_HEAD_DIM), lambda c: (c, 0, 0))
    in_specs = [
        smem, smem,
        pl.BlockSpec((r, PROJ_COLS), lambda c: (c, 0)),
        pl.BlockSpec((r, SIDE_COLS), lambda c: (c, 0)),
        full((N_SAMPLE, PROJ_COLS)),
        full((N_SAMPLE, SIDE_COLS)),
        cache_in, cache_in, new_in, new_in,
        pl.BlockSpec((None, SAMPLE_BB, RET_HEADS, RET_KEY_DIM, RET_VAL_DIM), lambda c: (layer, c, 0, 0, 0)),
        pl.BlockSpec((SWA_KV_HEADS, SWA_HEADS, r, CACHE_ROWS + N_SAMPLE), lambda c: (0, 0, c, 0)),
        pl.BlockSpec((RET_HEADS, r, N_SAMPLE), lambda c: (0, c, 0)),
        pl.BlockSpec((r, RET_WIDTH), lambda c: (c, 0)),
        pl.BlockSpec((r, RET_WIDTH), lambda c: (c, 0)),
    ]
    args = [sinks_l, cdec, proj, side, proj, side, cache_k, cache_v, k_new, v_new, state_ret, bias, dmask, qdec,
            kdec]
    aliases = {}
    if stacked is not None:
        for k, buf in enumerate(stacked):
            aliases[len(args)] = 1 + k
            in_specs.append(pl.BlockSpec(memory_space=pl.ANY))
            args.append(buf)
    cache_shape = jax.ShapeDtypeStruct((DEPTH, DEC_BATCH, CACHE_ROWS, SWA_HEAD_DIM), F32)
    cat, *new_stacked = pl.pallas_call(
        functools.partial(_mix_sample_kernel, layer=layer),
        grid=(DEC_BATCH // SAMPLE_BB,),
        in_specs=in_specs,
        out_specs=[
            pl.BlockSpec((r, D_MODEL), lambda c: (c, 0)),
            out_block(state_dims), out_block(cache_dims), out_block(cache_dims),
        ],
        out_shape=[jax.ShapeDtypeStruct((N_SAMPLE, D_MODEL), BF16),
                   jax.ShapeDtypeStruct((DEPTH, DEC_BATCH, RET_HEADS, RET_KEY_DIM, RET_VAL_DIM), F32),
                   cache_shape, cache_shape],
        input_output_aliases=aliases,
        compiler_params=pltpu.CompilerParams(
            dimension_semantics=("arbitrary",), vmem_limit_bytes=VMEM_LIMIT_BYTES),
        name="mix_sample",
    )(*args)
    return cat, tuple(new_stacked)


WO_TN = 512


def _wo_cast_kernel(cat_ref, w_ref, h_ref, out_ref, wbf_ref):
    w_bf = w_ref[...].astype(BF16)
    wbf_ref[...] = w_bf
    out_ref[...] = h_ref[...] + _dot(cat_ref[...], w_bf)


def _wo_kernel(cat_ref, w_ref, h_ref, out_ref):
    out_ref[...] = h_ref[...] + _dot(cat_ref[...], w_ref[...])


def _wo_call(cat, w_o_bf, h):
    m = h.shape[0]
    return pl.pallas_call(
        _wo_kernel,
        grid=(D_MODEL // WO_TN,),
        in_specs=[
            _resident((m, D_MODEL), lambda j: (0, 0)),
            pl.BlockSpec((D_MODEL, WO_TN), lambda j: (0, j)),
            pl.BlockSpec((m, WO_TN), lambda j: (0, j)),
        ],
        out_specs=pl.BlockSpec((m, WO_TN), lambda j: (0, j)),
        out_shape=jax.ShapeDtypeStruct((m, D_MODEL), F32),
        compiler_params=pltpu.CompilerParams(
            dimension_semantics=("arbitrary",), vmem_limit_bytes=VMEM_LIMIT_BYTES),
        name="wo",
    )(cat, w_o_bf, h)


def _wo_cast_call(cat, w_o, layer, h):
    m = h.shape[0]
    return pl.pallas_call(
        _wo_cast_kernel,
        grid=(D_MODEL // WO_TN,),
        in_specs=[
            _resident((m, D_MODEL), lambda j: (0, 0)),
            pl.BlockSpec((None, D_MODEL, WO_TN), lambda j: (layer, 0, j)),
            pl.BlockSpec((m, WO_TN), lambda j: (0, j)),
        ],
        out_specs=[pl.BlockSpec((m, WO_TN), lambda j: (0, j)),
                   pl.BlockSpec((D_MODEL, WO_TN), lambda j: (0, j))],
        out_shape=[jax.ShapeDtypeStruct((m, D_MODEL), F32),
                   jax.ShapeDtypeStruct((D_MODEL, D_MODEL), BF16)],
        compiler_params=pltpu.CompilerParams(
            dimension_semantics=("arbitrary",), vmem_limit_bytes=VMEM_LIMIT_BYTES),
        name="wo_cast",
    )(cat, w_o, h)


def _ffn_tile(f, n_f, h_ref, g_ref, wgu_ref, wd_ref, gfin_ref, out_ref, xn_ref, final_norm):
    @pl.when(f == 0)
    def _():
        x = h_ref[...]
        xn_ref[...] = (_rms_scale(x) * g_ref[...]).astype(BF16)
        out_ref[...] = x

    ab = _dot(xn_ref[...], wgu_ref[...])
    act = (_silu(ab[:, :FFN_TF]) * ab[:, FFN_TF:]).astype(BF16)
    out_ref[...] += _dot(act, wd_ref[...])

    if final_norm:
        @pl.when(f == n_f - 1)
        def _():
            out_ref[...] = _rms_scale(out_ref[...]) * gfin_ref[...]


def _ffn_kernel(h_ref, g_ref, wgu_ref, wd_ref, gfin_ref, hs_ref, out_ref, outs_ref, xn_ref, xns_ref, *,
                final_norm):
    i, f, n_f = pl.program_id(0), pl.program_id(1), pl.num_programs(1)
    _ffn_tile(f, n_f, h_ref, g_ref, wgu_ref, wd_ref, gfin_ref, out_ref, xn_ref, final_norm)

    @pl.when(i == 0)
    def _():
        _ffn_tile(f, n_f, hs_ref, g_ref, wgu_ref, wd_ref, gfin_ref, outs_ref, xns_ref, final_norm)


FFN_VMEM_LIMIT_BYTES = 62 * 1024 * 1024


def _ffn_call(h, h_sample, g, w_gu_bf, w_d_bf, g_final, layer, tm):
    m, ms = h.shape[0], h_sample.shape[0]
    return pl.pallas_call(
        functools.partial(_ffn_kernel, final_norm=(layer == DEPTH - 1)),
        grid=(m // tm, FFN_TILES),
        in_specs=[
            pl.BlockSpec((tm, D_MODEL), lambda i, f: (i, 0)),
            pl.BlockSpec((None, 1, D_MODEL), lambda i, f: (layer, 0, 0)),
            pl.BlockSpec((D_MODEL, 2 * FFN_TF), lambda i, f: (0, f)),
            pl.BlockSpec((FFN_TF, D_MODEL), lambda i, f: (f, 0)),
            pl.BlockSpec((1, D_MODEL), lambda i, f: (0, 0)),
            _resident((ms, D_MODEL), lambda i, f: (0, 0)),
        ],
        out_specs=[pl.BlockSpec((tm, D_MODEL), lambda i, f: (i, 0)),
                   _resident((ms, D_MODEL), lambda i, f: (0, 0))],
        out_shape=[jax.ShapeDtypeStruct((m, D_MODEL), F32),
                   jax.ShapeDtypeStruct((ms, D_MODEL), F32)],
        scratch_shapes=[pltpu.VMEM((tm, D_MODEL), BF16), pltpu.VMEM((ms, D_MODEL), BF16)],
        compiler_params=pltpu.CompilerParams(
            dimension_semantics=("arbitrary", "arbitrary"), vmem_limit_bytes=FFN_VMEM_LIMIT_BYTES),
        name="ffn",
    )(h, g, w_gu_bf, w_d_bf, g_final, h_sample)


def _rope_tables(pos):
    half = RET_KEY_DIM // 2
    inv = 1.0 / (ROPE_BASE ** jnp.linspace(0.0, 1.0, half, dtype=F32))
    ang = pos.astype(F32)[:, None] * inv[None, :]
    return jnp.cos(ang), jnp.sin(ang)


def _decay_tables(c):
    lg = jnp.log(1.0 - jnp.exp2(-5.0 - jnp.arange(RET_HEADS, dtype=F32)))
    idx = jnp.arange(c, dtype=F32)
    diff = idx[:, None] - idx[None, :]
    dmask = jnp.where(diff[None] >= 0, jnp.exp(jnp.maximum(diff, 0.0)[None] * lg[:, None, None]), 0.0)
    q_decay = jnp.exp((idx[:, None] + 1.0) * lg[None, :])
    k_decay = jnp.exp((c - 1.0 - idx)[:, None] * lg[None, :])
    c_decay = jnp.exp(c * lg)
    return dmask, q_decay, k_decay, c_decay


def _per_head_cols(t):
    return jnp.repeat(t, RET_KEY_DIM, axis=1)


def kernel(x_prompt, x_sample, cache_k_win, cache_v_win, state_ret, rel_bias, w_in, sinks, w_o,
           norm_mix, norm_ffn, w_gate_up, w_down, norm_final):
    norm_mix3 = norm_mix.reshape(DEPTH, 1, D_MODEL)
    norm_ffn3 = norm_ffn.reshape(DEPTH, 1, D_MODEL)
    norm_final2 = norm_final.reshape(1, D_MODEL)
    cache_k = cache_k_win.reshape(DEPTH, DEC_BATCH, CACHE_ROWS, SWA_HEAD_DIM)
    cache_v = cache_v_win.reshape(DEPTH, DEC_BATCH, CACHE_ROWS, SWA_HEAD_DIM)

    cos_p, sin_p = _rope_tables(jnp.arange(SEQ, dtype=jnp.int32))
    cos_4, sin_4 = _rope_tables(PAST_LEN + jnp.arange(DEC_SEQ, dtype=jnp.int32))
    cos_s = jnp.tile(cos_4, (DEC_BATCH, 1))
    sin_s = jnp.tile(sin_4, (DEC_BATCH, 1))

    dmask_p, qd_p, kd_p, cdec_p = _decay_tables(RET_CHUNK)
    qdec_p = _per_head_cols(qd_p)
    kdec_p = _per_head_cols(kd_p)
    dmask_4, qd_4, kd_4, cdec_s = _decay_tables(DEC_SEQ)
    eye_b = jnp.eye(DEC_BATCH, dtype=F32)
    dmask_s = jax.vmap(lambda d: jnp.kron(eye_b, d))(dmask_4)
    qdec_s = jnp.tile(_per_head_cols(qd_4), (DEC_BATCH, 1))
    kdec_s = jnp.tile(_per_head_cols(kd_4), (DEC_BATCH, 1))

    qi = np.arange(BLOCK)[:, None]
    kj = np.arange(2 * BLOCK)[None, :]
    delta_p = qi + BLOCK - kj
    in_window = (delta_p >= 0) & (delta_p < WINDOW)
    valid_p = np.stack([in_window & (kj >= BLOCK), in_window]).astype(np.float32)
    rows = np.arange(N_SAMPLE)
    rb, rt = rows // DEC_SEQ, rows % DEC_SEQ
    cache_pos = np.arange(CACHE_ROWS) // SWA_KV_HEADS
    cache_head = np.arange(CACHE_ROWS) % SWA_KV_HEADS
    delta_cache = (WINDOW + rt)[:, None] - cache_pos[None, :]
    delta_new = rt[:, None] - rt[None, :]
    same_b = rb[:, None] == rb[None, :]
    delta_s = np.concatenate([delta_cache, delta_new], axis=1)
    valid_s = np.stack([
        np.concatenate([(delta_cache < WINDOW) & (cache_head == kh)[None, :], same_b & (delta_new >= 0)], axis=1)
        for kh in range(SWA_KV_HEADS)]).astype(np.float32)
    rel_bias_t = rel_bias.T
    bias_p = _expand_bias(rel_bias_t, jnp.asarray(_t5_bucket_np(delta_p)), jnp.asarray(valid_p))
    bias_s = _expand_bias(rel_bias_t, jnp.asarray(_t5_bucket_np(delta_s)), jnp.asarray(valid_s))

    hp = x_prompt.reshape(BATCH * SEQ, D_MODEL)
    hs = x_sample.reshape(N_SAMPLE, D_MODEL)
    tm_proj, tm_ffn = 256, 1024
    kp_new, vp_new, rp_new = [], [], []
    sample_new = None
    kv_w = SWA_KV_HEADS * SWA_HEAD_DIM
    for l in range(DEPTH):
        if l == 0:
            proj_s, side_s, w_in_bf = _proj_cast_call(hs, norm_mix3, l, w_in, cos_s, sin_s)
        else:
            proj_s, side_s = _proj_call(hs, norm_mix3, l, w_in_bf, cos_s, sin_s, N_SAMPLE)
        convert = [(w_gate_up, l), (w_down, l)]
        if l + 1 < DEPTH:
            convert += [(w_in, l + 1), (w_o, l + 1)]
        proj, side, w_gu_bf, w_d_bf, *next_bf = _proj_call(hp, norm_mix3, l, w_in_bf, cos_p, sin_p, tm_proj,
                                                            convert)

        cat, sample_new = _mix_sample_call(proj_s, side_s, cache_k, cache_v, state_ret, l, sinks[l], cdec_s,
                                           bias_s, dmask_s, qdec_s, kdec_s, sample_new)
        if l == 0:
            hs, w_o_bf = _wo_cast_call(cat, w_o, l, hs)
        else:
            hs = _wo_call(cat, w_o_bf, hs)

        hp, st_p = _mix_prompt_call(proj, side, hp, w_o_bf, sinks[l], cdec_p, bias_p, dmask_p, qdec_p, kdec_p)
        hp, hs = _ffn_call(hp, hs, norm_ffn3, w_gu_bf, w_d_bf, norm_final2, l, tm_ffn)
        kv_tail = side.reshape(BATCH, SEQ, SIDE_COLS)[:, SEQ - WINDOW:, KV_COL:KV_COL + 2 * kv_w]
        kp_new.append(kv_tail[..., :kv_w].reshape(BATCH, WINDOW, SWA_KV_HEADS, SWA_HEAD_DIM))
        vp_new.append(kv_tail[..., kv_w:].reshape(BATCH, WINDOW, SWA_KV_HEADS, SWA_HEAD_DIM))
        rp_new.append(st_p)
        if next_bf:
            w_in_bf, w_o_bf = next_bf

    y_prompt = hp.reshape(BATCH, SEQ, D_MODEL)
    y_sample = hs.reshape(DEC_BATCH, DEC_SEQ, D_MODEL)
    rs_new, ks_new, vs_new = sample_new
    return (y_prompt, y_sample,
            jnp.stack(kp_new), jnp.stack(vp_new), jnp.stack(rp_new),
            ks_new.reshape(cache_k_win.shape), vs_new.reshape(cache_v_win.shape), rs_new)
```

```python
import functools
import math

import numpy as np
import jax
import jax.numpy as jnp
from jax import lax
from jax.experimental import pallas as pl
from jax.experimental.pallas import tpu as pltpu

D_MODEL = 2048
BATCH = 4
SEQ = 2048
DEPTH = 4
DEC_BATCH = 32
DEC_SEQ = 4
PAST_LEN = 16384

SWA_WIDTH = 1024
RET_WIDTH = 1024
SWA_HEADS = 8
SWA_KV_HEADS = 2
SWA_GROUP = SWA_HEADS // SWA_KV_HEADS
SWA_HEAD_DIM = 128
WINDOW = 128
BLOCK = WINDOW
RET_HEADS = 4
RET_KEY_DIM = 256
RET_VAL_DIM = 256
RET_CHUNK = 128
ROPE_BASE = 10000.0
N_BUCKETS = 32
MAX_DISTANCE = 128
EPS = 1e-6
D_FF = 5632
IN_COLS = 5632

F32 = jnp.float32
BF16 = jnp.bfloat16

VMEM_LIMIT_BYTES = 60 * 1024 * 1024

FFN_TF = 512
FFN_TILES = D_FF // FFN_TF
PROJ_TN = 512
PROJ_COLS = 3072
SIDE_COLS = 2560
QR_COL, VR_COL = 1024, 2048
G_COL, KV_COL = 1024, 2048
PROJ_TILES = (
    ("proj", 0, "plain"), ("proj", 512, "plain"),
    ("side", KV_COL, "plain"),
    ("proj", 1024, "rotary"), ("proj", 1536, "rotary"),
    ("side", 0, "rotary_k"), ("side", 512, "rotary_k"),
    ("proj", 2048, "plain"), ("proj", 2560, "plain"),
    ("side", 1024, "plain"), ("side", 1536, "plain"),
)


def _rms_scale(x):
    return x * lax.rsqrt(jnp.mean(x * x, axis=-1, keepdims=True) + EPS)


def _silu(x):
    return x * jax.nn.sigmoid(x)


def _dot(a, b):
    return jnp.dot(a, b, preferred_element_type=F32)


def _dot_nt(a, b):
    return lax.dot_general(a, b, (((1,), (1,)), ((), ())), preferred_element_type=F32)


def _dot_tn(a, b):
    return lax.dot_general(a, b, (((0,), (0,)), ((), ())), preferred_element_type=F32)


def _resident(block_shape, index_map):
    return pl.BlockSpec(block_shape, index_map, pipeline_mode=pl.Buffered(1))


MASKED = -1e30


def _bias_kernel(rbt_ref, idx_ref, valid_ref, out_ref):
    h = pl.program_id(1)
    idx = idx_ref[...]
    acc = jnp.zeros(idx.shape, F32)
    for b in range(N_BUCKETS):
        acc = jnp.where(idx == b, rbt_ref[h, b], acc)
    out_ref[...] = jnp.where(valid_ref[...] > 0.5, acc, MASKED)


def _expand_bias(rel_bias_t, bucket_idx, valid):
    nv, rows, cols = valid.shape
    return pl.pallas_call(
        _bias_kernel,
        grid=(nv, SWA_HEADS),
        in_specs=[pl.BlockSpec(memory_space=pltpu.SMEM),
                  pl.BlockSpec((rows, cols), lambda v, h: (0, 0)),
                  pl.BlockSpec((None, rows, cols), lambda v, h: (v, 0, 0))],
        out_specs=pl.BlockSpec((None, None, rows, cols), lambda v, h: (v, h, 0, 0)),
        out_shape=jax.ShapeDtypeStruct((nv, SWA_HEADS, rows, cols), F32),
        name="bias_expand",
    )(rel_bias_t, bucket_idx, valid)


def _t5_bucket_np(delta):
    n = np.maximum(delta, 0)
    max_exact = N_BUCKETS // 2
    nf = np.maximum(n, 1).astype(np.float64)
    large = max_exact + (np.log(nf / max_exact) / math.log(MAX_DISTANCE / max_exact)
                         * (N_BUCKETS - max_exact)).astype(np.int32)
    large = np.minimum(large, N_BUCKETS - 1)
    return np.where(n < max_exact, n, large).astype(np.int32)


def _proj_epilogue(acc, mode, cos_ref, sin_ref, out_ref, col):
    if mode == "plain":
        out_ref[:, col:col + PROJ_TN] = acc.astype(out_ref.dtype)
        return
    half = RET_KEY_DIM // 2
    cos = cos_ref[...]
    sin = sin_ref[...]
    for c0 in range(0, PROJ_TN, RET_KEY_DIM):
        x1 = acc[:, c0:c0 + half]
        x2 = acc[:, c0 + half:c0 + RET_KEY_DIM]
        o1 = x1 * cos - x2 * sin
        o2 = x1 * sin + x2 * cos
        if mode == "rotary_k":
            o1 = o1 * (RET_KEY_DIM ** -0.5)
            o2 = o2 * (RET_KEY_DIM ** -0.5)
        out_ref[:, col + c0:col + c0 + half] = o1.astype(out_ref.dtype)
        out_ref[:, col + c0 + half:col + c0 + RET_KEY_DIM] = o2.astype(out_ref.dtype)


def _proj_rows(h_ref, g_ref, w_ref, cos_ref, sin_ref, proj_ref, side_ref, xn_ref):
    xn_ref[...] = (_rms_scale(h_ref[...]) * g_ref[...]).astype(BF16)
    outs = {"proj": proj_ref, "side": side_ref}
    for t, (dst, col, mode) in enumerate(PROJ_TILES):
        acc = _dot(xn_ref[...], w_ref[:, t * PROJ_TN:(t + 1) * PROJ_TN])
        _proj_epilogue(acc, mode, cos_ref, sin_ref, outs[dst], col)


def _proj_kernel(h_ref, g_ref, w_ref, cos_ref, sin_ref, *refs, n_jobs, with_sample):
    refs = list(refs)
    sample_in = [refs.pop(0) for _ in range(3)] if with_sample else []
    src = [refs.pop(0) for _ in range(n_jobs)]
    proj_ref, side_ref = refs.pop(0), refs.pop(0)
    sample_out = [refs.pop(0) for _ in range(2)] if with_sample else []
    dst_refs = [refs.pop(0) for _ in range(n_jobs)]
    xn_ref = refs.pop(0)

    if with_sample:
        @pl.when(pl.program_id(0) == 0)
        def _():
            hs_ref, coss_ref, sins_ref = sample_in
            _proj_rows(hs_ref, g_ref, w_ref, coss_ref, sins_ref, *sample_out, refs[0])

    xn_ref[...] = (_rms_scale(h_ref[...]) * g_ref[...]).astype(BF16)
    outs = {"proj": proj_ref, "side": side_ref}
    n_t = len(PROJ_TILES)
    for t, (dst, col, mode) in enumerate(PROJ_TILES):
        acc = _dot(xn_ref[...], w_ref[:, t * PROJ_TN:(t + 1) * PROJ_TN])
        _proj_epilogue(acc, mode, cos_ref, sin_ref, outs[dst], col)
        if n_jobs >= 2:
            wgu_ref, wd_ref = src[:2]
            wgubf_ref, wdbf_ref = dst_refs[:2]
            gu_cols = wgu_ref.shape[1] // n_t
            for j in range(t * gu_cols // FFN_TF, (t + 1) * gu_cols // FFN_TF):
                c = (j % FFN_TILES) * 2 * FFN_TF + (j // FFN_TILES) * FFN_TF
                wgubf_ref[:, c:c + FFN_TF] = wgu_ref[:, j * FFN_TF:(j + 1) * FFN_TF].astype(BF16)
            d_rows = wd_ref.shape[0] // n_t
            wdbf_ref[t * d_rows:(t + 1) * d_rows] = wd_ref[t * d_rows:(t + 1) * d_rows].astype(BF16)
        if n_jobs == 4:
            win_ref, wo_ref = src[2:]
            winbf_ref, wobf_ref = dst_refs[2:]
            winbf_ref[:, t * PROJ_TN:(t + 1) * PROJ_TN] = win_ref[:, t * PROJ_TN:(t + 1) * PROJ_TN].astype(BF16)
            if t == 0:
                wobf_ref[...] = wo_ref[...].astype(BF16)


def _proj_call(h, g, layer, w_in_bf, cos, sin, tm, convert=(), sample=None):
    m = h.shape[0]
    n_steps = m // tm
    cos_blocks = cos.shape[0] // tm
    n_jobs = len(convert)
    assert n_jobs in (0, 2, 4)
    sample_in, sample_out, sample_shape, sample_scratch = [], [], [], []
    if sample is not None:
        ms = sample[0].shape[0]
        sample_in = [_resident(a.shape, lambda i: (0, 0)) for a in sample]
        sample_out = [_resident((ms, PROJ_COLS), lambda i: (0, 0)), _resident((ms, SIDE_COLS), lambda i: (0, 0))]
        sample_shape = [jax.ShapeDtypeStruct((ms, PROJ_COLS), BF16), jax.ShapeDtypeStruct((ms, SIDE_COLS), F32)]
        sample_scratch = [pltpu.VMEM((ms, D_MODEL), BF16)]
    slab_in, slab_out, slab_shape = [], [], []
    for w, w_layer in convert:
        rows, cols = w.shape[1] // n_steps, w.shape[2]
        assert w.shape[1] % n_steps == 0 and rows % 16 == 0
        slab_in.append(pl.BlockSpec((None, rows, cols), lambda i, w_layer=w_layer: (w_layer, i, 0)))
        slab_out.append(pl.BlockSpec((rows, cols), lambda i: (i, 0)))
        slab_shape.append(jax.ShapeDtypeStruct(w.shape[1:], BF16))
    return pl.pallas_call(
        functools.partial(_proj_kernel, n_jobs=n_jobs, with_sample=sample is not None),
        grid=(n_steps,),
        in_specs=[
            pl.BlockSpec((tm, D_MODEL), lambda i: (i, 0)),
            _resident((None, 1, D_MODEL), lambda i: (layer, 0, 0)),
            _resident((D_MODEL, IN_COLS), lambda i: (0, 0)),
            pl.BlockSpec((tm, RET_KEY_DIM // 2), lambda i: (i % cos_blocks, 0)),
            pl.BlockSpec((tm, RET_KEY_DIM // 2), lambda i: (i % cos_blocks, 0)),
        ] + sample_in + slab_in,
        out_specs=[
            pl.BlockSpec((tm, PROJ_COLS), lambda i: (i, 0)),
            pl.BlockSpec((tm, SIDE_COLS), lambda i: (i, 0)),
        ] + sample_out + slab_out,
        out_shape=[jax.ShapeDtypeStruct((m, PROJ_COLS), BF16),
                   jax.ShapeDtypeStruct((m, SIDE_COLS), F32)] + sample_shape + slab_shape,
        scratch_shapes=[pltpu.VMEM((tm, D_MODEL), BF16)] + sample_scratch,
        compiler_params=pltpu.CompilerParams(
            dimension_semantics=("arbitrary",), vmem_limit_bytes=VMEM_LIMIT_BYTES),
        name="proj",
    )(h, g, w_in_bf, cos, sin, *(sample or ()), *[w for w, _ in convert])


def _proj_cast_kernel(h_ref, g_ref, w_ref, cos_ref, sin_ref, proj_ref, side_ref, wbf_ref, xn_ref):
    j = pl.program_id(0)

    @pl.when(j == 0)
    def _():
        xn_ref[...] = (_rms_scale(h_ref[...]) * g_ref[...]).astype(BF16)

    w_bf = w_ref[...].astype(BF16)
    wbf_ref[...] = w_bf
    acc = _dot(xn_ref[...], w_bf)
    outs = {"proj": proj_ref, "side": side_ref}
    for t, (dst, col, mode) in enumerate(PROJ_TILES):
        @pl.when(j == t)
        def _():
            _proj_epilogue(acc, mode, cos_ref, sin_ref, outs[dst], col)


def _proj_cast_call(h, g, layer, w_in, cos, sin):
    m = h.shape[0]
    full = lambda shape: _resident(shape, lambda j: (0,) * len(shape))
    return pl.pallas_call(
        _proj_cast_kernel,
        grid=(len(PROJ_TILES),),
        in_specs=[
            full((m, D_MODEL)),
            _resident((None, 1, D_MODEL), lambda j: (layer, 0, 0)),
            pl.BlockSpec((None, D_MODEL, PROJ_TN), lambda j: (layer, 0, j)),
            full((m, RET_KEY_DIM // 2)),
            full((m, RET_KEY_DIM // 2)),
        ],
        out_specs=[
            pl.BlockSpec((m, PROJ_COLS), lambda j: (0, 0)),
            pl.BlockSpec((m, SIDE_COLS), lambda j: (0, 0)),
            pl.BlockSpec((D_MODEL, PROJ_TN), lambda j: (0, j)),
        ],
        out_shape=[jax.ShapeDtypeStruct((m, PROJ_COLS), BF16),
                   jax.ShapeDtypeStruct((m, SIDE_COLS), F32),
                   jax.ShapeDtypeStruct((D_MODEL, IN_COLS), BF16)],
        scratch_shapes=[pltpu.VMEM((m, D_MODEL), BF16)],
        compiler_params=pltpu.CompilerParams(
            dimension_semantics=("arbitrary",), vmem_limit_bytes=VMEM_LIMIT_BYTES),
        name="proj_cast",
    )(h, g, w_in, cos, sin)


def _softmax_sink(s, sink):
    m = jnp.maximum(jnp.max(s, axis=-1, keepdims=True), sink)
    p = jnp.exp(s - m)
    return p, 1.0 / (jnp.sum(p, axis=-1, keepdims=True) + jnp.exp(sink - m))


def _gate_out(o, g):
    return (_rms_scale(o) * _silu(g)).astype(BF16)


WO_BATCHES = 2


def _mix_prompt_kernel(sink_ref, cdec_ref, p_ref, s_ref,
                       bias_ref, dmask_ref, qdec_ref, kdec_ref, wo_ref, h_ref, *refs, with_sample):
    refs = list(refs)
    sample_in = [refs.pop(0) for _ in range(2)] if with_sample else []
    out_ref, st_ref = refs.pop(0), refs.pop(0)
    sample_out = [refs.pop(0)] if with_sample else []
    kvp_ref, cat_refs = refs[0], refs[1:]
    n = pl.program_id(0)

    @pl.when(n == 0)
    def _():
        st_ref[...] = jnp.zeros(st_ref.shape, F32)
        kvp_ref[...] = jnp.zeros(kvp_ref.shape, BF16)
        if with_sample:
            cats_ref, hs_ref = sample_in
            sample_out[0][...] = hs_ref[...] + _dot(cats_ref[...], wo_ref[...])

    scale = SWA_HEAD_DIM ** -0.5
    v_off = SWA_KV_HEADS * SWA_HEAD_DIM

    def cat_of(b):
        return cat_refs[b // WO_BATCHES].at[b % WO_BATCHES]

    def wo_rows(b0):
        bs = slice(b0, b0 + WO_BATCHES)
        cat = cat_refs[b0 // WO_BATCHES][...].reshape(WO_BATCHES * BLOCK, D_MODEL)
        out_ref[bs] = h_ref[bs] + _dot(cat, wo_ref[...]).reshape(WO_BATCHES, BLOCK, D_MODEL)

    def attention(b, kh):
        c0 = kh * SWA_HEAD_DIM
        k_cur = s_ref[b, :, KV_COL + c0:KV_COL + c0 + SWA_HEAD_DIM].astype(BF16)
        v_cur = s_ref[b, :, KV_COL + v_off + c0:KV_COL + v_off + c0 + SWA_HEAD_DIM].astype(BF16)
        k_cat = jnp.concatenate([kvp_ref[b, :, c0:c0 + SWA_HEAD_DIM], k_cur], axis=0)
        v_cat = jnp.concatenate([kvp_ref[b, :, v_off + c0:v_off + c0 + SWA_HEAD_DIM], v_cur], axis=0)
        kvp_ref[b, :, c0:c0 + SWA_HEAD_DIM] = k_cur
        kvp_ref[b, :, v_off + c0:v_off + c0 + SWA_HEAD_DIM] = v_cur
        for gq in range(SWA_GROUP):
            h = kh * SWA_GROUP + gq
            q = p_ref[b, :, h * SWA_HEAD_DIM:(h + 1) * SWA_HEAD_DIM]
            s = _dot_nt(q, k_cat) * scale + bias_ref[h]
            p, inv = _softmax_sink(s, sink_ref[h])
            o = _dot(p.astype(BF16), v_cat) * inv
            cat_of(b)[:, h * SWA_HEAD_DIM:(h + 1) * SWA_HEAD_DIM] = o.astype(BF16)

    def retention(b, h):
        cs = slice(h * RET_KEY_DIM, (h + 1) * RET_KEY_DIM)
        q = p_ref[b, :, QR_COL + h * RET_KEY_DIM:QR_COL + (h + 1) * RET_KEY_DIM]
        k32 = s_ref[b, :, cs]
        v = p_ref[b, :, VR_COL + h * RET_VAL_DIM:VR_COL + (h + 1) * RET_VAL_DIM]
        s = _dot_nt(q, k32.astype(BF16)) * dmask_ref[h]
        o = _dot(s.astype(BF16), v)
        st = st_ref[b, h]
        o = o + _dot(q, st.astype(BF16)) * qdec_ref[:, cs]
        kd = (k32 * kdec_ref[:, cs]).astype(BF16)
        st_ref[b, h] = cdec_ref[h] * st + _dot_tn(kd, v)
        cat_of(b)[:, SWA_WIDTH + h * RET_VAL_DIM:SWA_WIDTH + (h + 1) * RET_VAL_DIM] = (
            _gate_out(o, s_ref[b, :, G_COL + h * RET_VAL_DIM:G_COL + (h + 1) * RET_VAL_DIM]))

    for b in range(BATCH):
        for kh in range(SWA_KV_HEADS):
            attention(b, kh)
        for h in range(RET_HEADS):
            retention(b, h)
        if (b + 1) % WO_BATCHES == 0:
            wo_rows(b + 1 - WO_BATCHES)


def _mix_prompt_call(proj, side, h, w_o_bf, sinks_l, cdec, bias, dmask, qdec, kdec, sample=None):
    nblk = SEQ // BLOCK
    proj3 = proj.reshape(BATCH, SEQ, PROJ_COLS)
    side3 = side.reshape(BATCH, SEQ, SIDE_COLS)
    h3 = h.reshape(BATCH, SEQ, D_MODEL)
    smem = pl.BlockSpec(memory_space=pltpu.SMEM)
    full = lambda shape: _resident(shape, lambda n: (0,) * len(shape))
    sample_in, sample_out, sample_shape = [], [], []
    if sample is not None:
        sample_in = [full(a.shape) for a in sample]
        sample_out = [full(sample[1].shape)]
        sample_shape = [jax.ShapeDtypeStruct(sample[1].shape, F32)]
    out, st, *hs_new = pl.pallas_call(
        functools.partial(_mix_prompt_kernel, with_sample=sample is not None),
        grid=(nblk,),
        in_specs=[
            smem, smem,
            pl.BlockSpec((BATCH, BLOCK, PROJ_COLS), lambda n: (0, n, 0)),
            pl.BlockSpec((BATCH, BLOCK, SIDE_COLS), lambda n: (0, n, 0)),
            pl.BlockSpec((None, SWA_HEADS, BLOCK, 2 * BLOCK), lambda n: (jnp.minimum(n, 1), 0, 0, 0)),
            full((RET_HEADS, RET_CHUNK, RET_CHUNK)),
            full((RET_CHUNK, RET_WIDTH)),
            full((RET_CHUNK, RET_WIDTH)),
            full((D_MODEL, D_MODEL)),
            pl.BlockSpec((BATCH, BLOCK, D_MODEL), lambda n: (0, n, 0)),
        ] + sample_in,
        out_specs=[
            pl.BlockSpec((BATCH, BLOCK, D_MODEL), lambda n: (0, n, 0)),
            _resident((BATCH, RET_HEADS, RET_KEY_DIM, RET_VAL_DIM), lambda n: (0, 0, 0, 0)),
        ] + sample_out,
        out_shape=[jax.ShapeDtypeStruct((BATCH, SEQ, D_MODEL), F32),
                   jax.ShapeDtypeStruct((BATCH, RET_HEADS, RET_KEY_DIM, RET_VAL_DIM), F32)] + sample_shape,
        scratch_shapes=[pltpu.VMEM((BATCH, BLOCK, 2 * SWA_KV_HEADS * SWA_HEAD_DIM), BF16)]
        + [pltpu.VMEM((WO_BATCHES, BLOCK, D_MODEL), BF16)] * (BATCH // WO_BATCHES),
        compiler_params=pltpu.CompilerParams(
            dimension_semantics=("arbitrary",), vmem_limit_bytes=VMEM_LIMIT_BYTES),
        name="mix_prompt",
    )(sinks_l, cdec, proj3, side3, bias, dmask, qdec, kdec, w_o_bf, h3, *(sample or ()))
    return (out.reshape(BATCH * SEQ, D_MODEL), st, *hs_new)


SAMPLE_BB = 4
SAMPLE_ROWS = SAMPLE_BB * DEC_SEQ
N_SAMPLE = DEC_BATCH * DEC_SEQ
CACHE_ROWS = WINDOW * SWA_KV_HEADS


def _mix_sample_kernel(sink_ref, cdec_ref, prow_ref, srow_ref, pall_ref, sall_ref, ck_ref, cv_ref,
                       knew_ref, vnew_ref, st_ref, bias_ref, dmask_ref, qdec_ref, kdec_ref, *rest, layer):
    cat_ref, stout_ref, ckout_ref, cvout_ref = rest[-4:]
    if layer == 0:
        for ref in (stout_ref, ckout_ref, cvout_ref):
            ref[1:] = jnp.zeros((DEPTH - 1,) + ref.shape[1:], F32)
        stout_ref, ckout_ref, cvout_ref = stout_ref.at[0], ckout_ref.at[0], cvout_ref.at[0]
    r = SAMPLE_ROWS
    row_b = lax.broadcasted_iota(jnp.int32, (r, 1), 0) // DEC_SEQ
    row_b4 = lax.broadcasted_iota(jnp.int32, (SWA_GROUP * r, 1), 0) % r // DEC_SEQ
    scale = SWA_HEAD_DIM ** -0.5
    v_off = SWA_KV_HEADS * SWA_HEAD_DIM
    new_rows = DEC_SEQ * SWA_KV_HEADS

    for bi in range(SAMPLE_BB):
        ckout_ref[bi, :CACHE_ROWS - new_rows] = ck_ref[bi, new_rows:]
        ckout_ref[bi, CACHE_ROWS - new_rows:] = knew_ref[bi]
        cvout_ref[bi, :CACHE_ROWS - new_rows] = cv_ref[bi, new_rows:]
        cvout_ref[bi, CACHE_ROWS - new_rows:] = vnew_ref[bi]

    for kh in range(SWA_KV_HEADS):
        c0 = kh * SWA_HEAD_DIM
        q4 = jnp.concatenate(
            [prow_ref[:, (kh * SWA_GROUP + gq) * SWA_HEAD_DIM:(kh * SWA_GROUP + gq + 1) * SWA_HEAD_DIM]
             for gq in range(SWA_GROUP)], axis=0)
        k_new = sall_ref[:, KV_COL + c0:KV_COL + c0 + SWA_HEAD_DIM].astype(BF16)
        v_new = sall_ref[:, KV_COL + v_off + c0:KV_COL + v_off + c0 + SWA_HEAD_DIM].astype(BF16)
        s_cache = jnp.zeros((SWA_GROUP * r, CACHE_ROWS), F32)
        for bi in range(SAMPLE_BB):
            s_cache = jnp.where(row_b4 == bi, _dot_nt(q4, ck_ref[bi].astype(BF16)), s_cache)
        s_new = _dot_nt(q4, k_new)
        s4 = jnp.concatenate([s_cache, s_new], axis=1) * scale
        p_parts, inv_parts = [], []
        for gq in range(SWA_GROUP):
            h = kh * SWA_GROUP + gq
            s = s4[gq * r:(gq + 1) * r] + bias_ref[kh, h]
            p, inv = _softmax_sink(s, sink_ref[h])
            p_parts.append(p)
            inv_parts.append(inv)
        p4 = jnp.concatenate(p_parts, axis=0)
        inv4 = jnp.concatenate(inv_parts, axis=0)
        p_cache = p4[:, :CACHE_ROWS]
        o4 = _dot(p4[:, CACHE_ROWS:].astype(BF16), v_new)
        for bi in range(SAMPLE_BB):
            o4 = o4 + _dot(jnp.where(row_b4 == bi, p_cache, 0.0).astype(BF16), cv_ref[bi].astype(BF16))
        o4 = o4 * inv4
        for gq in range(SWA_GROUP):
            h = kh * SWA_GROUP + gq
            cat_ref[:, h * SWA_HEAD_DIM:(h + 1) * SWA_HEAD_DIM] = o4[gq * r:(gq + 1) * r].astype(BF16)

    for h in range(RET_HEADS):
        cs = slice(h * RET_KEY_DIM, (h + 1) * RET_KEY_DIM)
        q = prow_ref[:, 1024 + h * RET_KEY_DIM:1024 + (h + 1) * RET_KEY_DIM]
        k_all = sall_ref[:, cs].astype(BF16)
        v_all = pall_ref[:, 2048 + h * RET_VAL_DIM:2048 + (h + 1) * RET_VAL_DIM]
        s = _dot_nt(q, k_all) * dmask_ref[h]
        o = _dot(s.astype(BF16), v_all)
        k32 = srow_ref[:, cs]
        v = prow_ref[:, 2048 + h * RET_VAL_DIM:2048 + (h + 1) * RET_VAL_DIM]
        kd = k32 * kdec_ref[:, cs]
        cross = jnp.zeros((r, RET_VAL_DIM), F32)
        for bi in range(SAMPLE_BB):
            st = st_ref[bi, h]
            cross = jnp.where(row_b == bi, _dot(q, st.astype(BF16)), cross)
            kd_b = jnp.where(row_b == bi, kd, 0.0).astype(BF16)
            stout_ref[bi, h] = cdec_ref[h] * st + _dot_tn(kd_b, v)
        o = o + cross * qdec_ref[:, cs]
        g = srow_ref[:, 1024 + h * RET_VAL_DIM:1024 + (h + 1) * RET_VAL_DIM]
        cat_ref[:, SWA_WIDTH + h * RET_VAL_DIM:SWA_WIDTH + (h + 1) * RET_VAL_DIM] = _gate_out(o, g)


def _mix_sample_call(proj, side, cache_k, cache_v, state_ret, layer, sinks_l, cdec, bias, dmask, qdec, kdec,
                     stacked):
    r = SAMPLE_ROWS
    new_rows = DEC_SEQ * SWA_KV_HEADS
    kv_w = SWA_KV_HEADS * SWA_HEAD_DIM
    k_new = side[:, KV_COL:KV_COL + kv_w].reshape(DEC_BATCH, new_rows, SWA_HEAD_DIM)
    v_new = side[:, KV_COL + kv_w:KV_COL + 2 * kv_w].reshape(DEC_BATCH, new_rows, SWA_HEAD_DIM)
    smem = pl.BlockSpec(memory_space=pltpu.SMEM)
    full = lambda shape: pl.BlockSpec(shape, lambda c: (0,) * len(shape))
    cache_in = pl.BlockSpec((None, SAMPLE_BB, CACHE_ROWS, SWA_HEAD_DIM), lambda c: (layer, c, 0, 0))
    state_dims = (SAMPLE_BB, RET_HEADS, RET_KEY_DIM, RET_VAL_DIM)
    cache_dims = (SAMPLE_BB, CACHE_ROWS, SWA_HEAD_DIM)
    if layer == 0:
        out_block = lambda dims: pl.BlockSpec((DEPTH,) + dims, lambda c: (0, c) + (0,) * (len(dims) - 1))
    else:
        out_block = lambda dims: pl.BlockSpec((None,) + dims, lambda c: (layer, c) + (0,) * (len(dims) - 1))
    new_in = pl.BlockSpec((SAMPLE_BB, new_rows, SWA_HEAD_DIM), lambda c: (c, 0, 0))
    in_specs = [
        smem, smem,
        pl.BlockSpec((r, PROJ_COLS), lambda c: (c, 0)),
        pl.BlockSpec((r, SIDE_COLS), lambda c: (c, 0)),
        full((N_SAMPLE, PROJ_COLS)),
        full((N_SAMPLE, SIDE_COLS)),
        cache_in, cache_in, new_in, new_in,
        pl.BlockSpec((None, SAMPLE_BB, RET_HEADS, RET_KEY_DIM, RET_VAL_DIM), lambda c: (layer, c, 0, 0, 0)),
        pl.BlockSpec((SWA_KV_HEADS, SWA_HEADS, r, CACHE_ROWS + N_SAMPLE), lambda c: (0, 0, c, 0)),
        pl.BlockSpec((RET_HEADS, r, N_SAMPLE), lambda c: (0, c, 0)),
        pl.BlockSpec((r, RET_WIDTH), lambda c: (c, 0)),
        pl.BlockSpec((r, RET_WIDTH), lambda c: (c, 0)),
    ]
    args = [sinks_l, cdec, proj, side, proj, side, cache_k, cache_v, k_new, v_new, state_ret, bias, dmask, qdec,
            kdec]
    aliases = {}
    if stacked is not None:
        for k, buf in enumerate(stacked):
            aliases[len(args)] = 1 + k
            in_specs.append(pl.BlockSpec(memory_space=pl.ANY))
            args.append(buf)
    cache_shape = jax.ShapeDtypeStruct((DEPTH, DEC_BATCH, CACHE_ROWS, SWA_HEAD_DIM), F32)
    cat, *new_stacked = pl.pallas_call(
        functools.partial(_mix_sample_kernel, layer=layer),
        grid=(DEC_BATCH // SAMPLE_BB,),
        in_specs=in_specs,
        out_specs=[
            pl.BlockSpec((r, D_MODEL), lambda c: (c, 0)),
            out_block(state_dims), out_block(cache_dims), out_block(cache_dims),
        ],
        out_shape=[jax.ShapeDtypeStruct((N_SAMPLE, D_MODEL), BF16),
                   jax.ShapeDtypeStruct((DEPTH, DEC_BATCH, RET_HEADS, RET_KEY_DIM, RET_VAL_DIM), F32),
                   cache_shape, cache_shape],
        input_output_aliases=aliases,
        compiler_params=pltpu.CompilerParams(
            dimension_semantics=("arbitrary",), vmem_limit_bytes=VMEM_LIMIT_BYTES),
        name="mix_sample",
    )(*args)
    return cat, tuple(new_stacked)


WO_TN = 512


def _wo_cast_kernel(cat_ref, w_ref, h_ref, out_ref, wbf_ref):
    w_bf = w_ref[...].astype(BF16)
    wbf_ref[...] = w_bf
    out_ref[...] = h_ref[...] + _dot(cat_ref[...], w_bf)


def _wo_cast_call(cat, w_o, layer, h):
    m = h.shape[0]
    return pl.pallas_call(
        _wo_cast_kernel,
        grid=(D_MODEL // WO_TN,),
        in_specs=[
            _resident((m, D_MODEL), lambda j: (0, 0)),
            pl.BlockSpec((None, D_MODEL, WO_TN), lambda j: (layer, 0, j)),
            pl.BlockSpec((m, WO_TN), lambda j: (0, j)),
        ],
        out_specs=[pl.BlockSpec((m, WO_TN), lambda j: (0, j)),
                   pl.BlockSpec((D_MODEL, WO_TN), lambda j: (0, j))],
        out_shape=[jax.ShapeDtypeStruct((m, D_MODEL), F32),
                   jax.ShapeDtypeStruct((D_MODEL, D_MODEL), BF16)],
        compiler_params=pltpu.CompilerParams(
            dimension_semantics=("arbitrary",), vmem_limit_bytes=VMEM_LIMIT_BYTES),
        name="wo_cast",
    )(cat, w_o, h)


def _ffn_tile(f, n_f, h_ref, g_ref, wgu_ref, wd_ref, gfin_ref, out_ref, xn_ref, final_norm):
    @pl.when(f == 0)
    def _():
        x = h_ref[...]
        xn_ref[...] = (_rms_scale(x) * g_ref[...]).astype(BF16)
        out_ref[...] = x

    ab = _dot(xn_ref[...], wgu_ref[...])
    act = (_silu(ab[:, :FFN_TF]) * ab[:, FFN_TF:]).astype(BF16)
    out_ref[...] += _dot(act, wd_ref[...])

    if final_norm:
        @pl.when(f == n_f - 1)
        def _():
            out_ref[...] = _rms_scale(out_ref[...]) * gfin_ref[...]


def _ffn_kernel(h_ref, g_ref, wgu_ref, wd_ref, gfin_ref, hs_ref, out_ref, outs_ref, xn_ref, xns_ref, *,
                final_norm):
    i, f, n_f = pl.program_id(0), pl.program_id(1), pl.num_programs(1)
    _ffn_tile(f, n_f, h_ref, g_ref, wgu_ref, wd_ref, gfin_ref, out_ref, xn_ref, final_norm)

    @pl.when(i == 0)
    def _():
        _ffn_tile(f, n_f, hs_ref, g_ref, wgu_ref, wd_ref, gfin_ref, outs_ref, xns_ref, final_norm)


FFN_VMEM_LIMIT_BYTES = 62 * 1024 * 1024


def _ffn_call(h, h_sample, g, w_gu_bf, w_d_bf, g_final, layer, tm):
    m, ms = h.shape[0], h_sample.shape[0]
    return pl.pallas_call(
        functools.partial(_ffn_kernel, final_norm=(layer == DEPTH - 1)),
        grid=(m // tm, FFN_TILES),
        in_specs=[
            pl.BlockSpec((tm, D_MODEL), lambda i, f: (i, 0)),
            pl.BlockSpec((None, 1, D_MODEL), lambda i, f: (layer, 0, 0)),
            pl.BlockSpec((D_MODEL, 2 * FFN_TF), lambda i, f: (0, f)),
            pl.BlockSpec((FFN_TF, D_MODEL), lambda i, f: (f, 0)),
            pl.BlockSpec((1, D_MODEL), lambda i, f: (0, 0)),
            _resident((ms, D_MODEL), lambda i, f: (0, 0)),
        ],
        out_specs=[pl.BlockSpec((tm, D_MODEL), lambda i, f: (i, 0)),
                   _resident((ms, D_MODEL), lambda i, f: (0, 0))],
        out_shape=[jax.ShapeDtypeStruct((m, D_MODEL), F32),
                   jax.ShapeDtypeStruct((ms, D_MODEL), F32)],
        scratch_shapes=[pltpu.VMEM((tm, D_MODEL), BF16), pltpu.VMEM((ms, D_MODEL), BF16)],
        compiler_params=pltpu.CompilerParams(
            dimension_semantics=("arbitrary", "arbitrary"), vmem_limit_bytes=FFN_VMEM_LIMIT_BYTES),
        name="ffn",
    )(h, g, w_gu_bf, w_d_bf, g_final, h_sample)


def _rope_tables(pos):
    half = RET_KEY_DIM // 2
    inv = 1.0 / (ROPE_BASE ** jnp.linspace(0.0, 1.0, half, dtype=F32))
    ang = pos.astype(F32)[:, None] * inv[None, :]
    return jnp.cos(ang), jnp.sin(ang)


def _decay_tables(c):
    lg = jnp.log(1.0 - jnp.exp2(-5.0 - jnp.arange(RET_HEADS, dtype=F32)))
    idx = jnp.arange(c, dtype=F32)
    diff = idx[:, None] - idx[None, :]
    dmask = jnp.where(diff[None] >= 0, jnp.exp(jnp.maximum(diff, 0.0)[None] * lg[:, None, None]), 0.0)
    q_decay = jnp.exp((idx[:, None] + 1.0) * lg[None, :])
    k_decay = jnp.exp((c - 1.0 - idx)[:, None] * lg[None, :])
    c_decay = jnp.exp(c * lg)
    return dmask, q_decay, k_decay, c_decay


def _per_head_cols(t):
    return jnp.repeat(t, RET_KEY_DIM, axis=1)


def kernel(x_prompt, x_sample, cache_k_win, cache_v_win, state_ret, rel_bias, w_in, sinks, w_o,
           norm_mix, norm_ffn, w_gate_up, w_down, norm_final):
    norm_mix3 = norm_mix.reshape(DEPTH, 1, D_MODEL)
    norm_ffn3 = norm_ffn.reshape(DEPTH, 1, D_MODEL)
    norm_final2 = norm_final.reshape(1, D_MODEL)
    cache_k = cache_k_win.reshape(DEPTH, DEC_BATCH, CACHE_ROWS, SWA_HEAD_DIM)
    cache_v = cache_v_win.reshape(DEPTH, DEC_BATCH, CACHE_ROWS, SWA_HEAD_DIM)

    cos_p, sin_p = _rope_tables(jnp.arange(SEQ, dtype=jnp.int32))
    cos_4, sin_4 = _rope_tables(PAST_LEN + jnp.arange(DEC_SEQ, dtype=jnp.int32))
    cos_s = jnp.tile(cos_4, (DEC_BATCH, 1))
    sin_s = jnp.tile(sin_4, (DEC_BATCH, 1))

    dmask_p, qd_p, kd_p, cdec_p = _decay_tables(RET_CHUNK)
    qdec_p = _per_head_cols(qd_p)
    kdec_p = _per_head_cols(kd_p)
    dmask_4, qd_4, kd_4, cdec_s = _decay_tables(DEC_SEQ)
    eye_b = jnp.eye(DEC_BATCH, dtype=F32)
    dmask_s = jax.vmap(lambda d: jnp.kron(eye_b, d))(dmask_4)
    qdec_s = jnp.tile(_per_head_cols(qd_4), (DEC_BATCH, 1))
    kdec_s = jnp.tile(_per_head_cols(kd_4), (DEC_BATCH, 1))

    qi = np.arange(BLOCK)[:, None]
    kj = np.arange(2 * BLOCK)[None, :]
    delta_p = qi + BLOCK - kj
    in_window = (delta_p >= 0) & (delta_p < WINDOW)
    valid_p = np.stack([in_window & (kj >= BLOCK), in_window]).astype(np.float32)
    rows = np.arange(N_SAMPLE)
    rb, rt = rows // DEC_SEQ, rows % DEC_SEQ
    cache_pos = np.arange(CACHE_ROWS) // SWA_KV_HEADS
    cache_head = np.arange(CACHE_ROWS) % SWA_KV_HEADS
    delta_cache = (WINDOW + rt)[:, None] - cache_pos[None, :]
    delta_new = rt[:, None] - rt[None, :]
    same_b = rb[:, None] == rb[None, :]
    delta_s = np.concatenate([delta_cache, delta_new], axis=1)
    valid_s = np.stack([
        np.concatenate([(delta_cache < WINDOW) & (cache_head == kh)[None, :], same_b & (delta_new >= 0)], axis=1)
        for kh in range(SWA_KV_HEADS)]).astype(np.float32)
    rel_bias_t = rel_bias.T
    bias_p = _expand_bias(rel_bias_t, jnp.asarray(_t5_bucket_np(delta_p)), jnp.asarray(valid_p))
    bias_s = _expand_bias(rel_bias_t, jnp.asarray(_t5_bucket_np(delta_s)), jnp.asarray(valid_s))

    hp = x_prompt.reshape(BATCH * SEQ, D_MODEL)
    hs = x_sample.reshape(N_SAMPLE, D_MODEL)
    tm_proj, tm_ffn = 256, 1024
    kp_new, vp_new, rp_new = [], [], []
    sample_new = None
    kv_w = SWA_KV_HEADS * SWA_HEAD_DIM
    for l in range(DEPTH):
        convert = [(w_gate_up, l), (w_down, l)]
        if l + 1 < DEPTH:
            convert += [(w_in, l + 1), (w_o, l + 1)]
        if l == 0:
            proj_s, side_s, w_in_bf = _proj_cast_call(hs, norm_mix3, l, w_in, cos_s, sin_s)
            proj, side, w_gu_bf, w_d_bf, *next_bf = _proj_call(hp, norm_mix3, l, w_in_bf, cos_p, sin_p, tm_proj,
                                                                convert)
        else:
            proj, side, proj_s, side_s, w_gu_bf, w_d_bf, *next_bf = _proj_call(
                hp, norm_mix3, l, w_in_bf, cos_p, sin_p, tm_proj, convert, sample=(hs, cos_s, sin_s))

        cat, sample_new = _mix_sample_call(proj_s, side_s, cache_k, cache_v, state_ret, l, sinks[l], cdec_s,
                                           bias_s, dmask_s, qdec_s, kdec_s, sample_new)
        if l == 0:
            hs, w_o_bf = _wo_cast_call(cat, w_o, l, hs)
            hp, st_p = _mix_prompt_call(proj, side, hp, w_o_bf, sinks[l], cdec_p, bias_p, dmask_p, qdec_p, kdec_p)
        else:
            hp, st_p, hs = _mix_prompt_call(proj, side, hp, w_o_bf, sinks[l], cdec_p, bias_p, dmask_p, qdec_p,
                                            kdec_p, sample=(cat, hs))
        hp, hs = _ffn_call(hp, hs, norm_ffn3, w_gu_bf, w_d_bf, norm_final2, l, tm_ffn)
        kv_tail = side.reshape(BATCH, SEQ, SIDE_COLS)[:, SEQ - WINDOW:, KV_COL:KV_COL + 2 * kv_w]
        kp_new.append(kv_tail[..., :kv_w].reshape(BATCH, WINDOW, SWA_KV_HEADS, SWA_HEAD_DIM))
        vp_new.append(kv_tail[..., kv_w:].reshape(BATCH, WINDOW, SWA_KV_HEADS, SWA_HEAD_DIM))
        rp_new.append(st_p)
        if next_bf:
            w_in_bf, w_o_bf = next_bf

    y_prompt = hp.reshape(BATCH, SEQ, D_MODEL)
    y_sample = hs.reshape(DEC_BATCH, DEC_SEQ, D_MODEL)
    rs_new, ks_new, vs_new = sample_new
    return (y_prompt, y_sample,
            jnp.stack(kp_new), jnp.stack(vp_new), jnp.stack(rp_new),
            ks_new.reshape(cache_k_win.shape), vs_new.reshape(cache_v_win.shape), rs_new)
```

```python
import functools
import math

import numpy as np
import jax
import jax.numpy as jnp
from jax import lax
from jax.experimental import pallas as pl
from jax.experimental.pallas import tpu as pltpu

D_MODEL = 2048
BATCH = 4
SEQ = 2048
DEPTH = 4
DEC_BATCH = 32
DEC_SEQ = 4
PAST_LEN = 16384

SWA_WIDTH = 1024
RET_WIDTH = 1024
SWA_HEADS = 8
SWA_KV_HEADS = 2
SWA_GROUP = SWA_HEADS // SWA_KV_HEADS
SWA_HEAD_DIM = 128
WINDOW = 128
BLOCK = WINDOW
RET_HEADS = 4
RET_KEY_DIM = 256
RET_VAL_DIM = 256
RET_CHUNK = 128
ROPE_BASE = 10000.0
N_BUCKETS = 32
MAX_DISTANCE = 128
EPS = 1e-6
D_FF = 5632
IN_COLS = 5632

F32 = jnp.float32
BF16 = jnp.bfloat16

VMEM_LIMIT_BYTES = 60 * 1024 * 1024

FFN_TF = 512
FFN_TILES = D_FF // FFN_TF
PROJ_TN = 512
PROJ_COLS = 3072
SIDE_COLS = 2560
QR_COL, VR_COL = 1024, 2048
G_COL, KV_COL = 1024, 2048
PROJ_TILES = (
    ("proj", 0, "plain"), ("proj", 512, "plain"),
    ("side", KV_COL, "plain"),
    ("proj", 1024, "rotary"), ("proj", 1536, "rotary"),
    ("side", 0, "rotary_k"), ("side", 512, "rotary_k"),
    ("proj", 2048, "plain"), ("proj", 2560, "plain"),
    ("side", 1024, "plain"), ("side", 1536, "plain"),
)


def _rms_scale(x):
    return x * lax.rsqrt(jnp.mean(x * x, axis=-1, keepdims=True) + EPS)


def _silu(x):
    return x * jax.nn.sigmoid(x)


def _dot(a, b):
    return jnp.dot(a, b, preferred_element_type=F32)


def _dot_nt(a, b):
    return lax.dot_general(a, b, (((1,), (1,)), ((), ())), preferred_element_type=F32)


def _dot_tn(a, b):
    return lax.dot_general(a, b, (((0,), (0,)), ((), ())), preferred_element_type=F32)


def _resident(block_shape, index_map):
    return pl.BlockSpec(block_shape, index_map, pipeline_mode=pl.Buffered(1))


MASKED = -1e30


def _bias_kernel(rbt_ref, idx_ref, valid_ref, out_ref):
    h = pl.program_id(0)
    idx = idx_ref[...]
    acc = jnp.zeros(idx.shape, F32)
    for b in range(N_BUCKETS):
        acc = jnp.where(idx == b, rbt_ref[h, b], acc)
    for v in range(valid_ref.shape[0]):
        out_ref[v] = jnp.where(valid_ref[v] > 0.5, acc, MASKED)


def _expand_bias(rel_bias_t, bucket_idx, valid):
    nv, rows, cols = valid.shape
    return pl.pallas_call(
        _bias_kernel,
        grid=(SWA_HEADS,),
        in_specs=[pl.BlockSpec(memory_space=pltpu.SMEM),
                  pl.BlockSpec((rows, cols), lambda h: (0, 0)),
                  pl.BlockSpec((nv, rows, cols), lambda h: (0, 0, 0))],
        out_specs=pl.BlockSpec((nv, None, rows, cols), lambda h: (0, h, 0, 0)),
        out_shape=jax.ShapeDtypeStruct((nv, SWA_HEADS, rows, cols), F32),
        name="bias_expand",
    )(rel_bias_t, bucket_idx, valid)


def _t5_bucket_np(delta):
    n = np.maximum(delta, 0)
    max_exact = N_BUCKETS // 2
    nf = np.maximum(n, 1).astype(np.float64)
    large = max_exact + (np.log(nf / max_exact) / math.log(MAX_DISTANCE / max_exact)
                         * (N_BUCKETS - max_exact)).astype(np.int32)
    large = np.minimum(large, N_BUCKETS - 1)
    return np.where(n < max_exact, n, large).astype(np.int32)


def _proj_epilogue(acc, mode, cos_ref, sin_ref, out_ref, col):
    if mode == "plain":
        out_ref[:, col:col + PROJ_TN] = acc.astype(out_ref.dtype)
        return
    half = RET_KEY_DIM // 2
    cos = cos_ref[...]
    sin = sin_ref[...]
    for c0 in range(0, PROJ_TN, RET_KEY_DIM):
        x1 = acc[:, c0:c0 + half]
        x2 = acc[:, c0 + half:c0 + RET_KEY_DIM]
        o1 = x1 * cos - x2 * sin
        o2 = x1 * sin + x2 * cos
        if mode == "rotary_k":
            o1 = o1 * (RET_KEY_DIM ** -0.5)
            o2 = o2 * (RET_KEY_DIM ** -0.5)
        out_ref[:, col + c0:col + c0 + half] = o1.astype(out_ref.dtype)
        out_ref[:, col + c0 + half:col + c0 + RET_KEY_DIM] = o2.astype(out_ref.dtype)


def _proj_rows(h_ref, g_ref, w_ref, cos_ref, sin_ref, proj_ref, side_ref, xn_ref):
    xn_ref[...] = (_rms_scale(h_ref[...]) * g_ref[...]).astype(BF16)
    outs = {"proj": proj_ref, "side": side_ref}
    for t, (dst, col, mode) in enumerate(PROJ_TILES):
        acc = _dot(xn_ref[...], w_ref[:, t * PROJ_TN:(t + 1) * PROJ_TN])
        _proj_epilogue(acc, mode, cos_ref, sin_ref, outs[dst], col)


def _proj_kernel(h_ref, g_ref, w_ref, cos_ref, sin_ref, *refs, n_jobs, with_sample):
    refs = list(refs)
    sample_in = [refs.pop(0) for _ in range(3)] if with_sample else []
    src = [refs.pop(0) for _ in range(n_jobs)]
    proj_ref, side_ref = refs.pop(0), refs.pop(0)
    sample_out = [refs.pop(0) for _ in range(2)] if with_sample else []
    dst_refs = [refs.pop(0) for _ in range(n_jobs)]
    xn_ref = refs.pop(0)

    if with_sample:
        @pl.when(pl.program_id(0) == 0)
        def _():
            hs_ref, coss_ref, sins_ref = sample_in
            _proj_rows(hs_ref, g_ref, w_ref, coss_ref, sins_ref, *sample_out, refs[0])

    xn_ref[...] = (_rms_scale(h_ref[...]) * g_ref[...]).astype(BF16)
    outs = {"proj": proj_ref, "side": side_ref}
    n_t = len(PROJ_TILES)
    for t, (dst, col, mode) in enumerate(PROJ_TILES):
        acc = _dot(xn_ref[...], w_ref[:, t * PROJ_TN:(t + 1) * PROJ_TN])
        _proj_epilogue(acc, mode, cos_ref, sin_ref, outs[dst], col)
        if n_jobs >= 2:
            wgu_ref, wd_ref = src[:2]
            wgubf_ref, wdbf_ref = dst_refs[:2]
            gu_cols = wgu_ref.shape[1] // n_t
            for j in range(t * gu_cols // FFN_TF, (t + 1) * gu_cols // FFN_TF):
                c = (j % FFN_TILES) * 2 * FFN_TF + (j // FFN_TILES) * FFN_TF
                wgubf_ref[:, c:c + FFN_TF] = wgu_ref[:, j * FFN_TF:(j + 1) * FFN_TF].astype(BF16)
            d_rows = wd_ref.shape[0] // n_t
            wdbf_ref[t * d_rows:(t + 1) * d_rows] = wd_ref[t * d_rows:(t + 1) * d_rows].astype(BF16)
        if n_jobs == 4:
            win_ref, wo_ref = src[2:]
            winbf_ref, wobf_ref = dst_refs[2:]
            winbf_ref[:, t * PROJ_TN:(t + 1) * PROJ_TN] = win_ref[:, t * PROJ_TN:(t + 1) * PROJ_TN].astype(BF16)
            if t == 0:
                wobf_ref[...] = wo_ref[...].astype(BF16)


def _proj_call(h, g, layer, w_in_bf, cos, sin, tm, convert=(), sample=None):
    m = h.shape[0]
    n_steps = m // tm
    cos_blocks = cos.shape[0] // tm
    n_jobs = len(convert)
    assert n_jobs in (0, 2, 4)
    sample_in, sample_out, sample_shape, sample_scratch = [], [], [], []
    if sample is not None:
        ms = sample[0].shape[0]
        sample_in = [_resident(a.shape, lambda i: (0, 0)) for a in sample]
        sample_out = [_resident((ms, PROJ_COLS), lambda i: (0, 0)), _resident((ms, SIDE_COLS), lambda i: (0, 0))]
        sample_shape = [jax.ShapeDtypeStruct((ms, PROJ_COLS), BF16), jax.ShapeDtypeStruct((ms, SIDE_COLS), F32)]
        sample_scratch = [pltpu.VMEM((ms, D_MODEL), BF16)]
    slab_in, slab_out, slab_shape = [], [], []
    for w, w_layer in convert:
        rows, cols = w.shape[1] // n_steps, w.shape[2]
        assert w.shape[1] % n_steps == 0 and rows % 16 == 0
        slab_in.append(pl.BlockSpec((None, rows, cols), lambda i, w_layer=w_layer: (w_layer, i, 0)))
        slab_out.append(pl.BlockSpec((rows, cols), lambda i: (i, 0)))
        slab_shape.append(jax.ShapeDtypeStruct(w.shape[1:], BF16))
    return pl.pallas_call(
        functools.partial(_proj_kernel, n_jobs=n_jobs, with_sample=sample is not None),
        grid=(n_steps,),
        in_specs=[
            pl.BlockSpec((tm, D_MODEL), lambda i: (i, 0)),
            _resident((None, 1, D_MODEL), lambda i: (layer, 0, 0)),
            _resident((D_MODEL, IN_COLS), lambda i: (0, 0)),
            pl.BlockSpec((tm, RET_KEY_DIM // 2), lambda i: (i % cos_blocks, 0)),
            pl.BlockSpec((tm, RET_KEY_DIM // 2), lambda i: (i % cos_blocks, 0)),
        ] + sample_in + slab_in,
        out_specs=[
            pl.BlockSpec((tm, PROJ_COLS), lambda i: (i, 0)),
            pl.BlockSpec((tm, SIDE_COLS), lambda i: (i, 0)),
        ] + sample_out + slab_out,
        out_shape=[jax.ShapeDtypeStruct((m, PROJ_COLS), BF16),
                   jax.ShapeDtypeStruct((m, SIDE_COLS), F32)] + sample_shape + slab_shape,
        scratch_shapes=[pltpu.VMEM((tm, D_MODEL), BF16)] + sample_scratch,
        compiler_params=pltpu.CompilerParams(
            dimension_semantics=("arbitrary",), vmem_limit_bytes=VMEM_LIMIT_BYTES),
        name="proj",
    )(h, g, w_in_bf, cos, sin, *(sample or ()), *[w for w, _ in convert])


def _proj_cast_kernel(h_ref, g_ref, w_ref, cos_ref, sin_ref, proj_ref, side_ref, wbf_ref, xn_ref):
    j = pl.program_id(0)

    @pl.when(j == 0)
    def _():
        xn_ref[...] = (_rms_scale(h_ref[...]) * g_ref[...]).astype(BF16)

    w_bf = w_ref[...].astype(BF16)
    wbf_ref[...] = w_bf
    acc = _dot(xn_ref[...], w_bf)
    outs = {"proj": proj_ref, "side": side_ref}
    for t, (dst, col, mode) in enumerate(PROJ_TILES):
        @pl.when(j == t)
        def _():
            _proj_epilogue(acc, mode, cos_ref, sin_ref, outs[dst], col)


def _proj_cast_call(h, g, layer, w_in, cos, sin):
    m = h.shape[0]
    full = lambda shape: _resident(shape, lambda j: (0,) * len(shape))
    return pl.pallas_call(
        _proj_cast_kernel,
        grid=(len(PROJ_TILES),),
        in_specs=[
            full((m, D_MODEL)),
            _resident((None, 1, D_MODEL), lambda j: (layer, 0, 0)),
            pl.BlockSpec((None, D_MODEL, PROJ_TN), lambda j: (layer, 0, j)),
            full((m, RET_KEY_DIM // 2)),
            full((m, RET_KEY_DIM // 2)),
        ],
        out_specs=[
            pl.BlockSpec((m, PROJ_COLS), lambda j: (0, 0)),
            pl.BlockSpec((m, SIDE_COLS), lambda j: (0, 0)),
            pl.BlockSpec((D_MODEL, PROJ_TN), lambda j: (0, j)),
        ],
        out_shape=[jax.ShapeDtypeStruct((m, PROJ_COLS), BF16),
                   jax.ShapeDtypeStruct((m, SIDE_COLS), F32),
                   jax.ShapeDtypeStruct((D_MODEL, IN_COLS), BF16)],
        scratch_shapes=[pltpu.VMEM((m, D_MODEL), BF16)],
        compiler_params=pltpu.CompilerParams(
            dimension_semantics=("arbitrary",), vmem_limit_bytes=VMEM_LIMIT_BYTES),
        name="proj_cast",
    )(h, g, w_in, cos, sin)


def _softmax_sink(s, sink):
    m = jnp.maximum(jnp.max(s, axis=-1, keepdims=True), sink)
    p = jnp.exp(s - m)
    return p, 1.0 / (jnp.sum(p, axis=-1, keepdims=True) + jnp.exp(sink - m))


def _gate_out(o, g):
    return (_rms_scale(o) * _silu(g)).astype(BF16)


WO_BATCHES = 2


def _mix_prompt_kernel(sink_ref, cdec_ref, p_ref, s_ref,
                       bias_ref, dmask_ref, qdec_ref, kdec_ref, wo_ref, h_ref, *refs, with_sample):
    refs = list(refs)
    sample_in = [refs.pop(0) for _ in range(2)] if with_sample else []
    out_ref, st_ref = refs.pop(0), refs.pop(0)
    sample_out = [refs.pop(0)] if with_sample else []
    kvp_ref, cat_refs = refs[0], refs[1:]
    n = pl.program_id(0)

    @pl.when(n == 0)
    def _():
        st_ref[...] = jnp.zeros(st_ref.shape, F32)
        kvp_ref[...] = jnp.zeros(kvp_ref.shape, BF16)
        if with_sample:
            cats_ref, hs_ref = sample_in
            sample_out[0][...] = hs_ref[...] + _dot(cats_ref[...], wo_ref[...])

    scale = SWA_HEAD_DIM ** -0.5
    v_off = SWA_KV_HEADS * SWA_HEAD_DIM

    def cat_of(b):
        return cat_refs[b // WO_BATCHES].at[b % WO_BATCHES]

    def wo_rows(b0):
        bs = slice(b0, b0 + WO_BATCHES)
        cat = cat_refs[b0 // WO_BATCHES][...].reshape(WO_BATCHES * BLOCK, D_MODEL)
        out_ref[bs] = h_ref[bs] + _dot(cat, wo_ref[...]).reshape(WO_BATCHES, BLOCK, D_MODEL)

    def attention(b, kh):
        c0 = kh * SWA_HEAD_DIM
        k_cur = s_ref[b, :, KV_COL + c0:KV_COL + c0 + SWA_HEAD_DIM].astype(BF16)
        v_cur = s_ref[b, :, KV_COL + v_off + c0:KV_COL + v_off + c0 + SWA_HEAD_DIM].astype(BF16)
        k_cat = jnp.concatenate([kvp_ref[b, :, c0:c0 + SWA_HEAD_DIM], k_cur], axis=0)
        v_cat = jnp.concatenate([kvp_ref[b, :, v_off + c0:v_off + c0 + SWA_HEAD_DIM], v_cur], axis=0)
        kvp_ref[b, :, c0:c0 + SWA_HEAD_DIM] = k_cur
        kvp_ref[b, :, v_off + c0:v_off + c0 + SWA_HEAD_DIM] = v_cur
        for gq in range(SWA_GROUP):
            h = kh * SWA_GROUP + gq
            q = p_ref[b, :, h * SWA_HEAD_DIM:(h + 1) * SWA_HEAD_DIM]
            s = _dot_nt(q, k_cat) * scale + bias_ref[h]
            p, inv = _softmax_sink(s, sink_ref[h])
            o = _dot(p.astype(BF16), v_cat) * inv
            cat_of(b)[:, h * SWA_HEAD_DIM:(h + 1) * SWA_HEAD_DIM] = o.astype(BF16)

    def retention(b, h):
        cs = slice(h * RET_KEY_DIM, (h + 1) * RET_KEY_DIM)
        q = p_ref[b, :, QR_COL + h * RET_KEY_DIM:QR_COL + (h + 1) * RET_KEY_DIM]
        k32 = s_ref[b, :, cs]
        v = p_ref[b, :, VR_COL + h * RET_VAL_DIM:VR_COL + (h + 1) * RET_VAL_DIM]
        s = _dot_nt(q, k32.astype(BF16)) * dmask_ref[h]
        o = _dot(s.astype(BF16), v)
        st = st_ref[b, h]
        o = o + _dot(q, st.astype(BF16)) * qdec_ref[:, cs]
        kd = (k32 * kdec_ref[:, cs]).astype(BF16)
        st_ref[b, h] = cdec_ref[h] * st + _dot_tn(kd, v)
        cat_of(b)[:, SWA_WIDTH + h * RET_VAL_DIM:SWA_WIDTH + (h + 1) * RET_VAL_DIM] = (
            _gate_out(o, s_ref[b, :, G_COL + h * RET_VAL_DIM:G_COL + (h + 1) * RET_VAL_DIM]))

    for b in range(BATCH):
        for kh in range(SWA_KV_HEADS):
            attention(b, kh)
        for h in range(RET_HEADS):
            retention(b, h)
        if (b + 1) % WO_BATCHES == 0:
            wo_rows(b + 1 - WO_BATCHES)


def _mix_prompt_call(proj, side, h, w_o_bf, sinks_l, cdec, bias, dmask, qdec, kdec, sample=None):
    nblk = SEQ // BLOCK
    proj3 = proj.reshape(BATCH, SEQ, PROJ_COLS)
    side3 = side.reshape(BATCH, SEQ, SIDE_COLS)
    h3 = h.reshape(BATCH, SEQ, D_MODEL)
    smem = pl.BlockSpec(memory_space=pltpu.SMEM)
    full = lambda shape: _resident(shape, lambda n: (0,) * len(shape))
    sample_in, sample_out, sample_shape = [], [], []
    if sample is not None:
        sample_in = [full(a.shape) for a in sample]
        sample_out = [full(sample[1].shape)]
        sample_shape = [jax.ShapeDtypeStruct(sample[1].shape, F32)]
    out, st, *hs_new = pl.pallas_call(
        functools.partial(_mix_prompt_kernel, with_sample=sample is not None),
        grid=(nblk,),
        in_specs=[
            smem, smem,
            pl.BlockSpec((BATCH, BLOCK, PROJ_COLS), lambda n: (0, n, 0)),
            pl.BlockSpec((BATCH, BLOCK, SIDE_COLS), lambda n: (0, n, 0)),
            pl.BlockSpec((None, SWA_HEADS, BLOCK, 2 * BLOCK), lambda n: (jnp.minimum(n, 1), 0, 0, 0)),
            full((RET_HEADS, RET_CHUNK, RET_CHUNK)),
            full((RET_CHUNK, RET_WIDTH)),
            full((RET_CHUNK, RET_WIDTH)),
            full((D_MODEL, D_MODEL)),
            pl.BlockSpec((BATCH, BLOCK, D_MODEL), lambda n: (0, n, 0)),
        ] + sample_in,
        out_specs=[
            pl.BlockSpec((BATCH, BLOCK, D_MODEL), lambda n: (0, n, 0)),
            _resident((BATCH, RET_HEADS, RET_KEY_DIM, RET_VAL_DIM), lambda n: (0, 0, 0, 0)),
        ] + sample_out,
        out_shape=[jax.ShapeDtypeStruct((BATCH, SEQ, D_MODEL), F32),
                   jax.ShapeDtypeStruct((BATCH, RET_HEADS, RET_KEY_DIM, RET_VAL_DIM), F32)] + sample_shape,
        scratch_shapes=[pltpu.VMEM((BATCH, BLOCK, 2 * SWA_KV_HEADS * SWA_HEAD_DIM), BF16)]
        + [pltpu.VMEM((WO_BATCHES, BLOCK, D_MODEL), BF16)] * (BATCH // WO_BATCHES),
        compiler_params=pltpu.CompilerParams(
            dimension_semantics=("arbitrary",), vmem_limit_bytes=VMEM_LIMIT_BYTES),
        name="mix_prompt",
    )(sinks_l, cdec, proj3, side3, bias, dmask, qdec, kdec, w_o_bf, h3, *(sample or ()))
    return (out.reshape(BATCH * SEQ, D_MODEL), st, *hs_new)


SAMPLE_BB = 4
SAMPLE_ROWS = SAMPLE_BB * DEC_SEQ
N_SAMPLE = DEC_BATCH * DEC_SEQ
CACHE_ROWS = WINDOW * SWA_KV_HEADS


def _mix_sample_kernel(sink_ref, cdec_ref, prow_ref, srow_ref, pall_ref, sall_ref, ck_ref, cv_ref,
                       knew_ref, vnew_ref, st_ref, bias_ref, dmask_ref, qdec_ref, kdec_ref, *rest, layer):
    cat_ref, stout_ref, ckout_ref, cvout_ref = rest[-4:]
    if layer == 0:
        for ref in (stout_ref, ckout_ref, cvout_ref):
            ref[1:] = jnp.zeros((DEPTH - 1,) + ref.shape[1:], F32)
        stout_ref, ckout_ref, cvout_ref = stout_ref.at[0], ckout_ref.at[0], cvout_ref.at[0]
    r = SAMPLE_ROWS
    row_b = lax.broadcasted_iota(jnp.int32, (r, 1), 0) // DEC_SEQ
    row_b4 = lax.broadcasted_iota(jnp.int32, (SWA_GROUP * r, 1), 0) % r // DEC_SEQ
    scale = SWA_HEAD_DIM ** -0.5
    v_off = SWA_KV_HEADS * SWA_HEAD_DIM
    new_rows = DEC_SEQ * SWA_KV_HEADS

    for bi in range(SAMPLE_BB):
        ckout_ref[bi, :CACHE_ROWS - new_rows] = ck_ref[bi, new_rows:]
        ckout_ref[bi, CACHE_ROWS - new_rows:] = knew_ref[bi]
        cvout_ref[bi, :CACHE_ROWS - new_rows] = cv_ref[bi, new_rows:]
        cvout_ref[bi, CACHE_ROWS - new_rows:] = vnew_ref[bi]

    for kh in range(SWA_KV_HEADS):
        c0 = kh * SWA_HEAD_DIM
        q4 = jnp.concatenate(
            [prow_ref[:, (kh * SWA_GROUP + gq) * SWA_HEAD_DIM:(kh * SWA_GROUP + gq + 1) * SWA_HEAD_DIM]
             for gq in range(SWA_GROUP)], axis=0)
        k_new = sall_ref[:, KV_COL + c0:KV_COL + c0 + SWA_HEAD_DIM].astype(BF16)
        v_new = sall_ref[:, KV_COL + v_off + c0:KV_COL + v_off + c0 + SWA_HEAD_DIM].astype(BF16)
        s_cache = jnp.zeros((SWA_GROUP * r, CACHE_ROWS), F32)
        for bi in range(SAMPLE_BB):
            s_cache = jnp.where(row_b4 == bi, _dot_nt(q4, ck_ref[bi].astype(BF16)), s_cache)
        s_new = _dot_nt(q4, k_new)
        s4 = jnp.concatenate([s_cache, s_new], axis=1) * scale
        p_parts, inv_parts = [], []
        for gq in range(SWA_GROUP):
            h = kh * SWA_GROUP + gq
            s = s4[gq * r:(gq + 1) * r] + bias_ref[kh, h]
            p, inv = _softmax_sink(s, sink_ref[h])
            p_parts.append(p)
            inv_parts.append(inv)
        p4 = jnp.concatenate(p_parts, axis=0)
        inv4 = jnp.concatenate(inv_parts, axis=0)
        p_cache = p4[:, :CACHE_ROWS]
        o4 = _dot(p4[:, CACHE_ROWS:].astype(BF16), v_new)
        for bi in range(SAMPLE_BB):
            o4 = o4 + _dot(jnp.where(row_b4 == bi, p_cache, 0.0).astype(BF16), cv_ref[bi].astype(BF16))
        o4 = o4 * inv4
        for gq in range(SWA_GROUP):
            h = kh * SWA_GROUP + gq
            cat_ref[:, h * SWA_HEAD_DIM:(h + 1) * SWA_HEAD_DIM] = o4[gq * r:(gq + 1) * r].astype(BF16)

    for h in range(RET_HEADS):
        cs = slice(h * RET_KEY_DIM, (h + 1) * RET_KEY_DIM)
        q = prow_ref[:, 1024 + h * RET_KEY_DIM:1024 + (h + 1) * RET_KEY_DIM]
        k_all = sall_ref[:, cs].astype(BF16)
        v_all = pall_ref[:, 2048 + h * RET_VAL_DIM:2048 + (h + 1) * RET_VAL_DIM]
        s = _dot_nt(q, k_all) * dmask_ref[h]
        o = _dot(s.astype(BF16), v_all)
        k32 = srow_ref[:, cs]
        v = prow_ref[:, 2048 + h * RET_VAL_DIM:2048 + (h + 1) * RET_VAL_DIM]
        kd = k32 * kdec_ref[:, cs]
        cross = jnp.zeros((r, RET_VAL_DIM), F32)
        for bi in range(SAMPLE_BB):
            st = st_ref[bi, h]
            cross = jnp.where(row_b == bi, _dot(q, st.astype(BF16)), cross)
            kd_b = jnp.where(row_b == bi, kd, 0.0).astype(BF16)
            stout_ref[bi, h] = cdec_ref[h] * st + _dot_tn(kd_b, v)
        o = o + cross * qdec_ref[:, cs]
        g = srow_ref[:, 1024 + h * RET_VAL_DIM:1024 + (h + 1) * RET_VAL_DIM]
        cat_ref[:, SWA_WIDTH + h * RET_VAL_DIM:SWA_WIDTH + (h + 1) * RET_VAL_DIM] = _gate_out(o, g)


def _mix_sample_call(proj, side, cache_k, cache_v, state_ret, layer, sinks_l, cdec, bias, dmask, qdec, kdec,
                     stacked):
    r = SAMPLE_ROWS
    new_rows = DEC_SEQ * SWA_KV_HEADS
    kv_w = SWA_KV_HEADS * SWA_HEAD_DIM
    k_new = side[:, KV_COL:KV_COL + kv_w].reshape(DEC_BATCH, new_rows, SWA_HEAD_DIM)
    v_new = side[:, KV_COL + kv_w:KV_COL + 2 * kv_w].reshape(DEC_BATCH, new_rows, SWA_HEAD_DIM)
    smem = pl.BlockSpec(memory_space=pltpu.SMEM)
    full = lambda shape: pl.BlockSpec(shape, lambda c: (0,) * len(shape))
    cache_in = pl.BlockSpec((None, SAMPLE_BB, CACHE_ROWS, SWA_HEAD_DIM), lambda c: (layer, c, 0, 0))
    state_dims = (SAMPLE_BB, RET_HEADS, RET_KEY_DIM, RET_VAL_DIM)
    cache_dims = (SAMPLE_BB, CACHE_ROWS, SWA_HEAD_DIM)
    if layer == 0:
        out_block = lambda dims: pl.BlockSpec((DEPTH,) + dims, lambda c: (0, c) + (0,) * (len(dims) - 1))
    else:
        out_block = lambda dims: pl.BlockSpec((None,) + dims, lambda c: (layer, c) + (0,) * (len(dims) - 1))
    new_in = pl.BlockSpec((SAMPLE_BB, new_rows, SWA_HEAD_DIM), lambda c: (c, 0, 0))
    in_specs = [
        smem, smem,
        pl.BlockSpec((r, PROJ_COLS), lambda c: (c, 0)),
        pl.BlockSpec((r, SIDE_COLS), lambda c: (c, 0)),
        full((N_SAMPLE, PROJ_COLS)),
        full((N_SAMPLE, SIDE_COLS)),
        cache_in, cache_in, new_in, new_in,
        pl.BlockSpec((None, SAMPLE_BB, RET_HEADS, RET_KEY_DIM, RET_VAL_DIM), lambda c: (layer, c, 0, 0, 0)),
        pl.BlockSpec((SWA_KV_HEADS, SWA_HEADS, r, CACHE_ROWS + N_SAMPLE), lambda c: (0, 0, c, 0)),
        pl.BlockSpec((RET_HEADS, r, N_SAMPLE), lambda c: (0, c, 0)),
        pl.BlockSpec((r, RET_WIDTH), lambda c: (c, 0)),
        pl.BlockSpec((r, RET_WIDTH), lambda c: (c, 0)),
    ]
    args = [sinks_l, cdec, proj, side, proj, side, cache_k, cache_v, k_new, v_new, state_ret, bias, dmask, qdec,
            kdec]
    aliases = {}
    if stacked is not None:
        for k, buf in enumerate(stacked):
            aliases[len(args)] = 1 + k
            in_specs.append(pl.BlockSpec(memory_space=pl.ANY))
            args.append(buf)
    cache_shape = jax.ShapeDtypeStruct((DEPTH, DEC_BATCH, CACHE_ROWS, SWA_HEAD_DIM), F32)
    cat, *new_stacked = pl.pallas_call(
        functools.partial(_mix_sample_kernel, layer=layer),
        grid=(DEC_BATCH // SAMPLE_BB,),
        in_specs=in_specs,
        out_specs=[
            pl.BlockSpec((r, D_MODEL), lambda c: (c, 0)),
            out_block(state_dims), out_block(cache_dims), out_block(cache_dims),
        ],
        out_shape=[jax.ShapeDtypeStruct((N_SAMPLE, D_MODEL), BF16),
                   jax.ShapeDtypeStruct((DEPTH, DEC_BATCH, RET_HEADS, RET_KEY_DIM, RET_VAL_DIM), F32),
                   cache_shape, cache_shape],
        input_output_aliases=aliases,
        compiler_params=pltpu.CompilerParams(
            dimension_semantics=("arbitrary",), vmem_limit_bytes=VMEM_LIMIT_BYTES),
        name="mix_sample",
    )(*args)
    return cat, tuple(new_stacked)


WO_TN = 512


def _wo_cast_kernel(cat_ref, w_ref, h_ref, out_ref, wbf_ref):
    w_bf = w_ref[...].astype(BF16)
    wbf_ref[...] = w_bf
    out_ref[...] = h_ref[...] + _dot(cat_ref[...], w_bf)


def _wo_cast_call(cat, w_o, layer, h):
    m = h.shape[0]
    return pl.pallas_call(
        _wo_cast_kernel,
        grid=(D_MODEL // WO_TN,),
        in_specs=[
            _resident((m, D_MODEL), lambda j: (0, 0)),
            pl.BlockSpec((None, D_MODEL, WO_TN), lambda j: (layer, 0, j)),
            pl.BlockSpec((m, WO_TN), lambda j: (0, j)),
        ],
        out_specs=[pl.BlockSpec((m, WO_TN), lambda j: (0, j)),
                   pl.BlockSpec((D_MODEL, WO_TN), lambda j: (0, j))],
        out_shape=[jax.ShapeDtypeStruct((m, D_MODEL), F32),
                   jax.ShapeDtypeStruct((D_MODEL, D_MODEL), BF16)],
        compiler_params=pltpu.CompilerParams(
            dimension_semantics=("arbitrary",), vmem_limit_bytes=VMEM_LIMIT_BYTES),
        name="wo_cast",
    )(cat, w_o, h)


def _ffn_tile(f, n_f, h_ref, g_ref, wgu_ref, wd_ref, gfin_ref, out_ref, xn_ref, final_norm):
    @pl.when(f == 0)
    def _():
        x = h_ref[...]
        xn_ref[...] = (_rms_scale(x) * g_ref[...]).astype(BF16)
        out_ref[...] = x

    ab = _dot(xn_ref[...], wgu_ref[...])
    act = (_silu(ab[:, :FFN_TF]) * ab[:, FFN_TF:]).astype(BF16)
    out_ref[...] += _dot(act, wd_ref[...])

    if final_norm:
        @pl.when(f == n_f - 1)
        def _():
            out_ref[...] = _rms_scale(out_ref[...]) * gfin_ref[...]


def _ffn_kernel(h_ref, g_ref, wgu_ref, wd_ref, gfin_ref, hs_ref, out_ref, outs_ref, xn_ref, xns_ref, *,
                final_norm):
    i, f, n_f = pl.program_id(0), pl.program_id(1), pl.num_programs(1)
    _ffn_tile(f, n_f, h_ref, g_ref, wgu_ref, wd_ref, gfin_ref, out_ref, xn_ref, final_norm)

    @pl.when(i == 0)
    def _():
        _ffn_tile(f, n_f, hs_ref, g_ref, wgu_ref, wd_ref, gfin_ref, outs_ref, xns_ref, final_norm)


FFN_VMEM_LIMIT_BYTES = 62 * 1024 * 1024


def _ffn_call(h, h_sample, g, w_gu_bf, w_d_bf, g_final, layer, tm):
    m, ms = h.shape[0], h_sample.shape[0]
    return pl.pallas_call(
        functools.partial(_ffn_kernel, final_norm=(layer == DEPTH - 1)),
        grid=(m // tm, FFN_TILES),
        in_specs=[
            pl.BlockSpec((tm, D_MODEL), lambda i, f: (i, 0)),
            pl.BlockSpec((None, 1, D_MODEL), lambda i, f: (layer, 0, 0)),
            pl.BlockSpec((D_MODEL, 2 * FFN_TF), lambda i, f: (0, f)),
            pl.BlockSpec((FFN_TF, D_MODEL), lambda i, f: (f, 0)),
            pl.BlockSpec((1, D_MODEL), lambda i, f: (0, 0)),
            _resident((ms, D_MODEL), lambda i, f: (0, 0)),
        ],
        out_specs=[pl.BlockSpec((tm, D_MODEL), lambda i, f: (i, 0)),
                   _resident((ms, D_MODEL), lambda i, f: (0, 0))],
        out_shape=[jax.ShapeDtypeStruct((m, D_MODEL), F32),
                   jax.ShapeDtypeStruct((ms, D_MODEL), F32)],
        scratch_shapes=[pltpu.VMEM((tm, D_MODEL), BF16), pltpu.VMEM((ms, D_MODEL), BF16)],
        compiler_params=pltpu.CompilerParams(
            dimension_semantics=("arbitrary", "arbitrary"), vmem_limit_bytes=FFN_VMEM_LIMIT_BYTES),
        name="ffn",
    )(h, g, w_gu_bf, w_d_bf, g_final, h_sample)


def _rope_tables(pos):
    half = RET_KEY_DIM // 2
    inv = 1.0 / (ROPE_BASE ** jnp.linspace(0.0, 1.0, half, dtype=F32))
    ang = pos.astype(F32)[:, None] * inv[None, :]
    return jnp.cos(ang), jnp.sin(ang)


def _decay_tables(c):
    lg = jnp.log(1.0 - jnp.exp2(-5.0 - jnp.arange(RET_HEADS, dtype=F32)))
    idx = jnp.arange(c, dtype=F32)
    diff = idx[:, None] - idx[None, :]
    dmask = jnp.where(diff[None] >= 0, jnp.exp(jnp.maximum(diff, 0.0)[None] * lg[:, None, None]), 0.0)
    q_decay = jnp.exp((idx[:, None] + 1.0) * lg[None, :])
    k_decay = jnp.exp((c - 1.0 - idx)[:, None] * lg[None, :])
    c_decay = jnp.exp(c * lg)
    return dmask, q_decay, k_decay, c_decay


def _per_head_cols(t):
    return jnp.repeat(t, RET_KEY_DIM, axis=1)


def kernel(x_prompt, x_sample, cache_k_win, cache_v_win, state_ret, rel_bias, w_in, sinks, w_o,
           norm_mix, norm_ffn, w_gate_up, w_down, norm_final):
    norm_mix3 = norm_mix.reshape(DEPTH, 1, D_MODEL)
    norm_ffn3 = norm_ffn.reshape(DEPTH, 1, D_MODEL)
    norm_final2 = norm_final.reshape(1, D_MODEL)
    cache_k = cache_k_win.reshape(DEPTH, DEC_BATCH, CACHE_ROWS, SWA_HEAD_DIM)
    cache_v = cache_v_win.reshape(DEPTH, DEC_BATCH, CACHE_ROWS, SWA_HEAD_DIM)

    cos_p, sin_p = _rope_tables(jnp.arange(SEQ, dtype=jnp.int32))
    cos_4, sin_4 = _rope_tables(PAST_LEN + jnp.arange(DEC_SEQ, dtype=jnp.int32))
    cos_s = jnp.tile(cos_4, (DEC_BATCH, 1))
    sin_s = jnp.tile(sin_4, (DEC_BATCH, 1))

    dmask_p, qd_p, kd_p, cdec_p = _decay_tables(RET_CHUNK)
    qdec_p = _per_head_cols(qd_p)
    kdec_p = _per_head_cols(kd_p)
    dmask_4, qd_4, kd_4, cdec_s = _decay_tables(DEC_SEQ)
    eye_b = jnp.eye(DEC_BATCH, dtype=F32)
    dmask_s = jax.vmap(lambda d: jnp.kron(eye_b, d))(dmask_4)
    qdec_s = jnp.tile(_per_head_cols(qd_4), (DEC_BATCH, 1))
    kdec_s = jnp.tile(_per_head_cols(kd_4), (DEC_BATCH, 1))

    qi = np.arange(BLOCK)[:, None]
    kj = np.arange(2 * BLOCK)[None, :]
    delta_p = qi + BLOCK - kj
    in_window = (delta_p >= 0) & (delta_p < WINDOW)
    valid_p = np.stack([in_window & (kj >= BLOCK), in_window]).astype(np.float32)
    rows = np.arange(N_SAMPLE)
    rb, rt = rows // DEC_SEQ, rows % DEC_SEQ
    cache_pos = np.arange(CACHE_ROWS) // SWA_KV_HEADS
    cache_head = np.arange(CACHE_ROWS) % SWA_KV_HEADS
    delta_cache = (WINDOW + rt)[:, None] - cache_pos[None, :]
    delta_new = rt[:, None] - rt[None, :]
    same_b = rb[:, None] == rb[None, :]
    delta_s = np.concatenate([delta_cache, delta_new], axis=1)
    valid_s = np.stack([
        np.concatenate([(delta_cache < WINDOW) & (cache_head == kh)[None, :], same_b & (delta_new >= 0)], axis=1)
        for kh in range(SWA_KV_HEADS)]).astype(np.float32)
    rel_bias_t = rel_bias.T
    bias_p = _expand_bias(rel_bias_t, jnp.asarray(_t5_bucket_np(delta_p)), jnp.asarray(valid_p))
    bias_s = _expand_bias(rel_bias_t, jnp.asarray(_t5_bucket_np(delta_s)), jnp.asarray(valid_s))

    hp = x_prompt.reshape(BATCH * SEQ, D_MODEL)
    hs = x_sample.reshape(N_SAMPLE, D_MODEL)
    tm_proj, tm_ffn = 256, 1024
    kp_new, vp_new, rp_new = [], [], []
    sample_new = None
    kv_w = SWA_KV_HEADS * SWA_HEAD_DIM
    for l in range(DEPTH):
        convert = [(w_gate_up, l), (w_down, l)]
        if l + 1 < DEPTH:
            convert += [(w_in, l + 1), (w_o, l + 1)]
        if l == 0:
            proj_s, side_s, w_in_bf = _proj_cast_call(hs, norm_mix3, l, w_in, cos_s, sin_s)
            proj, side, w_gu_bf, w_d_bf, *next_bf = _proj_call(hp, norm_mix3, l, w_in_bf, cos_p, sin_p, tm_proj,
                                                                convert)
        else:
            proj, side, proj_s, side_s, w_gu_bf, w_d_bf, *next_bf = _proj_call(
                hp, norm_mix3, l, w_in_bf, cos_p, sin_p, tm_proj, convert, sample=(hs, cos_s, sin_s))

        cat, sample_new = _mix_sample_call(proj_s, side_s, cache_k, cache_v, state_ret, l, sinks[l], cdec_s,
                                           bias_s, dmask_s, qdec_s, kdec_s, sample_new)
        if l == 0:
            hs, w_o_bf = _wo_cast_call(cat, w_o, l, hs)
            hp, st_p = _mix_prompt_call(proj, side, hp, w_o_bf, sinks[l], cdec_p, bias_p, dmask_p, qdec_p, kdec_p)
        else:
            hp, st_p, hs = _mix_prompt_call(proj, side, hp, w_o_bf, sinks[l], cdec_p, bias_p, dmask_p, qdec_p,
                                            kdec_p, sample=(cat, hs))
        hp, hs = _ffn_call(hp, hs, norm_ffn3, w_gu_bf, w_d_bf, norm_final2, l, tm_ffn)
        kv_tail = side.reshape(BATCH, SEQ, SIDE_COLS)[:, SEQ - WINDOW:, KV_COL:KV_COL + 2 * kv_w]
        kp_new.append(kv_tail[..., :kv_w].reshape(BATCH, WINDOW, SWA_KV_HEADS, SWA_HEAD_DIM))
        vp_new.append(kv_tail[..., kv_w:].reshape(BATCH, WINDOW, SWA_KV_HEADS, SWA_HEAD_DIM))
        rp_new.append(st_p)
        if next_bf:
            w_in_bf, w_o_bf = next_bf

    y_prompt = hp.reshape(BATCH, SEQ, D_MODEL)
    y_sample = hs.reshape(DEC_BATCH, DEC_SEQ, D_MODEL)
    rs_new, ks_new, vs_new = sample_new
    return (y_prompt, y_sample,
            jnp.stack(kp_new), jnp.stack(vp_new), jnp.stack(rp_new),
            ks_new.reshape(cache_k_win.shape), vs_new.reshape(cache_v_win.shape), rs_new)
```

```python
import functools
import math

import numpy as np
import jax
import jax.numpy as jnp
from jax import lax
from jax.experimental import pallas as pl
from jax.experimental.pallas import tpu as pltpu

D_MODEL = 2048
BATCH = 4
SEQ = 2048
DEPTH = 4
DEC_BATCH = 32
DEC_SEQ = 4
PAST_LEN = 16384

SWA_WIDTH = 1024
RET_WIDTH = 1024
SWA_HEADS = 8
SWA_KV_HEADS = 2
SWA_GROUP = SWA_HEADS // SWA_KV_HEADS
SWA_HEAD_DIM = 128
WINDOW = 128
BLOCK = WINDOW
RET_HEADS = 4
RET_KEY_DIM = 256
RET_VAL_DIM = 256
RET_CHUNK = 128
ROPE_BASE = 10000.0
N_BUCKETS = 32
MAX_DISTANCE = 128
EPS = 1e-6
D_FF = 5632
IN_COLS = 5632

F32 = jnp.float32
BF16 = jnp.bfloat16

VMEM_LIMIT_BYTES = 60 * 1024 * 1024

FFN_TF = 512
FFN_TILES = D_FF // FFN_TF
PROJ_TN = 512
PROJ_COLS = 3072
SIDE_COLS = 2560
QR_COL, VR_COL = 1024, 2048
G_COL, KV_COL = 1024, 2048
PROJ_TILES = (
    ("proj", 0, "plain"), ("proj", 512, "plain"),
    ("side", KV_COL, "plain"),
    ("proj", 1024, "rotary"), ("proj", 1536, "rotary"),
    ("side", 0, "rotary_k"), ("side", 512, "rotary_k"),
    ("proj", 2048, "plain"), ("proj", 2560, "plain"),
    ("side", 1024, "plain"), ("side", 1536, "plain"),
)


def _rms_scale(x):
    return x * lax.rsqrt(jnp.mean(x * x, axis=-1, keepdims=True) + EPS)


def _silu(x):
    return x * jax.nn.sigmoid(x)


def _dot(a, b):
    return jnp.dot(a, b, preferred_element_type=F32)


def _dot_nt(a, b):
    return lax.dot_general(a, b, (((1,), (1,)), ((), ())), preferred_element_type=F32)


def _dot_tn(a, b):
    return lax.dot_general(a, b, (((0,), (0,)), ((), ())), preferred_element_type=F32)


def _resident(block_shape, index_map):
    return pl.BlockSpec(block_shape, index_map, pipeline_mode=pl.Buffered(1))


MASKED = -1e30


def _bias_kernel(rbt_ref, idx_ref, valid_ref, out_ref):
    h = pl.program_id(0)
    idx = idx_ref[...]
    acc = jnp.zeros(idx.shape, F32)
    for b in range(N_BUCKETS):
        acc = jnp.where(idx == b, rbt_ref[h, b], acc)
    for v in range(valid_ref.shape[0]):
        out_ref[v] = jnp.where(valid_ref[v] > 0.5, acc, MASKED)


def _expand_bias(rel_bias_t, bucket_idx, valid):
    nv, rows, cols = valid.shape
    return pl.pallas_call(
        _bias_kernel,
        grid=(SWA_HEADS,),
        in_specs=[pl.BlockSpec(memory_space=pltpu.SMEM),
                  pl.BlockSpec((rows, cols), lambda h: (0, 0)),
                  pl.BlockSpec((nv, rows, cols), lambda h: (0, 0, 0))],
        out_specs=pl.BlockSpec((nv, None, rows, cols), lambda h: (0, h, 0, 0)),
        out_shape=jax.ShapeDtypeStruct((nv, SWA_HEADS, rows, cols), F32),
        name="bias_expand",
    )(rel_bias_t, bucket_idx, valid)


def _t5_bucket_np(delta):
    n = np.maximum(delta, 0)
    max_exact = N_BUCKETS // 2
    nf = np.maximum(n, 1).astype(np.float64)
    large = max_exact + (np.log(nf / max_exact) / math.log(MAX_DISTANCE / max_exact)
                         * (N_BUCKETS - max_exact)).astype(np.int32)
    large = np.minimum(large, N_BUCKETS - 1)
    return np.where(n < max_exact, n, large).astype(np.int32)


def _proj_epilogue(acc, mode, cos_ref, sin_ref, out_ref, col):
    if mode == "plain":
        out_ref[:, col:col + PROJ_TN] = acc.astype(out_ref.dtype)
        return
    half = RET_KEY_DIM // 2
    cos = cos_ref[...]
    sin = sin_ref[...]
    for c0 in range(0, PROJ_TN, RET_KEY_DIM):
        x1 = acc[:, c0:c0 + half]
        x2 = acc[:, c0 + half:c0 + RET_KEY_DIM]
        o1 = x1 * cos - x2 * sin
        o2 = x1 * sin + x2 * cos
        if mode == "rotary_k":
            o1 = o1 * (RET_KEY_DIM ** -0.5)
            o2 = o2 * (RET_KEY_DIM ** -0.5)
        out_ref[:, col + c0:col + c0 + half] = o1.astype(out_ref.dtype)
        out_ref[:, col + c0 + half:col + c0 + RET_KEY_DIM] = o2.astype(out_ref.dtype)


def _proj_rows(h_ref, g_ref, w_ref, cos_ref, sin_ref, proj_ref, side_ref, xn_ref):
    xn_ref[...] = (_rms_scale(h_ref[...]) * g_ref[...]).astype(BF16)
    outs = {"proj": proj_ref, "side": side_ref}
    for t, (dst, col, mode) in enumerate(PROJ_TILES):
        acc = _dot(xn_ref[...], w_ref[:, t * PROJ_TN:(t + 1) * PROJ_TN])
        _proj_epilogue(acc, mode, cos_ref, sin_ref, outs[dst], col)


def _proj_kernel(h_ref, g_ref, w_ref, cos_ref, sin_ref, *refs, n_jobs, with_sample):
    refs = list(refs)
    sample_in = [refs.pop(0) for _ in range(3)] if with_sample else []
    src = [refs.pop(0) for _ in range(n_jobs)]
    proj_ref, side_ref = refs.pop(0), refs.pop(0)
    sample_out = [refs.pop(0) for _ in range(2)] if with_sample else []
    dst_refs = [refs.pop(0) for _ in range(n_jobs)]
    xn_ref = refs.pop(0)

    if with_sample:
        @pl.when(pl.program_id(0) == 0)
        def _():
            hs_ref, coss_ref, sins_ref = sample_in
            _proj_rows(hs_ref, g_ref, w_ref, coss_ref, sins_ref, *sample_out, refs[0])

    xn_ref[...] = (_rms_scale(h_ref[...]) * g_ref[...]).astype(BF16)
    outs = {"proj": proj_ref, "side": side_ref}
    n_t = len(PROJ_TILES)
    for t, (dst, col, mode) in enumerate(PROJ_TILES):
        acc = _dot(xn_ref[...], w_ref[:, t * PROJ_TN:(t + 1) * PROJ_TN])
        _proj_epilogue(acc, mode, cos_ref, sin_ref, outs[dst], col)
        if n_jobs >= 2:
            wgu_ref, wd_ref = src[:2]
            wgubf_ref, wdbf_ref = dst_refs[:2]
            gu_cols = wgu_ref.shape[1] // n_t
            for j in range(t * gu_cols // FFN_TF, (t + 1) * gu_cols // FFN_TF):
                c = (j % FFN_TILES) * 2 * FFN_TF + (j // FFN_TILES) * FFN_TF
                wgubf_ref[:, c:c + FFN_TF] = wgu_ref[:, j * FFN_TF:(j + 1) * FFN_TF].astype(BF16)
            d_rows = wd_ref.shape[0] // n_t
            wdbf_ref[t * d_rows:(t + 1) * d_rows] = wd_ref[t * d_rows:(t + 1) * d_rows].astype(BF16)
        if n_jobs == 4:
            win_ref, wo_ref = src[2:]
            winbf_ref, wobf_ref = dst_refs[2:]
            winbf_ref[:, t * PROJ_TN:(t + 1) * PROJ_TN] = win_ref[:, t * PROJ_TN:(t + 1) * PROJ_TN].astype(BF16)
            if t == 0:
                wobf_ref[...] = wo_ref[...].astype(BF16)


def _proj_call(h, g, layer, w_in_bf, cos, sin, tm, convert=(), sample=None):
    m = h.shape[0]
    n_steps = m // tm
    cos_blocks = cos.shape[0] // tm
    n_jobs = len(convert)
    assert n_jobs in (0, 2, 4)
    sample_in, sample_out, sample_shape, sample_scratch = [], [], [], []
    if sample is not None:
        ms = sample[0].shape[0]
        sample_in = [_resident(a.shape, lambda i: (0, 0)) for a in sample]
        sample_out = [_resident((ms, PROJ_COLS), lambda i: (0, 0)), _resident((ms, SIDE_COLS), lambda i: (0, 0))]
        sample_shape = [jax.ShapeDtypeStruct((ms, PROJ_COLS), BF16), jax.ShapeDtypeStruct((ms, SIDE_COLS), F32)]
        sample_scratch = [pltpu.VMEM((ms, D_MODEL), BF16)]
    slab_in, slab_out, slab_shape = [], [], []
    for w, w_layer in convert:
        rows, cols = w.shape[1] // n_steps, w.shape[2]
        assert w.shape[1] % n_steps == 0 and rows % 16 == 0
        slab_in.append(pl.BlockSpec((None, rows, cols), lambda i, w_layer=w_layer: (w_layer, i, 0)))
        slab_out.append(pl.BlockSpec((rows, cols), lambda i: (i, 0)))
        slab_shape.append(jax.ShapeDtypeStruct(w.shape[1:], BF16))
    return pl.pallas_call(
        functools.partial(_proj_kernel, n_jobs=n_jobs, with_sample=sample is not None),
        grid=(n_steps,),
        in_specs=[
            pl.BlockSpec((tm, D_MODEL), lambda i: (i, 0)),
            _resident((None, 1, D_MODEL), lambda i: (layer, 0, 0)),
            _resident((D_MODEL, IN_COLS), lambda i: (0, 0)),
            pl.BlockSpec((tm, RET_KEY_DIM // 2), lambda i: (i % cos_blocks, 0)),
            pl.BlockSpec((tm, RET_KEY_DIM // 2), lambda i: (i % cos_blocks, 0)),
        ] + sample_in + slab_in,
        out_specs=[
            pl.BlockSpec((tm, PROJ_COLS), lambda i: (i, 0)),
            pl.BlockSpec((tm, SIDE_COLS), lambda i: (i, 0)),
        ] + sample_out + slab_out,
        out_shape=[jax.ShapeDtypeStruct((m, PROJ_COLS), BF16),
                   jax.ShapeDtypeStruct((m, SIDE_COLS), F32)] + sample_shape + slab_shape,
        scratch_shapes=[pltpu.VMEM((tm, D_MODEL), BF16)] + sample_scratch,
        compiler_params=pltpu.CompilerParams(
            dimension_semantics=("arbitrary",), vmem_limit_bytes=VMEM_LIMIT_BYTES),
        name="proj",
    )(h, g, w_in_bf, cos, sin, *(sample or ()), *[w for w, _ in convert])


def _proj_cast_kernel(h_ref, g_ref, w_ref, cos_ref, sin_ref, proj_ref, side_ref, wbf_ref, xn_ref):
    j = pl.program_id(0)

    @pl.when(j == 0)
    def _():
        xn_ref[...] = (_rms_scale(h_ref[...]) * g_ref[...]).astype(BF16)

    w_bf = w_ref[...].astype(BF16)
    wbf_ref[...] = w_bf
    acc = _dot(xn_ref[...], w_bf)
    outs = {"proj": proj_ref, "side": side_ref}
    for t, (dst, col, mode) in enumerate(PROJ_TILES):
        @pl.when(j == t)
        def _():
            _proj_epilogue(acc, mode, cos_ref, sin_ref, outs[dst], col)


def _proj_cast_call(h, g, layer, w_in, cos, sin):
    m = h.shape[0]
    full = lambda shape: _resident(shape, lambda j: (0,) * len(shape))
    return pl.pallas_call(
        _proj_cast_kernel,
        grid=(len(PROJ_TILES),),
        in_specs=[
            full((m, D_MODEL)),
            _resident((None, 1, D_MODEL), lambda j: (layer, 0, 0)),
            pl.BlockSpec((None, D_MODEL, PROJ_TN), lambda j: (layer, 0, j)),
            full((m, RET_KEY_DIM // 2)),
            full((m, RET_KEY_DIM // 2)),
        ],
        out_specs=[
            pl.BlockSpec((m, PROJ_COLS), lambda j: (0, 0)),
            pl.BlockSpec((m, SIDE_COLS), lambda j: (0, 0)),
            pl.BlockSpec((D_MODEL, PROJ_TN), lambda j: (0, j)),
        ],
        out_shape=[jax.ShapeDtypeStruct((m, PROJ_COLS), BF16),
                   jax.ShapeDtypeStruct((m, SIDE_COLS), F32),
                   jax.ShapeDtypeStruct((D_MODEL, IN_COLS), BF16)],
        scratch_shapes=[pltpu.VMEM((m, D_MODEL), BF16)],
        compiler_params=pltpu.CompilerParams(
            dimension_semantics=("arbitrary",), vmem_limit_bytes=VMEM_LIMIT_BYTES),
        name="proj_cast",
    )(h, g, w_in, cos, sin)


def _softmax_sink(s, sink):
    m = jnp.maximum(jnp.max(s, axis=-1, keepdims=True), sink)
    p = jnp.exp(s - m)
    return p, 1.0 / (jnp.sum(p, axis=-1, keepdims=True) + jnp.exp(sink - m))


def _gate_out(o, g):
    return (_rms_scale(o) * _silu(g)).astype(BF16)


WO_BATCHES = 1


def _mix_prompt_kernel(sink_ref, cdec_ref, p_ref, s_ref,
                       bias_ref, dmask_ref, qdec_ref, kdec_ref, wo_ref, h_ref, *refs, with_sample):
    refs = list(refs)
    sample_in = [refs.pop(0) for _ in range(2)] if with_sample else []
    out_ref, st_ref = refs.pop(0), refs.pop(0)
    sample_out = [refs.pop(0)] if with_sample else []
    kvp_ref, cat_refs = refs[0], refs[1:]
    n = pl.program_id(0)

    @pl.when(n == 0)
    def _():
        st_ref[...] = jnp.zeros(st_ref.shape, F32)
        kvp_ref[...] = jnp.zeros(kvp_ref.shape, BF16)
        if with_sample:
            cats_ref, hs_ref = sample_in
            sample_out[0][...] = hs_ref[...] + _dot(cats_ref[...], wo_ref[...])

    scale = SWA_HEAD_DIM ** -0.5
    v_off = SWA_KV_HEADS * SWA_HEAD_DIM

    def cat_of(b):
        return cat_refs[b // WO_BATCHES].at[b % WO_BATCHES]

    def wo_rows(b0):
        bs = slice(b0, b0 + WO_BATCHES)
        cat = cat_refs[b0 // WO_BATCHES][...].reshape(WO_BATCHES * BLOCK, D_MODEL)
        out_ref[bs] = h_ref[bs] + _dot(cat, wo_ref[...]).reshape(WO_BATCHES, BLOCK, D_MODEL)

    def attention(b, kh):
        c0 = kh * SWA_HEAD_DIM
        k_cur = s_ref[b, :, KV_COL + c0:KV_COL + c0 + SWA_HEAD_DIM].astype(BF16)
        v_cur = s_ref[b, :, KV_COL + v_off + c0:KV_COL + v_off + c0 + SWA_HEAD_DIM].astype(BF16)
        k_cat = jnp.concatenate([kvp_ref[b, :, c0:c0 + SWA_HEAD_DIM], k_cur], axis=0)
        v_cat = jnp.concatenate([kvp_ref[b, :, v_off + c0:v_off + c0 + SWA_HEAD_DIM], v_cur], axis=0)
        kvp_ref[b, :, c0:c0 + SWA_HEAD_DIM] = k_cur
        kvp_ref[b, :, v_off + c0:v_off + c0 + SWA_HEAD_DIM] = v_cur
        for gq in range(SWA_GROUP):
            h = kh * SWA_GROUP + gq
            q = p_ref[b, :, h * SWA_HEAD_DIM:(h + 1) * SWA_HEAD_DIM]
            s = _dot_nt(q, k_cat) * scale + bias_ref[h]
            p, inv = _softmax_sink(s, sink_ref[h])
            o = _dot(p.astype(BF16), v_cat) * inv
            cat_of(b)[:, h * SWA_HEAD_DIM:(h + 1) * SWA_HEAD_DIM] = o.astype(BF16)

    def retention(b, h):
        cs = slice(h * RET_KEY_DIM, (h + 1) * RET_KEY_DIM)
        q = p_ref[b, :, QR_COL + h * RET_KEY_DIM:QR_COL + (h + 1) * RET_KEY_DIM]
        k32 = s_ref[b, :, cs]
        v = p_ref[b, :, VR_COL + h * RET_VAL_DIM:VR_COL + (h + 1) * RET_VAL_DIM]
        s = _dot_nt(q, k32.astype(BF16)) * dmask_ref[h]
        o = _dot(s.astype(BF16), v)
        st = st_ref[b, h]
        o = o + _dot(q, st.astype(BF16)) * qdec_ref[:, cs]
        kd = (k32 * kdec_ref[:, cs]).astype(BF16)
        st_ref[b, h] = cdec_ref[h] * st + _dot_tn(kd, v)
        cat_of(b)[:, SWA_WIDTH + h * RET_VAL_DIM:SWA_WIDTH + (h + 1) * RET_VAL_DIM] = (
            _gate_out(o, s_ref[b, :, G_COL + h * RET_VAL_DIM:G_COL + (h + 1) * RET_VAL_DIM]))

    for b in range(BATCH):
        for kh in range(SWA_KV_HEADS):
            attention(b, kh)
        for h in range(RET_HEADS):
            retention(b, h)
        if (b + 1) % WO_BATCHES == 0:
            wo_rows(b + 1 - WO_BATCHES)


def _mix_prompt_call(proj, side, h, w_o_bf, sinks_l, cdec, bias, dmask, qdec, kdec, sample=None):
    nblk = SEQ // BLOCK
    proj3 = proj.reshape(BATCH, SEQ, PROJ_COLS)
    side3 = side.reshape(BATCH, SEQ, SIDE_COLS)
    h3 = h.reshape(BATCH, SEQ, D_MODEL)
    smem = pl.BlockSpec(memory_space=pltpu.SMEM)
    full = lambda shape: _resident(shape, lambda n: (0,) * len(shape))
    sample_in, sample_out, sample_shape = [], [], []
    if sample is not None:
        sample_in = [full(a.shape) for a in sample]
        sample_out = [full(sample[1].shape)]
        sample_shape = [jax.ShapeDtypeStruct(sample[1].shape, F32)]
    out, st, *hs_new = pl.pallas_call(
        functools.partial(_mix_prompt_kernel, with_sample=sample is not None),
        grid=(nblk,),
        in_specs=[
            smem, smem,
            pl.BlockSpec((BATCH, BLOCK, PROJ_COLS), lambda n: (0, n, 0)),
            pl.BlockSpec((BATCH, BLOCK, SIDE_COLS), lambda n: (0, n, 0)),
            pl.BlockSpec((None, SWA_HEADS, BLOCK, 2 * BLOCK), lambda n: (jnp.minimum(n, 1), 0, 0, 0)),
            full((RET_HEADS, RET_CHUNK, RET_CHUNK)),
            full((RET_CHUNK, RET_WIDTH)),
            full((RET_CHUNK, RET_WIDTH)),
            full((D_MODEL, D_MODEL)),
            pl.BlockSpec((BATCH, BLOCK, D_MODEL), lambda n: (0, n, 0)),
        ] + sample_in,
        out_specs=[
            pl.BlockSpec((BATCH, BLOCK, D_MODEL), lambda n: (0, n, 0)),
            _resident((BATCH, RET_HEADS, RET_KEY_DIM, RET_VAL_DIM), lambda n: (0, 0, 0, 0)),
        ] + sample_out,
        out_shape=[jax.ShapeDtypeStruct((BATCH, SEQ, D_MODEL), F32),
                   jax.ShapeDtypeStruct((BATCH, RET_HEADS, RET_KEY_DIM, RET_VAL_DIM), F32)] + sample_shape,
        scratch_shapes=[pltpu.VMEM((BATCH, BLOCK, 2 * SWA_KV_HEADS * SWA_HEAD_DIM), BF16)]
        + [pltpu.VMEM((WO_BATCHES, BLOCK, D_MODEL), BF16)] * (BATCH // WO_BATCHES),
        compiler_params=pltpu.CompilerParams(
            dimension_semantics=("arbitrary",), vmem_limit_bytes=VMEM_LIMIT_BYTES),
        name="mix_prompt",
    )(sinks_l, cdec, proj3, side3, bias, dmask, qdec, kdec, w_o_bf, h3, *(sample or ()))
    return (out.reshape(BATCH * SEQ, D_MODEL), st, *hs_new)


SAMPLE_BB = 4
SAMPLE_ROWS = SAMPLE_BB * DEC_SEQ
N_SAMPLE = DEC_BATCH * DEC_SEQ
CACHE_ROWS = WINDOW * SWA_KV_HEADS


def _mix_sample_kernel(sink_ref, cdec_ref, prow_ref, srow_ref, pall_ref, sall_ref, ck_ref, cv_ref,
                       knew_ref, vnew_ref, st_ref, bias_ref, dmask_ref, qdec_ref, kdec_ref, *rest, layer):
    cat_ref, stout_ref, ckout_ref, cvout_ref = rest[-4:]
    if layer == 0:
        for ref in (stout_ref, ckout_ref, cvout_ref):
            ref[1:] = jnp.zeros((DEPTH - 1,) + ref.shape[1:], F32)
        stout_ref, ckout_ref, cvout_ref = stout_ref.at[0], ckout_ref.at[0], cvout_ref.at[0]
    r = SAMPLE_ROWS
    row_b = lax.broadcasted_iota(jnp.int32, (r, 1), 0) // DEC_SEQ
    row_b4 = lax.broadcasted_iota(jnp.int32, (SWA_GROUP * r, 1), 0) % r // DEC_SEQ
    scale = SWA_HEAD_DIM ** -0.5
    v_off = SWA_KV_HEADS * SWA_HEAD_DIM
    new_rows = DEC_SEQ * SWA_KV_HEADS

    for bi in range(SAMPLE_BB):
        ckout_ref[bi, :CACHE_ROWS - new_rows] = ck_ref[bi, new_rows:]
        ckout_ref[bi, CACHE_ROWS - new_rows:] = knew_ref[bi]
        cvout_ref[bi, :CACHE_ROWS - new_rows] = cv_ref[bi, new_rows:]
        cvout_ref[bi, CACHE_ROWS - new_rows:] = vnew_ref[bi]

    for kh in range(SWA_KV_HEADS):
        c0 = kh * SWA_HEAD_DIM
        q4 = jnp.concatenate(
            [prow_ref[:, (kh * SWA_GROUP + gq) * SWA_HEAD_DIM:(kh * SWA_GROUP + gq + 1) * SWA_HEAD_DIM]
             for gq in range(SWA_GROUP)], axis=0)
        k_new = sall_ref[:, KV_COL + c0:KV_COL + c0 + SWA_HEAD_DIM].astype(BF16)
        v_new = sall_ref[:, KV_COL + v_off + c0:KV_COL + v_off + c0 + SWA_HEAD_DIM].astype(BF16)
        s_cache = jnp.zeros((SWA_GROUP * r, CACHE_ROWS), F32)
        for bi in range(SAMPLE_BB):
            s_cache = jnp.where(row_b4 == bi, _dot_nt(q4, ck_ref[bi].astype(BF16)), s_cache)
        s_new = _dot_nt(q4, k_new)
        s4 = jnp.concatenate([s_cache, s_new], axis=1) * scale
        p_parts, inv_parts = [], []
        for gq in range(SWA_GROUP):
            h = kh * SWA_GROUP + gq
            s = s4[gq * r:(gq + 1) * r] + bias_ref[kh, h]
            p, inv = _softmax_sink(s, sink_ref[h])
            p_parts.append(p)
            inv_parts.append(inv)
        p4 = jnp.concatenate(p_parts, axis=0)
        inv4 = jnp.concatenate(inv_parts, axis=0)
        p_cache = p4[:, :CACHE_ROWS]
        o4 = _dot(p4[:, CACHE_ROWS:].astype(BF16), v_new)
        for bi in range(SAMPLE_BB):
            o4 = o4 + _dot(jnp.where(row_b4 == bi, p_cache, 0.0).astype(BF16), cv_ref[bi].astype(BF16))
        o4 = o4 * inv4
        for gq in range(SWA_GROUP):
            h = kh * SWA_GROUP + gq
            cat_ref[:, h * SWA_HEAD_DIM:(h + 1) * SWA_HEAD_DIM] = o4[gq * r:(gq + 1) * r].astype(BF16)

    for h in range(RET_HEADS):
        cs = slice(h * RET_KEY_DIM, (h + 1) * RET_KEY_DIM)
        q = prow_ref[:, 1024 + h * RET_KEY_DIM:1024 + (h + 1) * RET_KEY_DIM]
        k_all = sall_ref[:, cs].astype(BF16)
        v_all = pall_ref[:, 2048 + h * RET_VAL_DIM:2048 + (h + 1) * RET_VAL_DIM]
        s = _dot_nt(q, k_all) * dmask_ref[h]
        o = _dot(s.astype(BF16), v_all)
        k32 = srow_ref[:, cs]
        v = prow_ref[:, 2048 + h * RET_VAL_DIM:2048 + (h + 1) * RET_VAL_DIM]
        kd = k32 * kdec_ref[:, cs]
        cross = jnp.zeros((r, RET_VAL_DIM), F32)
        for bi in range(SAMPLE_BB):
            st = st_ref[bi, h]
            cross = jnp.where(row_b == bi, _dot(q, st.astype(BF16)), cross)
            kd_b = jnp.where(row_b == bi, kd, 0.0).astype(BF16)
            stout_ref[bi, h] = cdec_ref[h] * st + _dot_tn(kd_b, v)
        o = o + cross * qdec_ref[:, cs]
        g = srow_ref[:, 1024 + h * RET_VAL_DIM:1024 + (h + 1) * RET_VAL_DIM]
        cat_ref[:, SWA_WIDTH + h * RET_VAL_DIM:SWA_WIDTH + (h + 1) * RET_VAL_DIM] = _gate_out(o, g)


def _mix_sample_call(proj, side, cache_k, cache_v, state_ret, layer, sinks_l, cdec, bias, dmask, qdec, kdec,
                     stacked):
    r = SAMPLE_ROWS
    new_rows = DEC_SEQ * SWA_KV_HEADS
    kv_w = SWA_KV_HEADS * SWA_HEAD_DIM
    k_new = side[:, KV_COL:KV_COL + kv_w].reshape(DEC_BATCH, new_rows, SWA_HEAD_DIM)
    v_new = side[:, KV_COL + kv_w:KV_COL + 2 * kv_w].reshape(DEC_BATCH, new_rows, SWA_HEAD_DIM)
    smem = pl.BlockSpec(memory_space=pltpu.SMEM)
    full = lambda shape: pl.BlockSpec(shape, lambda c: (0,) * len(shape))
    cache_in = pl.BlockSpec((None, SAMPLE_BB, CACHE_ROWS, SWA_HEAD_DIM), lambda c: (layer, c, 0, 0))
    state_dims = (SAMPLE_BB, RET_HEADS, RET_KEY_DIM, RET_VAL_DIM)
    cache_dims = (SAMPLE_BB, CACHE_ROWS, SWA_HEAD_DIM)
    if layer == 0:
        out_block = lambda dims: pl.BlockSpec((DEPTH,) + dims, lambda c: (0, c) + (0,) * (len(dims) - 1))
    else:
        out_block = lambda dims: pl.BlockSpec((None,) + dims, lambda c: (layer, c) + (0,) * (len(dims) - 1))
    new_in = pl.BlockSpec((SAMPLE_BB, new_rows, SWA_HEAD_DIM), lambda c: (c, 0, 0))
    in_specs = [
        smem, smem,
        pl.BlockSpec((r, PROJ_COLS), lambda c: (c, 0)),
        pl.BlockSpec((r, SIDE_COLS), lambda c: (c, 0)),
        full((N_SAMPLE, PROJ_COLS)),
        full((N_SAMPLE, SIDE_COLS)),
        cache_in, cache_in, new_in, new_in,
        pl.BlockSpec((None, SAMPLE_BB, RET_HEADS, RET_KEY_DIM, RET_VAL_DIM), lambda c: (layer, c, 0, 0, 0)),
        pl.BlockSpec((SWA_KV_HEADS, SWA_HEADS, r, CACHE_ROWS + N_SAMPLE), lambda c: (0, 0, c, 0)),
        pl.BlockSpec((RET_HEADS, r, N_SAMPLE), lambda c: (0, c, 0)),
        pl.BlockSpec((r, RET_WIDTH), lambda c: (c, 0)),
        pl.BlockSpec((r, RET_WIDTH), lambda c: (c, 0)),
    ]
    args = [sinks_l, cdec, proj, side, proj, side, cache_k, cache_v, k_new, v_new, state_ret, bias, dmask, qdec,
            kdec]
    aliases = {}
    if stacked is not None:
        for k, buf in enumerate(stacked):
            aliases[len(args)] = 1 + k
            in_specs.append(pl.BlockSpec(memory_space=pl.ANY))
            args.append(buf)
    cache_shape = jax.ShapeDtypeStruct((DEPTH, DEC_BATCH, CACHE_ROWS, SWA_HEAD_DIM), F32)
    cat, *new_stacked = pl.pallas_call(
        functools.partial(_mix_sample_kernel, layer=layer),
        grid=(DEC_BATCH // SAMPLE_BB,),
        in_specs=in_specs,
        out_specs=[
            pl.BlockSpec((r, D_MODEL), lambda c: (c, 0)),
            out_block(state_dims), out_block(cache_dims), out_block(cache_dims),
        ],
        out_shape=[jax.ShapeDtypeStruct((N_SAMPLE, D_MODEL), BF16),
                   jax.ShapeDtypeStruct((DEPTH, DEC_BATCH, RET_HEADS, RET_KEY_DIM, RET_VAL_DIM), F32),
                   cache_shape, cache_shape],
        input_output_aliases=aliases,
        compiler_params=pltpu.CompilerParams(
            dimension_semantics=("arbitrary",), vmem_limit_bytes=VMEM_LIMIT_BYTES),
        name="mix_sample",
    )(*args)
    return cat, tuple(new_stacked)


WO_TN = 512


def _wo_cast_kernel(cat_ref, w_ref, h_ref, out_ref, wbf_ref):
    w_bf = w_ref[...].astype(BF16)
    wbf_ref[...] = w_bf
    out_ref[...] = h_ref[...] + _dot(cat_ref[...], w_bf)


def _wo_cast_call(cat, w_o, layer, h):
    m = h.shape[0]
    return pl.pallas_call(
        _wo_cast_kernel,
        grid=(D_MODEL // WO_TN,),
        in_specs=[
            _resident((m, D_MODEL), lambda j: (0, 0)),
            pl.BlockSpec((None, D_MODEL, WO_TN), lambda j: (layer, 0, j)),
            pl.BlockSpec((m, WO_TN), lambda j: (0, j)),
        ],
        out_specs=[pl.BlockSpec((m, WO_TN), lambda j: (0, j)),
                   pl.BlockSpec((D_MODEL, WO_TN), lambda j: (0, j))],
        out_shape=[jax.ShapeDtypeStruct((m, D_MODEL), F32),
                   jax.ShapeDtypeStruct((D_MODEL, D_MODEL), BF16)],
        compiler_params=pltpu.CompilerParams(
            dimension_semantics=("arbitrary",), vmem_limit_bytes=VMEM_LIMIT_BYTES),
        name="wo_cast",
    )(cat, w_o, h)


def _ffn_tile(f, n_f, h_ref, g_ref, wgu_ref, wd_ref, gfin_ref, out_ref, xn_ref, final_norm):
    @pl.when(f == 0)
    def _():
        x = h_ref[...]
        xn_ref[...] = (_rms_scale(x) * g_ref[...]).astype(BF16)
        out_ref[...] = x

    ab = _dot(xn_ref[...], wgu_ref[...])
    act = (_silu(ab[:, :FFN_TF]) * ab[:, FFN_TF:]).astype(BF16)
    out_ref[...] += _dot(act, wd_ref[...])

    if final_norm:
        @pl.when(f == n_f - 1)
        def _():
            out_ref[...] = _rms_scale(out_ref[...]) * gfin_ref[...]


def _ffn_kernel(h_ref, g_ref, wgu_ref, wd_ref, gfin_ref, hs_ref, out_ref, outs_ref, xn_ref, xns_ref, *,
                final_norm):
    i, f, n_f = pl.program_id(0), pl.program_id(1), pl.num_programs(1)
    _ffn_tile(f, n_f, h_ref, g_ref, wgu_ref, wd_ref, gfin_ref, out_ref, xn_ref, final_norm)

    @pl.when(i == 0)
    def _():
        _ffn_tile(f, n_f, hs_ref, g_ref, wgu_ref, wd_ref, gfin_ref, outs_ref, xns_ref, final_norm)


FFN_VMEM_LIMIT_BYTES = 62 * 1024 * 1024


def _ffn_call(h, h_sample, g, w_gu_bf, w_d_bf, g_final, layer, tm):
    m, ms = h.shape[0], h_sample.shape[0]
    return pl.pallas_call(
        functools.partial(_ffn_kernel, final_norm=(layer == DEPTH - 1)),
        grid=(m // tm, FFN_TILES),
        in_specs=[
            pl.BlockSpec((tm, D_MODEL), lambda i, f: (i, 0)),
            pl.BlockSpec((None, 1, D_MODEL), lambda i, f: (layer, 0, 0)),
            pl.BlockSpec((D_MODEL, 2 * FFN_TF), lambda i, f: (0, f)),
            pl.BlockSpec((FFN_TF, D_MODEL), lambda i, f: (f, 0)),
            pl.BlockSpec((1, D_MODEL), lambda i, f: (0, 0)),
            _resident((ms, D_MODEL), lambda i, f: (0, 0)),
        ],
        out_specs=[pl.BlockSpec((tm, D_MODEL), lambda i, f: (i, 0)),
                   _resident((ms, D_MODEL), lambda i, f: (0, 0))],
        out_shape=[jax.ShapeDtypeStruct((m, D_MODEL), F32),
                   jax.ShapeDtypeStruct((ms, D_MODEL), F32)],
        scratch_shapes=[pltpu.VMEM((tm, D_MODEL), BF16), pltpu.VMEM((ms, D_MODEL), BF16)],
        compiler_params=pltpu.CompilerParams(
            dimension_semantics=("arbitrary", "arbitrary"), vmem_limit_bytes=FFN_VMEM_LIMIT_BYTES),
        name="ffn",
    )(h, g, w_gu_bf, w_d_bf, g_final, h_sample)


def _rope_tables(pos):
    half = RET_KEY_DIM // 2
    inv = 1.0 / (ROPE_BASE ** jnp.linspace(0.0, 1.0, half, dtype=F32))
    ang = pos.astype(F32)[:, None] * inv[None, :]
    return jnp.cos(ang), jnp.sin(ang)


def _decay_tables(c):
    lg = jnp.log(1.0 - jnp.exp2(-5.0 - jnp.arange(RET_HEADS, dtype=F32)))
    idx = jnp.arange(c, dtype=F32)
    diff = idx[:, None] - idx[None, :]
    dmask = jnp.where(diff[None] >= 0, jnp.exp(jnp.maximum(diff, 0.0)[None] * lg[:, None, None]), 0.0)
    q_decay = jnp.exp((idx[:, None] + 1.0) * lg[None, :])
    k_decay = jnp.exp((c - 1.0 - idx)[:, None] * lg[None, :])
    c_decay = jnp.exp(c * lg)
    return dmask, q_decay, k_decay, c_decay


def _per_head_cols(t):
    return jnp.repeat(t, RET_KEY_DIM, axis=1)


def kernel(x_prompt, x_sample, cache_k_win, cache_v_win, state_ret, rel_bias, w_in, sinks, w_o,
           norm_mix, norm_ffn, w_gate_up, w_down, norm_final):
    norm_mix3 = norm_mix.reshape(DEPTH, 1, D_MODEL)
    norm_ffn3 = norm_ffn.reshape(DEPTH, 1, D_MODEL)
    norm_final2 = norm_final.reshape(1, D_MODEL)
    cache_k = cache_k_win.reshape(DEPTH, DEC_BATCH, CACHE_ROWS, SWA_HEAD_DIM)
    cache_v = cache_v_win.reshape(DEPTH, DEC_BATCH, CACHE_ROWS, SWA_HEAD_DIM)

    cos_p, sin_p = _rope_tables(jnp.arange(SEQ, dtype=jnp.int32))
    cos_4, sin_4 = _rope_tables(PAST_LEN + jnp.arange(DEC_SEQ, dtype=jnp.int32))
    cos_s = jnp.tile(cos_4, (DEC_BATCH, 1))
    sin_s = jnp.tile(sin_4, (DEC_BATCH, 1))

    dmask_p, qd_p, kd_p, cdec_p = _decay_tables(RET_CHUNK)
    qdec_p = _per_head_cols(qd_p)
    kdec_p = _per_head_cols(kd_p)
    dmask_4, qd_4, kd_4, cdec_s = _decay_tables(DEC_SEQ)
    eye_b = jnp.eye(DEC_BATCH, dtype=F32)
    dmask_s = jax.vmap(lambda d: jnp.kron(eye_b, d))(dmask_4)
    qdec_s = jnp.tile(_per_head_cols(qd_4), (DEC_BATCH, 1))
    kdec_s = jnp.tile(_per_head_cols(kd_4), (DEC_BATCH, 1))

    qi = np.arange(BLOCK)[:, None]
    kj = np.arange(2 * BLOCK)[None, :]
    delta_p = qi + BLOCK - kj
    in_window = (delta_p >= 0) & (delta_p < WINDOW)
    valid_p = np.stack([in_window & (kj >= BLOCK), in_window]).astype(np.float32)
    rows = np.arange(N_SAMPLE)
    rb, rt = rows // DEC_SEQ, rows % DEC_SEQ
    cache_pos = np.arange(CACHE_ROWS) // SWA_KV_HEADS
    cache_head = np.arange(CACHE_ROWS) % SWA_KV_HEADS
    delta_cache = (WINDOW + rt)[:, None] - cache_pos[None, :]
    delta_new = rt[:, None] - rt[None, :]
    same_b = rb[:, None] == rb[None, :]
    delta_s = np.concatenate([delta_cache, delta_new], axis=1)
    valid_s = np.stack([
        np.concatenate([(delta_cache < WINDOW) & (cache_head == kh)[None, :], same_b & (delta_new >= 0)], axis=1)
        for kh in range(SWA_KV_HEADS)]).astype(np.float32)
    rel_bias_t = rel_bias.T
    bias_p = _expand_bias(rel_bias_t, jnp.asarray(_t5_bucket_np(delta_p)), jnp.asarray(valid_p))
    bias_s = _expand_bias(rel_bias_t, jnp.asarray(_t5_bucket_np(delta_s)), jnp.asarray(valid_s))

    hp = x_prompt.reshape(BATCH * SEQ, D_MODEL)
    hs = x_sample.reshape(N_SAMPLE, D_MODEL)
    tm_proj, tm_ffn = 256, 1024
    kp_new, vp_new, rp_new = [], [], []
    sample_new = None
    kv_w = SWA_KV_HEADS * SWA_HEAD_DIM
    for l in range(DEPTH):
        convert = [(w_gate_up, l), (w_down, l)]
        if l + 1 < DEPTH:
            convert += [(w_in, l + 1), (w_o, l + 1)]
        if l == 0:
            proj_s, side_s, w_in_bf = _proj_cast_call(hs, norm_mix3, l, w_in, cos_s, sin_s)
            proj, side, w_gu_bf, w_d_bf, *next_bf = _proj_call(hp, norm_mix3, l, w_in_bf, cos_p, sin_p, tm_proj,
                                                                convert)
        else:
            proj, side, proj_s, side_s, w_gu_bf, w_d_bf, *next_bf = _proj_call(
                hp, norm_mix3, l, w_in_bf, cos_p, sin_p, tm_proj, convert, sample=(hs, cos_s, sin_s))

        cat, sample_new = _mix_sample_call(proj_s, side_s, cache_k, cache_v, state_ret, l, sinks[l], cdec_s,
                                           bias_s, dmask_s, qdec_s, kdec_s, sample_new)
        if l == 0:
            hs, w_o_bf = _wo_cast_call(cat, w_o, l, hs)
            hp, st_p = _mix_prompt_call(proj, side, hp, w_o_bf, sinks[l], cdec_p, bias_p, dmask_p, qdec_p, kdec_p)
        else:
            hp, st_p, hs = _mix_prompt_call(proj, side, hp, w_o_bf, sinks[l], cdec_p, bias_p, dmask_p, qdec_p,
                                            kdec_p, sample=(cat, hs))
        hp, hs = _ffn_call(hp, hs, norm_ffn3, w_gu_bf, w_d_bf, norm_final2, l, tm_ffn)
        kv_tail = side.reshape(BATCH, SEQ, SIDE_COLS)[:, SEQ - WINDOW:, KV_COL:KV_COL + 2 * kv_w]
        kp_new.append(kv_tail[..., :kv_w].reshape(BATCH, WINDOW, SWA_KV_HEADS, SWA_HEAD_DIM))
        vp_new.append(kv_tail[..., kv_w:].reshape(BATCH, WINDOW, SWA_KV_HEADS, SWA_HEAD_DIM))
        rp_new.append(st_p)
        if next_bf:
            w_in_bf, w_o_bf = next_bf

    y_prompt = hp.reshape(BATCH, SEQ, D_MODEL)
    y_sample = hs.reshape(DEC_BATCH, DEC_SEQ, D_MODEL)
    rs_new, ks_new, vs_new = sample_new
    return (y_prompt, y_sample,
            jnp.stack(kp_new), jnp.stack(vp_new), jnp.stack(rp_new),
            ks_new.reshape(cache_k_win.shape), vs_new.reshape(cache_v_win.shape), rs_new)
```

```python
import functools
import math

import numpy as np
import jax
import jax.numpy as jnp
from jax import lax
from jax.experimental import pallas as pl
from jax.experimental.pallas import tpu as pltpu

D_MODEL = 2048
BATCH = 4
SEQ = 2048
DEPTH = 4
DEC_BATCH = 32
DEC_SEQ = 4
PAST_LEN = 16384

SWA_WIDTH = 1024
RET_WIDTH = 1024
SWA_HEADS = 8
SWA_KV_HEADS = 2
SWA_GROUP = SWA_HEADS // SWA_KV_HEADS
SWA_HEAD_DIM = 128
WINDOW = 128
BLOCK = WINDOW
RET_HEADS = 4
RET_KEY_DIM = 256
RET_VAL_DIM = 256
RET_CHUNK = 128
ROPE_BASE = 10000.0
N_BUCKETS = 32
MAX_DISTANCE = 128
EPS = 1e-6
D_FF = 5632
IN_COLS = 5632

F32 = jnp.float32
BF16 = jnp.bfloat16

VMEM_LIMIT_BYTES = 60 * 1024 * 1024

FFN_TF = 512
FFN_TILES = D_FF // FFN_TF
PROJ_TN = 512
PROJ_COLS = 3072
SIDE_COLS = 2560
QR_COL, VR_COL = 1024, 2048
G_COL, KV_COL = 1024, 2048
PROJ_TILES = (
    ("proj", 0, "plain"), ("proj", 512, "plain"),
    ("side", KV_COL, "plain"),
    ("proj", 1024, "rotary"), ("proj", 1536, "rotary"),
    ("side", 0, "rotary_k"), ("side", 512, "rotary_k"),
    ("proj", 2048, "plain"), ("proj", 2560, "plain"),
    ("side", 1024, "plain"), ("side", 1536, "plain"),
)


def _rms_scale(x):
    return x * lax.rsqrt(jnp.mean(x * x, axis=-1, keepdims=True) + EPS)


def _silu(x):
    return x * jax.nn.sigmoid(x)


def _dot(a, b):
    return jnp.dot(a, b, preferred_element_type=F32)


def _dot_nt(a, b):
    return lax.dot_general(a, b, (((1,), (1,)), ((), ())), preferred_element_type=F32)


def _dot_tn(a, b):
    return lax.dot_general(a, b, (((0,), (0,)), ((), ())), preferred_element_type=F32)


def _resident(block_shape, index_map):
    return pl.BlockSpec(block_shape, index_map, pipeline_mode=pl.Buffered(1))


MASKED = -1e30


def _bias_kernel(rbt_ref, idx_ref, valid_ref, out_ref):
    h = pl.program_id(0)
    idx = idx_ref[...]
    acc = jnp.zeros(idx.shape, F32)
    for b in range(N_BUCKETS):
        acc = jnp.where(idx == b, rbt_ref[h, b], acc)
    for v in range(valid_ref.shape[0]):
        out_ref[v] = jnp.where(valid_ref[v] > 0.5, acc, MASKED)


def _expand_bias(rel_bias_t, bucket_idx, valid):
    nv, rows, cols = valid.shape
    return pl.pallas_call(
        _bias_kernel,
        grid=(SWA_HEADS,),
        in_specs=[pl.BlockSpec(memory_space=pltpu.SMEM),
                  pl.BlockSpec((rows, cols), lambda h: (0, 0)),
                  pl.BlockSpec((nv, rows, cols), lambda h: (0, 0, 0))],
        out_specs=pl.BlockSpec((nv, None, rows, cols), lambda h: (0, h, 0, 0)),
        out_shape=jax.ShapeDtypeStruct((nv, SWA_HEADS, rows, cols), F32),
        name="bias_expand",
    )(rel_bias_t, bucket_idx, valid)


def _t5_bucket_np(delta):
    n = np.maximum(delta, 0)
    max_exact = N_BUCKETS // 2
    nf = np.maximum(n, 1).astype(np.float64)
    large = max_exact + (np.log(nf / max_exact) / math.log(MAX_DISTANCE / max_exact)
                         * (N_BUCKETS - max_exact)).astype(np.int32)
    large = np.minimum(large, N_BUCKETS - 1)
    return np.where(n < max_exact, n, large).astype(np.int32)


def _proj_epilogue(acc, mode, cos_ref, sin_ref, out_ref, col):
    if mode == "plain":
        out_ref[:, col:col + PROJ_TN] = acc.astype(out_ref.dtype)
        return
    half = RET_KEY_DIM // 2
    cos = cos_ref[...]
    sin = sin_ref[...]
    for c0 in range(0, PROJ_TN, RET_KEY_DIM):
        x1 = acc[:, c0:c0 + half]
        x2 = acc[:, c0 + half:c0 + RET_KEY_DIM]
        o1 = x1 * cos - x2 * sin
        o2 = x1 * sin + x2 * cos
        if mode == "rotary_k":
            o1 = o1 * (RET_KEY_DIM ** -0.5)
            o2 = o2 * (RET_KEY_DIM ** -0.5)
        out_ref[:, col + c0:col + c0 + half] = o1.astype(out_ref.dtype)
        out_ref[:, col + c0 + half:col + c0 + RET_KEY_DIM] = o2.astype(out_ref.dtype)


def _proj_rows(h_ref, g_ref, w_ref, cos_ref, sin_ref, proj_ref, side_ref, xn_ref):
    xn_ref[...] = (_rms_scale(h_ref[...]) * g_ref[...]).astype(BF16)
    outs = {"proj": proj_ref, "side": side_ref}
    for t, (dst, col, mode) in enumerate(PROJ_TILES):
        acc = _dot(xn_ref[...], w_ref[:, t * PROJ_TN:(t + 1) * PROJ_TN])
        _proj_epilogue(acc, mode, cos_ref, sin_ref, outs[dst], col)


def _proj_kernel(h_ref, g_ref, w_ref, cos_ref, sin_ref, *refs, n_jobs, with_sample):
    refs = list(refs)
    sample_in = [refs.pop(0) for _ in range(3)] if with_sample else []
    src = [refs.pop(0) for _ in range(n_jobs)]
    proj_ref, side_ref = refs.pop(0), refs.pop(0)
    sample_out = [refs.pop(0) for _ in range(2)] if with_sample else []
    dst_refs = [refs.pop(0) for _ in range(n_jobs)]
    xn_ref = refs.pop(0)

    if with_sample:
        @pl.when(pl.program_id(0) == 0)
        def _():
            hs_ref, coss_ref, sins_ref = sample_in
            _proj_rows(hs_ref, g_ref, w_ref, coss_ref, sins_ref, *sample_out, refs[0])

    xn_ref[...] = (_rms_scale(h_ref[...]) * g_ref[...]).astype(BF16)
    outs = {"proj": proj_ref, "side": side_ref}
    n_t = len(PROJ_TILES)
    for t, (dst, col, mode) in enumerate(PROJ_TILES):
        acc = _dot(xn_ref[...], w_ref[:, t * PROJ_TN:(t + 1) * PROJ_TN])
        _proj_epilogue(acc, mode, cos_ref, sin_ref, outs[dst], col)
        if n_jobs >= 2:
            wgu_ref, wd_ref = src[:2]
            wgubf_ref, wdbf_ref = dst_refs[:2]
            gu_cols = wgu_ref.shape[1] // n_t
            for j in range(t * gu_cols // FFN_TF, (t + 1) * gu_cols // FFN_TF):
                c = (j % FFN_TILES) * 2 * FFN_TF + (j // FFN_TILES) * FFN_TF
                wgubf_ref[:, c:c + FFN_TF] = wgu_ref[:, j * FFN_TF:(j + 1) * FFN_TF].astype(BF16)
            d_rows = wd_ref.shape[0] // n_t
            wdbf_ref[t * d_rows:(t + 1) * d_rows] = wd_ref[t * d_rows:(t + 1) * d_rows].astype(BF16)
        if n_jobs == 4:
            win_ref, wo_ref = src[2:]
            winbf_ref, wobf_ref = dst_refs[2:]
            winbf_ref[:, t * PROJ_TN:(t + 1) * PROJ_TN] = win_ref[:, t * PROJ_TN:(t + 1) * PROJ_TN].astype(BF16)
            if t == 0:
                wobf_ref[...] = wo_ref[...].astype(BF16)


def _proj_call(h, g, layer, w_in_bf, cos, sin, tm, convert=(), sample=None):
    m = h.shape[0]
    n_steps = m // tm
    cos_blocks = cos.shape[0] // tm
    n_jobs = len(convert)
    assert n_jobs in (0, 2, 4)
    sample_in, sample_out, sample_shape, sample_scratch = [], [], [], []
    if sample is not None:
        ms = sample[0].shape[0]
        sample_in = [_resident(a.shape, lambda i: (0, 0)) for a in sample]
        sample_out = [_resident((ms, PROJ_COLS), lambda i: (0, 0)), _resident((ms, SIDE_COLS), lambda i: (0, 0))]
        sample_shape = [jax.ShapeDtypeStruct((ms, PROJ_COLS), BF16), jax.ShapeDtypeStruct((ms, SIDE_COLS), F32)]
        sample_scratch = [pltpu.VMEM((ms, D_MODEL), BF16)]
    slab_in, slab_out, slab_shape = [], [], []
    for w, w_layer in convert:
        rows, cols = w.shape[1] // n_steps, w.shape[2]
        assert w.shape[1] % n_steps == 0 and rows % 16 == 0
        slab_in.append(pl.BlockSpec((None, rows, cols), lambda i, w_layer=w_layer: (w_layer, i, 0)))
        slab_out.append(pl.BlockSpec((rows, cols), lambda i: (i, 0)))
        slab_shape.append(jax.ShapeDtypeStruct(w.shape[1:], BF16))
    return pl.pallas_call(
        functools.partial(_proj_kernel, n_jobs=n_jobs, with_sample=sample is not None),
        grid=(n_steps,),
        in_specs=[
            pl.BlockSpec((tm, D_MODEL), lambda i: (i, 0)),
            _resident((None, 1, D_MODEL), lambda i: (layer, 0, 0)),
            _resident((D_MODEL, IN_COLS), lambda i: (0, 0)),
            pl.BlockSpec((tm, RET_KEY_DIM // 2), lambda i: (i % cos_blocks, 0)),
            pl.BlockSpec((tm, RET_KEY_DIM // 2), lambda i: (i % cos_blocks, 0)),
        ] + sample_in + slab_in,
        out_specs=[
            pl.BlockSpec((tm, PROJ_COLS), lambda i: (i, 0)),
            pl.BlockSpec((tm, SIDE_COLS), lambda i: (i, 0)),
        ] + sample_out + slab_out,
        out_shape=[jax.ShapeDtypeStruct((m, PROJ_COLS), BF16),
                   jax.ShapeDtypeStruct((m, SIDE_COLS), F32)] + sample_shape + slab_shape,
        scratch_shapes=[pltpu.VMEM((tm, D_MODEL), BF16)] + sample_scratch,
        compiler_params=pltpu.CompilerParams(
            dimension_semantics=("arbitrary",), vmem_limit_bytes=VMEM_LIMIT_BYTES),
        name="proj",
    )(h, g, w_in_bf, cos, sin, *(sample or ()), *[w for w, _ in convert])


def _proj_cast_kernel(h_ref, g_ref, w_ref, cos_ref, sin_ref, proj_ref, side_ref, wbf_ref, xn_ref):
    j = pl.program_id(0)

    @pl.when(j == 0)
    def _():
        xn_ref[...] = (_rms_scale(h_ref[...]) * g_ref[...]).astype(BF16)

    w_bf = w_ref[...].astype(BF16)
    wbf_ref[...] = w_bf
    acc = _dot(xn_ref[...], w_bf)
    outs = {"proj": proj_ref, "side": side_ref}
    for t, (dst, col, mode) in enumerate(PROJ_TILES):
        @pl.when(j == t)
        def _():
            _proj_epilogue(acc, mode, cos_ref, sin_ref, outs[dst], col)


def _proj_cast_call(h, g, layer, w_in, cos, sin):
    m = h.shape[0]
    full = lambda shape: _resident(shape, lambda j: (0,) * len(shape))
    return pl.pallas_call(
        _proj_cast_kernel,
        grid=(len(PROJ_TILES),),
        in_specs=[
            full((m, D_MODEL)),
            _resident((None, 1, D_MODEL), lambda j: (layer, 0, 0)),
            pl.BlockSpec((None, D_MODEL, PROJ_TN), lambda j: (layer, 0, j)),
            full((m, RET_KEY_DIM // 2)),
            full((m, RET_KEY_DIM // 2)),
        ],
        out_specs=[
            pl.BlockSpec((m, PROJ_COLS), lambda j: (0, 0)),
            pl.BlockSpec((m, SIDE_COLS), lambda j: (0, 0)),
            pl.BlockSpec((D_MODEL, PROJ_TN), lambda j: (0, j)),
        ],
        out_shape=[jax.ShapeDtypeStruct((m, PROJ_COLS), BF16),
                   jax.ShapeDtypeStruct((m, SIDE_COLS), F32),
                   jax.ShapeDtypeStruct((D_MODEL, IN_COLS), BF16)],
        scratch_shapes=[pltpu.VMEM((m, D_MODEL), BF16)],
        compiler_params=pltpu.CompilerParams(
            dimension_semantics=("arbitrary",), vmem_limit_bytes=VMEM_LIMIT_BYTES),
        name="proj_cast",
    )(h, g, w_in, cos, sin)


def _softmax_sink(s, sink):
    m = jnp.maximum(jnp.max(s, axis=-1, keepdims=True), sink)
    p = jnp.exp(s - m)
    return p, 1.0 / (jnp.sum(p, axis=-1, keepdims=True) + jnp.exp(sink - m))


def _gate_out(o, g):
    return (_rms_scale(o) * _silu(g)).astype(BF16)


WO_BATCHES = 1


def _mix_prompt_kernel(sink_ref, cdec_ref, p_ref, s_ref,
                       bias_ref, dmask_ref, qdec_ref, kdec_ref, wo_ref, h_ref, *refs, with_sample):
    refs = list(refs)
    sample_in = [refs.pop(0) for _ in range(2)] if with_sample else []
    out_ref, st_ref = refs.pop(0), refs.pop(0)
    sample_out = [refs.pop(0)] if with_sample else []
    kvp_ref, cat_refs = refs[0], refs[1:]
    n = pl.program_id(0)

    @pl.when(n == 0)
    def _():
        st_ref[...] = jnp.zeros(st_ref.shape, F32)
        kvp_ref[...] = jnp.zeros(kvp_ref.shape, BF16)
        if with_sample:
            cats_ref, hs_ref = sample_in
            sample_out[0][...] = hs_ref[...] + _dot(cats_ref[...], wo_ref[...])

    scale = SWA_HEAD_DIM ** -0.5
    v_off = SWA_KV_HEADS * SWA_HEAD_DIM

    def cat_of(b):
        return cat_refs[b // WO_BATCHES].at[b % WO_BATCHES]

    def wo_rows(b0):
        bs = slice(b0, b0 + WO_BATCHES)
        cat = cat_refs[b0 // WO_BATCHES][...].reshape(WO_BATCHES * BLOCK, D_MODEL)
        out_ref[bs] = h_ref[bs] + _dot(cat, wo_ref[...]).reshape(WO_BATCHES, BLOCK, D_MODEL)

    def attention(b, kh):
        c0 = kh * SWA_HEAD_DIM
        k_cur = s_ref[b, :, KV_COL + c0:KV_COL + c0 + SWA_HEAD_DIM].astype(BF16)
        v_cur = s_ref[b, :, KV_COL + v_off + c0:KV_COL + v_off + c0 + SWA_HEAD_DIM].astype(BF16)
        k_cat = jnp.concatenate([kvp_ref[b, :, c0:c0 + SWA_HEAD_DIM], k_cur], axis=0)
        v_cat = jnp.concatenate([kvp_ref[b, :, v_off + c0:v_off + c0 + SWA_HEAD_DIM], v_cur], axis=0)
        kvp_ref[b, :, c0:c0 + SWA_HEAD_DIM] = k_cur
        kvp_ref[b, :, v_off + c0:v_off + c0 + SWA_HEAD_DIM] = v_cur
        for gq in range(SWA_GROUP):
            h = kh * SWA_GROUP + gq
            q = p_ref[b, :, h * SWA_HEAD_DIM:(h + 1) * SWA_HEAD_DIM]
            s = _dot_nt(q, k_cat) * scale + bias_ref[h]
            p, inv = _softmax_sink(s, sink_ref[h])
            o = _dot(p.astype(BF16), v_cat) * inv
            cat_of(b)[:, h * SWA_HEAD_DIM:(h + 1) * SWA_HEAD_DIM] = o.astype(BF16)

    def retention(b, h):
        cs = slice(h * RET_KEY_DIM, (h + 1) * RET_KEY_DIM)
        q = p_ref[b, :, QR_COL + h * RET_KEY_DIM:QR_COL + (h + 1) * RET_KEY_DIM]
        k32 = s_ref[b, :, cs]
        v = p_ref[b, :, VR_COL + h * RET_VAL_DIM:VR_COL + (h + 1) * RET_VAL_DIM]
        s = _dot_nt(q, k32.astype(BF16)) * dmask_ref[h]
        o = _dot(s.astype(BF16), v)
        st = st_ref[b, h]
        o = o + _dot(q, st.astype(BF16)) * qdec_ref[:, cs]
        kd = (k32 * kdec_ref[:, cs]).astype(BF16)
        st_ref[b, h] = cdec_ref[h] * st + _dot_tn(kd, v)
        cat_of(b)[:, SWA_WIDTH + h * RET_VAL_DIM:SWA_WIDTH + (h + 1) * RET_VAL_DIM] = (
            _gate_out(o, s_ref[b, :, G_COL + h * RET_VAL_DIM:G_COL + (h + 1) * RET_VAL_DIM]))

    for b in range(BATCH):
        for kh in range(SWA_KV_HEADS):
            attention(b, kh)
        for h in range(RET_HEADS):
            retention(b, h)
        if (b + 1) % WO_BATCHES == 0:
            wo_rows(b + 1 - WO_BATCHES)


def _mix_prompt_call(proj, side, h, w_o_bf, sinks_l, cdec, bias, dmask, qdec, kdec, sample=None):
    nblk = SEQ // BLOCK
    proj3 = proj.reshape(BATCH, SEQ, PROJ_COLS)
    side3 = side.reshape(BATCH, SEQ, SIDE_COLS)
    h3 = h.reshape(BATCH, SEQ, D_MODEL)
    smem = pl.BlockSpec(memory_space=pltpu.SMEM)
    full = lambda shape: _resident(shape, lambda n: (0,) * len(shape))
    sample_in, sample_out, sample_shape = [], [], []
    if sample is not None:
        sample_in = [full(a.shape) for a in sample]
        sample_out = [full(sample[1].shape)]
        sample_shape = [jax.ShapeDtypeStruct(sample[1].shape, F32)]
    out, st, *hs_new = pl.pallas_call(
        functools.partial(_mix_prompt_kernel, with_sample=sample is not None),
        grid=(nblk,),
        in_specs=[
            smem, smem,
            pl.BlockSpec((BATCH, BLOCK, PROJ_COLS), lambda n: (0, n, 0)),
            pl.BlockSpec((BATCH, BLOCK, SIDE_COLS), lambda n: (0, n, 0)),
            pl.BlockSpec((None, SWA_HEADS, BLOCK, 2 * BLOCK), lambda n: (jnp.minimum(n, 1), 0, 0, 0)),
            full((RET_HEADS, RET_CHUNK, RET_CHUNK)),
            full((RET_CHUNK, RET_WIDTH)),
            full((RET_CHUNK, RET_WIDTH)),
            full((D_MODEL, D_MODEL)),
            pl.BlockSpec((BATCH, BLOCK, D_MODEL), lambda n: (0, n, 0)),
        ] + sample_in,
        out_specs=[
            pl.BlockSpec((BATCH, BLOCK, D_MODEL), lambda n: (0, n, 0)),
            _resident((BATCH, RET_HEADS, RET_KEY_DIM, RET_VAL_DIM), lambda n: (0, 0, 0, 0)),
        ] + sample_out,
        out_shape=[jax.ShapeDtypeStruct((BATCH, SEQ, D_MODEL), F32),
                   jax.ShapeDtypeStruct((BATCH, RET_HEADS, RET_KEY_DIM, RET_VAL_DIM), F32)] + sample_shape,
        scratch_shapes=[pltpu.VMEM((BATCH, BLOCK, 2 * SWA_KV_HEADS * SWA_HEAD_DIM), BF16)]
        + [pltpu.VMEM((WO_BATCHES, BLOCK, D_MODEL), BF16)] * (BATCH // WO_BATCHES),
        compiler_params=pltpu.CompilerParams(
            dimension_semantics=("arbitrary",), vmem_limit_bytes=VMEM_LIMIT_BYTES),
        name="mix_prompt",
    )(sinks_l, cdec, proj3, side3, bias, dmask, qdec, kdec, w_o_bf, h3, *(sample or ()))
    return (out.reshape(BATCH * SEQ, D_MODEL), st, *hs_new)


SAMPLE_BB = 4
SAMPLE_ROWS = SAMPLE_BB * DEC_SEQ
N_SAMPLE = DEC_BATCH * DEC_SEQ
CACHE_ROWS = WINDOW * SWA_KV_HEADS
STATE_BUFS = 3
assert DEC_BATCH // SAMPLE_BB >= STATE_BUFS


def _mix_sample_kernel(sink_ref, cdec_ref, prow_ref, srow_ref, pall_ref, sall_ref, ck_ref, cv_ref,
                       knew_ref, vnew_ref, st_hbm, bias_ref, dmask_ref, qdec_ref, kdec_ref, *rest, layer):
    cat_ref, stout_ref, ckout_ref, cvout_ref, st_buf, st_sem = rest[-6:]

    c = pl.program_id(0)
    n_steps = pl.num_programs(0)

    def state_copy(step, slot):
        return pltpu.make_async_copy(st_hbm.at[layer, pl.ds(step * SAMPLE_BB, SAMPLE_BB)],
                                     st_buf.at[slot], st_sem.at[slot])

    @pl.when(c == 0)
    def _():
        for s in range(STATE_BUFS):
            state_copy(s, s).start()

    slot = c % STATE_BUFS
    state_copy(c, slot).wait()
    st_ref = st_buf.at[slot]

    if layer == 0:
        for ref in (stout_ref, ckout_ref, cvout_ref):
            ref[1:] = jnp.zeros((DEPTH - 1,) + ref.shape[1:], F32)
        stout_ref, ckout_ref, cvout_ref = stout_ref.at[0], ckout_ref.at[0], cvout_ref.at[0]
    r = SAMPLE_ROWS
    row_b = lax.broadcasted_iota(jnp.int32, (r, 1), 0) // DEC_SEQ
    row_b4 = lax.broadcasted_iota(jnp.int32, (SWA_GROUP * r, 1), 0) % r // DEC_SEQ
    scale = SWA_HEAD_DIM ** -0.5
    v_off = SWA_KV_HEADS * SWA_HEAD_DIM
    new_rows = DEC_SEQ * SWA_KV_HEADS

    for bi in range(SAMPLE_BB):
        ckout_ref[bi, :CACHE_ROWS - new_rows] = ck_ref[bi, new_rows:]
        ckout_ref[bi, CACHE_ROWS - new_rows:] = knew_ref[bi]
        cvout_ref[bi, :CACHE_ROWS - new_rows] = cv_ref[bi, new_rows:]
        cvout_ref[bi, CACHE_ROWS - new_rows:] = vnew_ref[bi]

    for kh in range(SWA_KV_HEADS):
        c0 = kh * SWA_HEAD_DIM
        q4 = jnp.concatenate(
            [prow_ref[:, (kh * SWA_GROUP + gq) * SWA_HEAD_DIM:(kh * SWA_GROUP + gq + 1) * SWA_HEAD_DIM]
             for gq in range(SWA_GROUP)], axis=0)
        k_new = sall_ref[:, KV_COL + c0:KV_COL + c0 + SWA_HEAD_DIM].astype(BF16)
        v_new = sall_ref[:, KV_COL + v_off + c0:KV_COL + v_off + c0 + SWA_HEAD_DIM].astype(BF16)
        s_cache = jnp.zeros((SWA_GROUP * r, CACHE_ROWS), F32)
        for bi in range(SAMPLE_BB):
            s_cache = jnp.where(row_b4 == bi, _dot_nt(q4, ck_ref[bi].astype(BF16)), s_cache)
        s_new = _dot_nt(q4, k_new)
        s4 = jnp.concatenate([s_cache, s_new], axis=1) * scale
        p_parts, inv_parts = [], []
        for gq in range(SWA_GROUP):
            h = kh * SWA_GROUP + gq
            s = s4[gq * r:(gq + 1) * r] + bias_ref[kh, h]
            p, inv = _softmax_sink(s, sink_ref[h])
            p_parts.append(p)
            inv_parts.append(inv)
        p4 = jnp.concatenate(p_parts, axis=0)
        inv4 = jnp.concatenate(inv_parts, axis=0)
        p_cache = p4[:, :CACHE_ROWS]
        o4 = _dot(p4[:, CACHE_ROWS:].astype(BF16), v_new)
        for bi in range(SAMPLE_BB):
            o4 = o4 + _dot(jnp.where(row_b4 == bi, p_cache, 0.0).astype(BF16), cv_ref[bi].astype(BF16))
        o4 = o4 * inv4
        for gq in range(SWA_GROUP):
            h = kh * SWA_GROUP + gq
            cat_ref[:, h * SWA_HEAD_DIM:(h + 1) * SWA_HEAD_DIM] = o4[gq * r:(gq + 1) * r].astype(BF16)

    for h in range(RET_HEADS):
        cs = slice(h * RET_KEY_DIM, (h + 1) * RET_KEY_DIM)
        q = prow_ref[:, 1024 + h * RET_KEY_DIM:1024 + (h + 1) * RET_KEY_DIM]
        k_all = sall_ref[:, cs].astype(BF16)
        v_all = pall_ref[:, 2048 + h * RET_VAL_DIM:2048 + (h + 1) * RET_VAL_DIM]
        s = _dot_nt(q, k_all) * dmask_ref[h]
        o = _dot(s.astype(BF16), v_all)
        k32 = srow_ref[:, cs]
        v = prow_ref[:, 2048 + h * RET_VAL_DIM:2048 + (h + 1) * RET_VAL_DIM]
        kd = k32 * kdec_ref[:, cs]
        cross = jnp.zeros((r, RET_VAL_DIM), F32)
        for bi in range(SAMPLE_BB):
            st = st_ref[bi, h]
            cross = jnp.where(row_b == bi, _dot(q, st.astype(BF16)), cross)
            kd_b = jnp.where(row_b == bi, kd, 0.0).astype(BF16)
            stout_ref[bi, h] = cdec_ref[h] * st + _dot_tn(kd_b, v)
        o = o + cross * qdec_ref[:, cs]
        g = srow_ref[:, 1024 + h * RET_VAL_DIM:1024 + (h + 1) * RET_VAL_DIM]
        cat_ref[:, SWA_WIDTH + h * RET_VAL_DIM:SWA_WIDTH + (h + 1) * RET_VAL_DIM] = _gate_out(o, g)

    @pl.when(c + STATE_BUFS < n_steps)
    def _():
        state_copy(c + STATE_BUFS, slot).start()


def _mix_sample_call(proj, side, cache_k, cache_v, state_ret, layer, sinks_l, cdec, bias, dmask, qdec, kdec,
                     stacked):
    r = SAMPLE_ROWS
    new_rows = DEC_SEQ * SWA_KV_HEADS
    kv_w = SWA_KV_HEADS * SWA_HEAD_DIM
    k_new = side[:, KV_COL:KV_COL + kv_w].reshape(DEC_BATCH, new_rows, SWA_HEAD_DIM)
    v_new = side[:, KV_COL + kv_w:KV_COL + 2 * kv_w].reshape(DEC_BATCH, new_rows, SWA_HEAD_DIM)
    smem = pl.BlockSpec(memory_space=pltpu.SMEM)
    full = lambda shape: pl.BlockSpec(shape, lambda c: (0,) * len(shape))
    cache_in = pl.BlockSpec((None, SAMPLE_BB, CACHE_ROWS, SWA_HEAD_DIM), lambda c: (layer, c, 0, 0))
    state_dims = (SAMPLE_BB, RET_HEADS, RET_KEY_DIM, RET_VAL_DIM)
    cache_dims = (SAMPLE_BB, CACHE_ROWS, SWA_HEAD_DIM)
    if layer == 0:
        out_block = lambda dims: pl.BlockSpec((DEPTH,) + dims, lambda c: (0, c) + (0,) * (len(dims) - 1))
    else:
        out_block = lambda dims: pl.BlockSpec((None,) + dims, lambda c: (layer, c) + (0,) * (len(dims) - 1))
    new_in = pl.BlockSpec((SAMPLE_BB, new_rows, SWA_HEAD_DIM), lambda c: (c, 0, 0))
    in_specs = [
        smem, smem,
        pl.BlockSpec((r, PROJ_COLS), lambda c: (c, 0)),
        pl.BlockSpec((r, SIDE_COLS), lambda c: (c, 0)),
        full((N_SAMPLE, PROJ_COLS)),
        full((N_SAMPLE, SIDE_COLS)),
        cache_in, cache_in, new_in, new_in,
        pl.BlockSpec(memory_space=pl.ANY),
        pl.BlockSpec((SWA_KV_HEADS, SWA_HEADS, r, CACHE_ROWS + N_SAMPLE), lambda c: (0, 0, c, 0)),
        pl.BlockSpec((RET_HEADS, r, N_SAMPLE), lambda c: (0, c, 0)),
        pl.BlockSpec((r, RET_WIDTH), lambda c: (c, 0)),
        pl.BlockSpec((r, RET_WIDTH), lambda c: (c, 0)),
    ]
    args = [sinks_l, cdec, proj, side, proj, side, cache_k, cache_v, k_new, v_new, state_ret, bias, dmask, qdec,
            kdec]
    aliases = {}
    if stacked is not None:
        for k, buf in enumerate(stacked):
            aliases[len(args)] = 1 + k
            in_specs.append(pl.BlockSpec(memory_space=pl.ANY))
            args.append(buf)
    cache_shape = jax.ShapeDtypeStruct((DEPTH, DEC_BATCH, CACHE_ROWS, SWA_HEAD_DIM), F32)
    cat, *new_stacked = pl.pallas_call(
        functools.partial(_mix_sample_kernel, layer=layer),
        grid=(DEC_BATCH // SAMPLE_BB,),
        in_specs=in_specs,
        out_specs=[
            pl.BlockSpec((r, D_MODEL), lambda c: (c, 0)),
            out_block(state_dims), out_block(cache_dims), out_block(cache_dims),
        ],
        out_shape=[jax.ShapeDtypeStruct((N_SAMPLE, D_MODEL), BF16),
                   jax.ShapeDtypeStruct((DEPTH, DEC_BATCH, RET_HEADS, RET_KEY_DIM, RET_VAL_DIM), F32),
                   cache_shape, cache_shape],
        input_output_aliases=aliases,
        scratch_shapes=[pltpu.VMEM((STATE_BUFS,) + state_dims, F32), pltpu.SemaphoreType.DMA((STATE_BUFS,))],
        compiler_params=pltpu.CompilerParams(
            dimension_semantics=("arbitrary",), vmem_limit_bytes=VMEM_LIMIT_BYTES),
        name="mix_sample",
    )(*args)
    return cat, tuple(new_stacked)


WO_TN = 512


def _wo_cast_kernel(cat_ref, w_ref, h_ref, out_ref, wbf_ref):
    w_bf = w_ref[...].astype(BF16)
    wbf_ref[...] = w_bf
    out_ref[...] = h_ref[...] + _dot(cat_ref[...], w_bf)


def _wo_cast_call(cat, w_o, layer, h):
    m = h.shape[0]
    return pl.pallas_call(
        _wo_cast_kernel,
        grid=(D_MODEL // WO_TN,),
        in_specs=[
            _resident((m, D_MODEL), lambda j: (0, 0)),
            pl.BlockSpec((None, D_MODEL, WO_TN), lambda j: (layer, 0, j)),
            pl.BlockSpec((m, WO_TN), lambda j: (0, j)),
        ],
        out_specs=[pl.BlockSpec((m, WO_TN), lambda j: (0, j)),
                   pl.BlockSpec((D_MODEL, WO_TN), lambda j: (0, j))],
        out_shape=[jax.ShapeDtypeStruct((m, D_MODEL), F32),
                   jax.ShapeDtypeStruct((D_MODEL, D_MODEL), BF16)],
        compiler_params=pltpu.CompilerParams(
            dimension_semantics=("arbitrary",), vmem_limit_bytes=VMEM_LIMIT_BYTES),
        name="wo_cast",
    )(cat, w_o, h)


def _ffn_tile(f, n_f, h_ref, g_ref, wgu_ref, wd_ref, gfin_ref, out_ref, xn_ref, final_norm):
    @pl.when(f == 0)
    def _():
        x = h_ref[...]
        xn_ref[...] = (_rms_scale(x) * g_ref[...]).astype(BF16)
        out_ref[...] = x

    ab = _dot(xn_ref[...], wgu_ref[...])
    act = (_silu(ab[:, :FFN_TF]) * ab[:, FFN_TF:]).astype(BF16)
    out_ref[...] += _dot(act, wd_ref[...])

    if final_norm:
        @pl.when(f == n_f - 1)
        def _():
            out_ref[...] = _rms_scale(out_ref[...]) * gfin_ref[...]


def _ffn_kernel(h_ref, g_ref, wgu_ref, wd_ref, gfin_ref, hs_ref, out_ref, outs_ref, xn_ref, xns_ref, *,
                final_norm):
    i, f, n_f = pl.program_id(0), pl.program_id(1), pl.num_programs(1)
    _ffn_tile(f, n_f, h_ref, g_ref, wgu_ref, wd_ref, gfin_ref, out_ref, xn_ref, final_norm)

    @pl.when(i == 0)
    def _():
        _ffn_tile(f, n_f, hs_ref, g_ref, wgu_ref, wd_ref, gfin_ref, outs_ref, xns_ref, final_norm)


FFN_VMEM_LIMIT_BYTES = 62 * 1024 * 1024


def _ffn_call(h, h_sample, g, w_gu_bf, w_d_bf, g_final, layer, tm):
    m, ms = h.shape[0], h_sample.shape[0]
    return pl.pallas_call(
        functools.partial(_ffn_kernel, final_norm=(layer == DEPTH - 1)),
        grid=(m // tm, FFN_TILES),
        in_specs=[
            pl.BlockSpec((tm, D_MODEL), lambda i, f: (i, 0)),
            pl.BlockSpec((None, 1, D_MODEL), lambda i, f: (layer, 0, 0)),
            pl.BlockSpec((D_MODEL, 2 * FFN_TF), lambda i, f: (0, f)),
            pl.BlockSpec((FFN_TF, D_MODEL), lambda i, f: (f, 0)),
            pl.BlockSpec((1, D_MODEL), lambda i, f: (0, 0)),
            _resident((ms, D_MODEL), lambda i, f: (0, 0)),
        ],
        out_specs=[pl.BlockSpec((tm, D_MODEL), lambda i, f: (i, 0)),
                   _resident((ms, D_MODEL), lambda i, f: (0, 0))],
        out_shape=[jax.ShapeDtypeStruct((m, D_MODEL), F32),
                   jax.ShapeDtypeStruct((ms, D_MODEL), F32)],
        scratch_shapes=[pltpu.VMEM((tm, D_MODEL), BF16), pltpu.VMEM((ms, D_MODEL), BF16)],
        compiler_params=pltpu.CompilerParams(
            dimension_semantics=("arbitrary", "arbitrary"), vmem_limit_bytes=FFN_VMEM_LIMIT_BYTES),
        name="ffn",
    )(h, g, w_gu_bf, w_d_bf, g_final, h_sample)


def _rope_tables(pos):
    half = RET_KEY_DIM // 2
    inv = 1.0 / (ROPE_BASE ** jnp.linspace(0.0, 1.0, half, dtype=F32))
    ang = pos.astype(F32)[:, None] * inv[None, :]
    return jnp.cos(ang), jnp.sin(ang)


def _decay_tables(c):
    lg = jnp.log(1.0 - jnp.exp2(-5.0 - jnp.arange(RET_HEADS, dtype=F32)))
    idx = jnp.arange(c, dtype=F32)
    diff = idx[:, None] - idx[None, :]
    dmask = jnp.where(diff[None] >= 0, jnp.exp(jnp.maximum(diff, 0.0)[None] * lg[:, None, None]), 0.0)
    q_decay = jnp.exp((idx[:, None] + 1.0) * lg[None, :])
    k_decay = jnp.exp((c - 1.0 - idx)[:, None] * lg[None, :])
    c_decay = jnp.exp(c * lg)
    return dmask, q_decay, k_decay, c_decay


def _per_head_cols(t):
    return jnp.repeat(t, RET_KEY_DIM, axis=1)


def kernel(x_prompt, x_sample, cache_k_win, cache_v_win, state_ret, rel_bias, w_in, sinks, w_o,
           norm_mix, norm_ffn, w_gate_up, w_down, norm_final):
    norm_mix3 = norm_mix.reshape(DEPTH, 1, D_MODEL)
    norm_ffn3 = norm_ffn.reshape(DEPTH, 1, D_MODEL)
    norm_final2 = norm_final.reshape(1, D_MODEL)
    cache_k = cache_k_win.reshape(DEPTH, DEC_BATCH, CACHE_ROWS, SWA_HEAD_DIM)
    cache_v = cache_v_win.reshape(DEPTH, DEC_BATCH, CACHE_ROWS, SWA_HEAD_DIM)

    cos_p, sin_p = _rope_tables(jnp.arange(SEQ, dtype=jnp.int32))
    cos_4, sin_4 = _rope_tables(PAST_LEN + jnp.arange(DEC_SEQ, dtype=jnp.int32))
    cos_s = jnp.tile(cos_4, (DEC_BATCH, 1))
    sin_s = jnp.tile(sin_4, (DEC_BATCH, 1))

    dmask_p, qd_p, kd_p, cdec_p = _decay_tables(RET_CHUNK)
    qdec_p = _per_head_cols(qd_p)
    kdec_p = _per_head_cols(kd_p)
    dmask_4, qd_4, kd_4, cdec_s = _decay_tables(DEC_SEQ)
    eye_b = jnp.eye(DEC_BATCH, dtype=F32)
    dmask_s = jax.vmap(lambda d: jnp.kron(eye_b, d))(dmask_4)
    qdec_s = jnp.tile(_per_head_cols(qd_4), (DEC_BATCH, 1))
    kdec_s = jnp.tile(_per_head_cols(kd_4), (DEC_BATCH, 1))

    qi = np.arange(BLOCK)[:, None]
    kj = np.arange(2 * BLOCK)[None, :]
    delta_p = qi + BLOCK - kj
    in_window = (delta_p >= 0) & (delta_p < WINDOW)
    valid_p = np.stack([in_window & (kj >= BLOCK), in_window]).astype(np.float32)
    rows = np.arange(N_SAMPLE)
    rb, rt = rows // DEC_SEQ, rows % DEC_SEQ
    cache_pos = np.arange(CACHE_ROWS) // SWA_KV_HEADS
    cache_head = np.arange(CACHE_ROWS) % SWA_KV_HEADS
    delta_cache = (WINDOW + rt)[:, None] - cache_pos[None, :]
    delta_new = rt[:, None] - rt[None, :]
    same_b = rb[:, None] == rb[None, :]
    delta_s = np.concatenate([delta_cache, delta_new], axis=1)
    valid_s = np.stack([
        np.concatenate([(delta_cache < WINDOW) & (cache_head == kh)[None, :], same_b & (delta_new >= 0)], axis=1)
        for kh in range(SWA_KV_HEADS)]).astype(np.float32)
    rel_bias_t = rel_bias.T
    bias_p = _expand_bias(rel_bias_t, jnp.asarray(_t5_bucket_np(delta_p)), jnp.asarray(valid_p))
    bias_s = _expand_bias(rel_bias_t, jnp.asarray(_t5_bucket_np(delta_s)), jnp.asarray(valid_s))

    hp = x_prompt.reshape(BATCH * SEQ, D_MODEL)
    hs = x_sample.reshape(N_SAMPLE, D_MODEL)
    tm_proj, tm_ffn = 256, 1024
    kp_new, vp_new, rp_new = [], [], []
    sample_new = None
    kv_w = SWA_KV_HEADS * SWA_HEAD_DIM
    for l in range(DEPTH):
        convert = [(w_gate_up, l), (w_down, l)]
        if l + 1 < DEPTH:
            convert += [(w_in, l + 1), (w_o, l + 1)]
        if l == 0:
            proj_s, side_s, w_in_bf = _proj_cast_call(hs, norm_mix3, l, w_in, cos_s, sin_s)
            proj, side, w_gu_bf, w_d_bf, *next_bf = _proj_call(hp, norm_mix3, l, w_in_bf, cos_p, sin_p, tm_proj,
                                                                convert)
        else:
            proj, side, proj_s, side_s, w_gu_bf, w_d_bf, *next_bf = _proj_call(
                hp, norm_mix3, l, w_in_bf, cos_p, sin_p, tm_proj, convert, sample=(hs, cos_s, sin_s))

        cat, sample_new = _mix_sample_call(proj_s, side_s, cache_k, cache_v, state_ret, l, sinks[l], cdec_s,
                                           bias_s, dmask_s, qdec_s, kdec_s, sample_new)
        if l == 0:
            hs, w_o_bf = _wo_cast_call(cat, w_o, l, hs)
            hp, st_p = _mix_prompt_call(proj, side, hp, w_o_bf, sinks[l], cdec_p, bias_p, dmask_p, qdec_p, kdec_p)
        else:
            hp, st_p, hs = _mix_prompt_call(proj, side, hp, w_o_bf, sinks[l], cdec_p, bias_p, dmask_p, qdec_p,
                                            kdec_p, sample=(cat, hs))
        hp, hs = _ffn_call(hp, hs, norm_ffn3, w_gu_bf, w_d_bf, norm_final2, l, tm_ffn)
        kv_tail = side.reshape(BATCH, SEQ, SIDE_COLS)[:, SEQ - WINDOW:, KV_COL:KV_COL + 2 * kv_w]
        kp_new.append(kv_tail[..., :kv_w].reshape(BATCH, WINDOW, SWA_KV_HEADS, SWA_HEAD_DIM))
        vp_new.append(kv_tail[..., kv_w:].reshape(BATCH, WINDOW, SWA_KV_HEADS, SWA_HEAD_DIM))
        rp_new.append(st_p)
        if next_bf:
            w_in_bf, w_o_bf = next_bf

    y_prompt = hp.reshape(BATCH, SEQ, D_MODEL)
    y_sample = hs.reshape(DEC_BATCH, DEC_SEQ, D_MODEL)
    rs_new, ks_new, vs_new = sample_new
    return (y_prompt, y_sample,
            jnp.stack(kp_new), jnp.stack(vp_new), jnp.stack(rp_new),
            ks_new.reshape(cache_k_win.shape), vs_new.reshape(cache_v_win.shape), rs_new)
```
